```python
import math
import jax
import jax.numpy as jnp
from jax import lax
import numpy as np

D_MODEL = 2048
BATCH = 1
SEQ = 8192
DEPTH = 2
DEC_BATCH = 16
DEC_SEQ = 64
PAST_LEN = 4096

CHUNK = 64
EPS = 1e-6
N_EVEN = (DEPTH + 1) // 2
N_ODD = DEPTH // 2
MIX_W = D_MODEL
N_MEM = 256
MEM_HEADS = 4
MEM_HD = D_MODEL // MEM_HEADS
A_HEADS = 8
A_DK = 128
A_DV = MIX_W // 2 // A_HEADS
A_KW = A_HEADS * A_DK
A_VW = A_HEADS * A_DV
B_W = MIX_W // 2
B_HEADS = 8
B_BD = B_W // B_HEADS
B_CONV = 4
RG_C = 8.0
C_W = MIX_W // 2
C_HD = 64
C_HEADS = C_W // C_HD
C_STATE = 128
C_GROUPS = 2
C_CONV = 4
C_CONV_DIM = C_W + 2 * C_GROUPS * C_STATE
D_HEADS = 4
D_VAL = MIX_W // 2
D_KEY = D_VAL // 2
D_DK = D_KEY // D_HEADS
D_DV = D_VAL // D_HEADS
D_GATE_RANK = 16
D_GATE_NORM = 16.0
D_FF = 5504
FFN_CONV = 3
EV_SPLIT = (A_KW, A_KW, A_VW, A_VW, B_W, B_W)
OD_SPLIT = (C_W, C_CONV_DIM, C_HEADS, D_KEY, D_KEY, D_VAL, D_VAL, D_GATE_RANK)
EV_IN = sum(EV_SPLIT)
OD_IN = sum(OD_SPLIT)

kernel_name = 'hybrid_streaming_encoder_step'


def rmsnorm(x, g):
    xf = x.astype(jnp.float32)
    y = xf * lax.rsqrt(jnp.mean(xf * xf, axis=-1, keepdims=True) + EPS)
    return (y * g.astype(jnp.float32)).astype(x.dtype)


def split_cols(h, sizes):
    return jnp.split(h, [int(c) for c in np.cumsum(sizes)[:-1]], axis=-1)


def causal_dwconv(x, buf, w, b):
    width = w.shape[0]
    T = x.shape[1]
    xp = jnp.concatenate([buf.astype(x.dtype), x], axis=1)
    y = b
    for j in range(width):
        y = y + w[j] * xp[:, j:j + T]
    return y, xp[:, T:]


def _affine_combine(e1, e2):
    a1, b1 = e1
    a2, b2 = e2
    return a1 * a2, a2 * b1 + b2


def gated_linear_chunked(q, k, v, log_g, s0):
    Bsz, T, H, _ = q.shape
    n = -(-T // CHUNK)
    pad = n * CHUNK - T

    def prep(a):
        a = jnp.pad(a.astype(jnp.float32), ((0, 0), (0, pad), (0, 0), (0, 0)))
        return a.reshape(Bsz, n, CHUNK, H, a.shape[-1]).transpose(1, 0, 3, 2, 4)

    causal = jnp.tril(jnp.ones((CHUNK, CHUNK), dtype=bool))[:, :, None]

    def step(S, blk):
        qb, kb, vb, gb = blk
        b = jnp.cumsum(gb, axis=2)
        decay = jnp.exp(jnp.where(causal, b[:, :, :, None, :] - b[:, :, None, :, :], -jnp.inf))
        scores = jnp.einsum('bhtd,bhsd,bhtsd->bhts', qb, kb, decay)
        o = jnp.einsum('bhts,bhsv->bhtv', scores, vb) + jnp.einsum('bhtd,bhdv->bhtv', qb * jnp.exp(b), S)
        b_end = b[:, :, -1:, :]
        S = jnp.exp(b_end[:, :, 0, :, None]) * S + jnp.einsum('bhsd,bhsv->bhdv', kb * jnp.exp(b_end - b), vb)
        return S, o

    S, o = lax.scan(step, s0.astype(jnp.float32), (prep(q), prep(k), prep(v), prep(log_g)))
    o = o.transpose(1, 0, 3, 2, 4).reshape(Bsz, n * CHUNK, H, -1)[:, :T]
    return o, S


def ssd_chunked(x, dt, A, Bm, Cm, s0):
    Bsz, T, H, P = x.shape
    G, N = Bm.shape[2], Bm.shape[3]
    HG = H // G
    n = -(-T // CHUNK)
    pad = n * CHUNK - T

    def prep(a):
        a = jnp.pad(a.astype(jnp.float32), [(0, 0), (0, pad)] + [(0, 0)] * (a.ndim - 2))
        return jnp.moveaxis(a.reshape((Bsz, n, CHUNK) + a.shape[2:]), 1, 0)

    causal = jnp.tril(jnp.ones((CHUNK, CHUNK), dtype=bool))[None, :, :, None]

    def step(S, blk):
        xb, dtb, bb, cb = blk
        cum = jnp.cumsum(dtb * A, axis=1)
        L = jnp.exp(jnp.where(causal, cum[:, :, None, :] - cum[:, None, :, :], -jnp.inf))
        L = L.reshape(Bsz, CHUNK, CHUNK, G, HG)
        xdt = (xb * dtb[..., None]).reshape(Bsz, CHUNK, G, HG, P)
        cbs = jnp.einsum('btgn,bsgn->bgts', cb, bb)
        y = jnp.einsum('bgts,btsgh,bsghp->btghp', cbs, L, xdt)
        y = y + jnp.einsum('btgn,bghpn->btghp', cb, S) * jnp.exp(cum).reshape(Bsz, CHUNK, G, HG, 1)
        w_end = jnp.exp(cum[:, -1:] - cum).reshape(Bsz, CHUNK, G, HG)
        S = jnp.exp(cum[:, -1]).reshape(Bsz, G, HG, 1, 1) * S + jnp.einsum('bsgn,bsgh,bsghp->bghpn', bb, w_end, xdt)
        return S, y.reshape(Bsz, CHUNK, H, P)

    S0 = s0.astype(jnp.float32).reshape(Bsz, G, HG, P, N)
    S, y = lax.scan(step, S0, (prep(x), prep(dt), prep(Bm), prep(Cm)))
    y = jnp.moveaxis(y, 0, 1).reshape(Bsz, n * CHUNK, H, P)[:, :T]
    return y, S.reshape(Bsz, H, P, N)


def hgrn2(q, f, i, g, lb, norm_g, s0):
    Bsz, T, _ = q.shape
    ff = f.astype(jnp.float32)
    log_f = jnp.logaddexp(jnp.log(lb), jnp.log1p(-lb) + jax.nn.log_sigmoid(ff))
    k = (1.0 - lb) * jax.nn.sigmoid(-ff)
    heads = lambda a, d: a.reshape(Bsz, T, A_HEADS, d)
    o, S = gated_linear_chunked(heads(jax.nn.silu(q), A_DK), heads(k, A_DK), heads(i, A_DV),
                                heads(log_f, A_DK), s0)
    o = rmsnorm(o, norm_g.reshape(A_HEADS, A_DV)).astype(q.dtype) * jax.nn.silu(heads(g, A_DV))
    return o.reshape(Bsz, T, A_VW), S.astype(s0.dtype)


def rglru_block(xb, gate, conv_buf, conv_w, conv_b, w_a, b_a, w_x, b_x, lam, h0, start):
    Bsz, T, _ = xb.shape
    xc, conv_new = causal_dwconv(xb, conv_buf, conv_w, conv_b)
    xblk = xc.reshape(Bsz, T, B_HEADS, B_BD)
    r = jax.nn.sigmoid(jnp.einsum('bthi,hij->bthj', xblk, w_a).reshape(Bsz, T, B_W) + b_a)
    ig = jax.nn.sigmoid(jnp.einsum('bthi,hij->bthj', xblk, w_x).reshape(Bsz, T, B_W) + b_x)
    log_a = RG_C * r.astype(jnp.float32) * jax.nn.log_sigmoid(lam.astype(jnp.float32))
    a = jnp.exp(log_a)
    reset = (start + jnp.arange(T) == 0)[None, :, None]
    mult = jnp.where(reset, 1.0, jnp.sqrt(-jnp.expm1(2.0 * log_a)))
    u = (xc * ig).astype(jnp.float32) * mult
    u = u.at[:, 0].add(a[:, 0] * h0.astype(jnp.float32))
    _, hs = lax.associative_scan(_affine_combine, (a, u), axis=1)
    y = hs.astype(xb.dtype) * jax.nn.gelu(gate)
    return y, hs[:, -1].astype(h0.dtype), conv_new


def mamba2(z, xbc, dt, conv_buf, conv_w, conv_b, dt_bias, a_log, d_skip, norm_g, s0):
    Bsz, T, _ = z.shape
    xbc, conv_new = causal_dwconv(xbc, conv_buf, conv_w, conv_b)
    xs, Bm, Cm = split_cols(jax.nn.silu(xbc), (C_W, C_GROUPS * C_STATE, C_GROUPS * C_STATE))
    xs = xs.reshape(Bsz, T, C_HEADS, C_HD)
    dtp = jax.nn.softplus(dt.astype(jnp.float32) + dt_bias.astype(jnp.float32))
    A = -jnp.exp(a_log.astype(jnp.float32))
    y, S = ssd_chunked(xs, dtp, A, Bm.reshape(Bsz, T, C_GROUPS, C_STATE),
                       Cm.reshape(Bsz, T, C_GROUPS, C_STATE), s0)
    y = (y + d_skip.astype(jnp.float32)[:, None] * xs.astype(jnp.float32)).reshape(Bsz, T, C_W).astype(z.dtype)
    y = rmsnorm((y * jax.nn.silu(z)).reshape(Bsz, T, C_GROUPS, C_W // C_GROUPS),
                norm_g.reshape(C_GROUPS, C_W // C_GROUPS))
    return y.reshape(Bsz, T, C_W), S.astype(s0.dtype), conv_new


def gla(q, k, v, g, gl, gk_w, gk_b, norm_g, s0):
    Bsz, T, _ = q.shape
    log_g = jax.nn.log_sigmoid((gl @ gk_w + gk_b).astype(jnp.float32)) / D_GATE_NORM
    heads = lambda a, d: a.reshape(Bsz, T, D_HEADS, d)
    o, S = gated_linear_chunked(heads(q * D_DK ** -0.5, D_DK), heads(k, D_DK), heads(v, D_DV),
                                heads(log_g, D_DK), s0)
    o = rmsnorm(o, norm_g.reshape(D_HEADS, D_DV)).astype(q.dtype) * jax.nn.silu(heads(g, D_DV))
    return o.reshape(Bsz, T, D_VAL), S.astype(s0.dtype)


def memory_kv(mem, norm_mem, wk, wv):
    mn = rmsnorm(mem[None], norm_mem[:, None, None, :])
    shp = (norm_mem.shape[0], mem.shape[0], mem.shape[1], MEM_HEADS, MEM_HD)
    k = jnp.einsum('lbmd,lde->lbme', mn, wk).reshape(shp)
    v = jnp.einsum('lbmd,lde->lbme', mn, wv).reshape(shp)
    return k, v


def mem_attention(x, mk, mv, wq, wo):
    Bsz, T, D = x.shape
    q = jnp.einsum('btd,dhk->bthk', x, wq.reshape(D, MEM_HEADS, MEM_HD))
    s = jnp.einsum('bthk,bmhk->bhtm', q, mk).astype(jnp.float32) * MEM_HD ** -0.5
    p = jax.nn.softmax(s, axis=-1).astype(x.dtype)
    o = jnp.einsum('bhtm,bmhk->bthk', p, mv).reshape(Bsz, T, D)
    return o @ wo


def conv_ffn(x, buf, w_up, conv_w, conv_b, w_down):
    gate, val = jnp.split(x @ w_up, 2, axis=-1)
    gc, buf_new = causal_dwconv(gate, buf, conv_w, conv_b)
    return (jax.nn.silu(gc) * val) @ w_down, buf_new


def run_trunk(x, start, mem_k, mem_v, states, p):
    st_hgrn, st_rglru, st_rgconv, st_ssd, st_ssdconv, st_gla, st_ffn = states
    lb_all = jnp.cumsum(jax.nn.softmax(p['hgrn_lb_logits'].astype(jnp.float32), axis=0), axis=0)
    n_hgrn, n_rl, n_rc, n_ssd, n_sc, n_gla, n_ffn = [], [], [], [], [], [], []
    for l in range(DEPTH):
        h = rmsnorm(x, p['norm_mix'][l])
        j = l // 2
        if l % 2 == 0:
            aq, af, ai, ag, bx, bg = split_cols(h @ p['ev_w_in'][j], EV_SPLIT)
            o_a, s_a = hgrn2(aq, af, ai, ag, lb_all[l], p['hgrn_norm'][j], st_hgrn[j])
            o_b, h_last, rbuf = rglru_block(bx, bg, st_rgconv[j], p['rg_conv_w'][j], p['rg_conv_b'][j],
                                            p['rg_w_a'][j], p['rg_b_a'][j], p['rg_w_x'][j], p['rg_b_x'][j],
                                            p['rg_lambda'][j], st_rglru[j], start)
            mix = jnp.concatenate([o_a, o_b], axis=-1) @ p['ev_w_out'][j]
            n_hgrn.append(s_a)
            n_rl.append(h_last)
            n_rc.append(rbuf)
        else:
            cz, cxbc, cdt, dq, dk, dv, dg, dgl = split_cols(h @ p['od_w_in'][j], OD_SPLIT)
            o_c, s_c, cbuf = mamba2(cz, cxbc, cdt, st_ssdconv[j], p['ssd_conv_w'][j], p['ssd_conv_b'][j],
                                    p['ssd_dt_bias'][j], p['ssd_A_log'][j], p['ssd_D'][j], p['ssd_norm'][j],
                                    st_ssd[j])
            o_d, s_d = gla(dq, dk, dv, dg, dgl, p['gla_gk_w'][j], p['gla_gk_b'][j], p['gla_norm'][j], st_gla[j])
            mix = jnp.concatenate([o_c, o_d], axis=-1) @ p['od_w_out'][j]
            n_ssd.append(s_c)
            n_sc.append(cbuf)
            n_gla.append(s_d)
        x = x + mix
        x = x + mem_attention(rmsnorm(x, p['norm_xattn'][l]), mem_k[l], mem_v[l], p['xa_wq'][l], p['xa_wo'][l])
        f, fbuf = conv_ffn(rmsnorm(x, p['norm_ffn'][l]), st_ffn[l], p['ffn_up'][l], p['ffn_conv_w'][l],
                           p['ffn_conv_b'][l], p['ffn_down'][l])
        x = x + f
        n_ffn.append(fbuf)
    y = rmsnorm(x, p['final_norm'])
    return y, (jnp.stack(n_hgrn), jnp.stack(n_rl), jnp.stack(n_rc), jnp.stack(n_ssd),
               jnp.stack(n_sc), jnp.stack(n_gla), jnp.stack(n_ffn))


def setup_inputs(seed: int = 0) -> dict:
    key = jax.random.key(seed)
    count = [0]

    def nk():
        count[0] += 1
        return jax.random.fold_in(key, count[0])

    def nrm(shape, scale=1.0):
        return scale * jax.random.normal(nk(), shape, jnp.float32)

    def unif(shape, lo, hi):
        return jax.random.uniform(nk(), shape, jnp.float32, lo, hi)

    def gain(shape):
        return 1.0 + nrm(shape, 0.02)

    D = D_MODEL
    a0 = unif((N_EVEN, B_W), 0.9, 0.999)
    s_rg = a0 ** (1.0 / RG_C)
    dt0 = jnp.exp(unif((N_ODD, C_HEADS), math.log(1e-3), math.log(1e-1)))
    return {
        'x_prompt': nrm((BATCH, SEQ, D)),
        'x_sample': nrm((DEC_BATCH, DEC_SEQ, D)),
        'mem_prompt': nrm((BATCH, N_MEM, D)),
        'cache_mem_k': nrm((DEPTH, DEC_BATCH, N_MEM, MEM_HEADS, MEM_HD)),
        'cache_mem_v': nrm((DEPTH, DEC_BATCH, N_MEM, MEM_HEADS, MEM_HD)),
        'state_hgrn': nrm((N_EVEN, DEC_BATCH, A_HEADS, A_DK, A_DV), 0.5),
        'state_rglru': nrm((N_EVEN, DEC_BATCH, B_W), 0.5),
        'state_rg_conv': nrm((N_EVEN, DEC_BATCH, B_CONV - 1, B_W)),
        'state_ssd': nrm((N_ODD, DEC_BATCH, C_HEADS, C_HD, C_STATE), 0.1),
        'state_ssd_conv': nrm((N_ODD, DEC_BATCH, C_CONV - 1, C_CONV_DIM)),
        'state_gla': nrm((N_ODD, DEC_BATCH, D_HEADS, D_DK, D_DV), 0.5),
        'state_ffn_conv': nrm((DEPTH, DEC_BATCH, FFN_CONV - 1, D_FF)),
        'norm_mix': gain((DEPTH, D)),
        'norm_xattn': gain((DEPTH, D)),
        'norm_mem': gain((DEPTH, D)),
        'norm_ffn': gain((DEPTH, D)),
        'ev_w_in': nrm((N_EVEN, D, EV_IN), D ** -0.5),
        'hgrn_lb_logits': nrm((DEPTH + 1, A_KW), 0.1),
        'hgrn_norm': gain((N_EVEN, A_VW)),
        'rg_conv_w': nrm((N_EVEN, B_CONV, B_W), B_CONV ** -0.5),
        'rg_conv_b': nrm((N_EVEN, B_W), 0.02),
        'rg_w_a': nrm((N_EVEN, B_HEADS, B_BD, B_BD), B_BD ** -0.5),
        'rg_b_a': nrm((N_EVEN, B_W), 0.1),
        'rg_w_x': nrm((N_EVEN, B_HEADS, B_BD, B_BD), B_BD ** -0.5),
        'rg_b_x': nrm((N_EVEN, B_W), 0.1),
        'rg_lambda': jnp.log(s_rg) - jnp.log1p(-s_rg),
        'ev_w_out': nrm((N_EVEN, A_VW + B_W, D), (A_VW + B_W) ** -0.5),
        'od_w_in': nrm((N_ODD, D, OD_IN), D ** -0.5),
        'ssd_conv_w': nrm((N_ODD, C_CONV, C_CONV_DIM), C_CONV ** -0.5),
        'ssd_conv_b': nrm((N_ODD, C_CONV_DIM), 0.02),
        'ssd_dt_bias': dt0 + jnp.log(-jnp.expm1(-dt0)),
        'ssd_A_log': jnp.log(unif((N_ODD, C_HEADS), 1.0, 16.0)),
        'ssd_D': gain((N_ODD, C_HEADS)),
        'ssd_norm': gain((N_ODD, C_W)),
        'gla_gk_w': nrm((N_ODD, D_GATE_RANK, D_KEY), D_GATE_RANK ** -0.5),
        'gla_gk_b': nrm((N_ODD, D_KEY), 0.1),
        'gla_norm': gain((N_ODD, D_VAL)),
        'od_w_out': nrm((N_ODD, C_W + D_VAL, D), (C_W + D_VAL) ** -0.5),
        'xa_wq': nrm((DEPTH, D, D), D ** -0.5),
        'xa_wk': nrm((DEPTH, D, D), D ** -0.5),
        'xa_wv': nrm((DEPTH, D, D), D ** -0.5),
        'xa_wo': nrm((DEPTH, D, D), D ** -0.5),
        'ffn_up': nrm((DEPTH, D, 2 * D_FF), D ** -0.5),
        'ffn_conv_w': nrm((DEPTH, FFN_CONV, D_FF), FFN_CONV ** -0.5),
        'ffn_conv_b': nrm((DEPTH, D_FF), 0.02),
        'ffn_down': nrm((DEPTH, D_FF, D), D_FF ** -0.5),
        'final_norm': gain((D,)),
    }


def reference(x_prompt, x_sample, mem_prompt, cache_mem_k, cache_mem_v, state_hgrn, state_rglru,
              state_rg_conv, state_ssd, state_ssd_conv, state_gla, state_ffn_conv,
              norm_mix, norm_xattn, norm_mem, norm_ffn, ev_w_in, hgrn_lb_logits, hgrn_norm,
              rg_conv_w, rg_conv_b, rg_w_a, rg_b_a, rg_w_x, rg_b_x, rg_lambda, ev_w_out,
              od_w_in, ssd_conv_w, ssd_conv_b, ssd_dt_bias, ssd_A_log, ssd_D, ssd_norm,
              gla_gk_w, gla_gk_b, gla_norm, od_w_out, xa_wq, xa_wk, xa_wv, xa_wo,
              ffn_up, ffn_conv_w, ffn_conv_b, ffn_down, final_norm):
    p = dict(norm_mix=norm_mix, norm_xattn=norm_xattn, norm_ffn=norm_ffn, ev_w_in=ev_w_in,
             hgrn_lb_logits=hgrn_lb_logits, hgrn_norm=hgrn_norm, rg_conv_w=rg_conv_w,
             rg_conv_b=rg_conv_b, rg_w_a=rg_w_a, rg_b_a=rg_b_a, rg_w_x=rg_w_x, rg_b_x=rg_b_x,
             rg_lambda=rg_lambda, ev_w_out=ev_w_out, od_w_in=od_w_in, ssd_conv_w=ssd_conv_w,
             ssd_conv_b=ssd_conv_b, ssd_dt_bias=ssd_dt_bias, ssd_A_log=ssd_A_log, ssd_D=ssd_D,
             ssd_norm=ssd_norm, gla_gk_w=gla_gk_w, gla_gk_b=gla_gk_b, gla_norm=gla_norm,
             od_w_out=od_w_out, xa_wq=xa_wq, xa_wo=xa_wo, ffn_up=ffn_up, ffn_conv_w=ffn_conv_w,
             ffn_conv_b=ffn_conv_b, ffn_down=ffn_down, final_norm=final_norm)
    mem_k_p, mem_v_p = memory_kv(mem_prompt, norm_mem, xa_wk, xa_wv)
    bp = x_prompt.shape[0]
    dtp = x_prompt.dtype
    zeros_p = (jnp.zeros((N_EVEN, bp, A_HEADS, A_DK, A_DV), dtp),
               jnp.zeros((N_EVEN, bp, B_W), dtp),
               jnp.zeros((N_EVEN, bp, B_CONV - 1, B_W), dtp),
               jnp.zeros((N_ODD, bp, C_HEADS, C_HD, C_STATE), dtp),
               jnp.zeros((N_ODD, bp, C_CONV - 1, C_CONV_DIM), dtp),
               jnp.zeros((N_ODD, bp, D_HEADS, D_DK, D_DV), dtp),
               jnp.zeros((DEPTH, bp, FFN_CONV - 1, D_FF), dtp))
    y_prompt, (hg_p, rl_p, rc_p, ss_p, sc_p, gl_p, fc_p) = run_trunk(x_prompt, 0, mem_k_p, mem_v_p, zeros_p, p)
    states_s = (state_hgrn, state_rglru, state_rg_conv, state_ssd, state_ssd_conv, state_gla, state_ffn_conv)
    y_sample, (hg_s, rl_s, rc_s, ss_s, sc_s, gl_s, fc_s) = run_trunk(x_sample, PAST_LEN, cache_mem_k, cache_mem_v,
                                                                     states_s, p)
    return (y_prompt, y_sample, mem_k_p, mem_v_p, hg_p, rl_p, rc_p, ss_p, sc_p, gl_p, fc_p,
            hg_s, rl_s, rc_s, ss_s, sc_s, gl_s, fc_s)
```

```python
import functools
import math

import jax
import jax.numpy as jnp
from jax import lax
from jax.experimental import pallas as pl
from jax.experimental.pallas import tpu as pltpu

F32 = jnp.float32
BF16 = jnp.bfloat16

CHUNK = 64
SUB = 16
EPS = 1e-6
LANES = 128
VMEM_LIMIT = 52 * 1024 * 1024

MEM_HEADS = 4
A_HEADS = 8
B_HEADS = 8
C_HEADS = 16
C_GROUPS = 2
C_STATE = 128
D_HEADS = 4
RG_C = 8.0
D_GATE_NORM = 16.0

NT_DIMS = (((1,), (1,)), ((), ()))
TN_DIMS = (((0,), (0,)), ((), ()))


def _cparams(n_axes):
    return pltpu.CompilerParams(dimension_semantics=("arbitrary",) * n_axes,
                                vmem_limit_bytes=VMEM_LIMIT)


def _sigmoid(x):
    return 1.0 / (1.0 + jnp.exp(-x))


def _silu(x):
    return x * _sigmoid(x)


def _log_sigmoid(x):
    return jnp.minimum(x, 0.0) - jnp.log1p(jnp.exp(-jnp.abs(x)))


def _softplus(x):
    return jnp.maximum(x, 0.0) + jnp.log1p(jnp.exp(-jnp.abs(x)))


def _gelu_tanh(x):
    return 0.5 * x * (1.0 + jnp.tanh(math.sqrt(2.0 / math.pi) * (x + 0.044715 * (x * x * x))))


def _split3(x):
    hi = x.astype(BF16)
    r = x - hi.astype(F32)
    mid = r.astype(BF16)
    lo = (r - mid.astype(F32)).astype(BF16)
    return hi, mid, lo


def _dot01_left(t01, x):
    return sum(jnp.dot(t01, p, preferred_element_type=F32) for p in _split3(x))


def _dot01_right(x, e01):
    return sum(jnp.dot(p, e01, preferred_element_type=F32) for p in _split3(x))


def _tril01(n):
    r = lax.broadcasted_iota(jnp.int32, (n, n), 0)
    c = lax.broadcasted_iota(jnp.int32, (n, n), 1)
    return jnp.where(r >= c, 1.0, 0.0).astype(BF16)


def _causal_conv(x, prev, w, b, nseq, seqlen):
    width = w.shape[0]
    ch = x.shape[-1]
    row = lax.broadcasted_iota(jnp.int32, (nseq, seqlen, ch), 1)
    y = (b + w[width - 1:width] * x).reshape(nseq, seqlen, ch)
    for d in range(1, width):
        xs = pltpu.roll(x, d, axis=0).reshape(nseq, seqlen, ch)
        for r in range(d):
            p = width - 1 - d + r
            xs = jnp.where(row == r, prev[:, p:p + 1, :], xs)
        y = y + w[width - 1 - d:width - d].reshape(1, 1, ch) * xs
    return y.reshape(nseq * seqlen, ch)


def _rmsnorm_rows(x, g):
    ms = jnp.mean(x * x, axis=-1, keepdims=True)
    return x * lax.rsqrt(ms + EPS) * g


def _norm_matmul_body(x_ref, g_ref, w_ref, o_ref, xn_ref):
    @pl.when(pl.program_id(1) == 0)
    def _():
        xn_ref[...] = _rmsnorm_rows(x_ref[...], g_ref[...]).astype(BF16)

    o_ref[...] = jnp.dot(xn_ref[...], w_ref[...], preferred_element_type=F32).astype(o_ref.dtype)


def norm_matmul(x, g, w, *, tm, tn, out_dtype):
    m, d = x.shape
    n = w.shape[1]
    return pl.pallas_call(
        _norm_matmul_body,
        grid=(m // tm, n // tn),
        in_specs=[pl.BlockSpec((tm, d), lambda i, j: (i, 0)),
                  pl.BlockSpec((1, d), lambda i, j: (0, 0)),
                  pl.BlockSpec((d, tn), lambda i, j: (0, j))],
        out_specs=pl.BlockSpec((tm, tn), lambda i, j: (i, j)),
        out_shape=jax.ShapeDtypeStruct((m, n), out_dtype),
        scratch_shapes=[pltpu.VMEM((tm, d), BF16)],
        compiler_params=_cparams(2),
        name="norm_matmul",
    )(x, g.reshape(1, d), w)


def _matmul_res_body(*refs, n_pairs):
    a_refs = refs[:n_pairs]
    w_refs = refs[n_pairs:2 * n_pairs]
    r_ref, o_ref = refs[2 * n_pairs], refs[2 * n_pairs + 1]
    acc = r_ref[...]
    for a_ref, w_ref in zip(a_refs, w_refs):
        acc = acc + jnp.dot(a_ref[...], w_ref[...], preferred_element_type=F32)
    o_ref[...] = acc


def matmul_res(a_list, w_list, res, *, tm, tn):
    m, n = res.shape
    n_pairs = len(a_list)
    in_specs = [pl.BlockSpec((tm, a.shape[1]), lambda i, j: (i, 0)) for a in a_list]
    in_specs += [pl.BlockSpec((w.shape[0], tn), lambda i, j: (0, j)) for w in w_list]
    in_specs += [pl.BlockSpec((tm, tn), lambda i, j: (i, j))]
    return pl.pallas_call(
        functools.partial(_matmul_res_body, n_pairs=n_pairs),
        grid=(m // tm, n // tn),
        in_specs=in_specs,
        out_specs=pl.BlockSpec((tm, tn), lambda i, j: (i, j)),
        out_shape=jax.ShapeDtypeStruct((m, n), F32),
        compiler_params=_cparams(2),
        name="matmul_res",
    )(*a_list, *w_list, res)


def _rmsnorm_body(x_ref, g_ref, o_ref):
    o_ref[...] = _rmsnorm_rows(x_ref[...], g_ref[...])


def rmsnorm_rows(x, g, *, tm):
    m, d = x.shape
    return pl.pallas_call(
        _rmsnorm_body,
        grid=(m // tm,),
        in_specs=[pl.BlockSpec((tm, d), lambda i: (i, 0)), pl.BlockSpec((1, d), lambda i: (0, 0))],
        out_specs=pl.BlockSpec((tm, d), lambda i: (i, 0)),
        out_shape=jax.ShapeDtypeStruct((m, d), F32),
        compiler_params=_cparams(1),
        name="final_norm",
    )(x, g.reshape(1, d))


def _attn_rows(q, k, v, heads):
    hd = q.shape[-1] // heads
    scale = hd ** -0.5
    outs = []
    for h in range(heads):
        sl = slice(h * hd, (h + 1) * hd)
        s = lax.dot_general(q[:, sl], k[:, sl], NT_DIMS, preferred_element_type=F32) * scale
        p = jnp.exp(s - jnp.max(s, axis=-1, keepdims=True))
        p = p / jnp.sum(p, axis=-1, keepdims=True)
        outs.append(jnp.dot(p.astype(BF16), v[:, sl], preferred_element_type=F32))
    return jnp.concatenate(outs, axis=-1)


def _attn_body(q_ref, kp_ref, vp_ref, ks_ref, vs_ref, o_ref, *, n_prompt_tiles, nseq, heads):
    i = pl.program_id(0)

    @pl.when(i < n_prompt_tiles)
    def _():
        o_ref[...] = _attn_rows(q_ref[...], kp_ref[0], vp_ref[0], heads).astype(o_ref.dtype)

    @pl.when(i >= n_prompt_tiles)
    def _():
        for s in range(nseq):
            rows = slice(s * CHUNK, (s + 1) * CHUNK)
            o_ref[rows, :] = _attn_rows(q_ref[rows, :], ks_ref[s], vs_ref[s], heads).astype(o_ref.dtype)


def mem_attention(q, kp, vp, ks, vs, *, n_prompt_rows, tm):
    m, d = q.shape
    nm = kp.shape[1]
    npt = n_prompt_rows // tm
    nseq = tm // CHUNK
    samp = lambda i: (jnp.maximum(i - npt, 0), 0, 0)
    return pl.pallas_call(
        functools.partial(_attn_body, n_prompt_tiles=npt, nseq=nseq, heads=MEM_HEADS),
        grid=(m // tm,),
        in_specs=[pl.BlockSpec((tm, d), lambda i: (i, 0)),
                  pl.BlockSpec((1, nm, d), lambda i: (0, 0, 0)),
                  pl.BlockSpec((1, nm, d), lambda i: (0, 0, 0)),
                  pl.BlockSpec((nseq, nm, d), samp),
                  pl.BlockSpec((nseq, nm, d), samp)],
        out_specs=pl.BlockSpec((tm, d), lambda i: (i, 0)),
        out_shape=jax.ShapeDtypeStruct((m, d), BF16),
        compiler_params=_cparams(1),
        name="mem_attention",
    )(q, kp, vp, ks, vs)


def _ffn_up_body(x_ref, g_ref, wg_ref, wv_ref, cw_ref, cb_ref, prev_ref, act_ref, stp_ref, sts_ref,
                 xn_ref, *, n_prompt_tiles, nseq):
    i = pl.program_id(0)
    j = pl.program_id(1)
    tm = x_ref.shape[0]

    @pl.when(j == 0)
    def _():
        xn_ref[...] = _rmsnorm_rows(x_ref[...], g_ref[...]).astype(BF16)

    xn = xn_ref[...]
    gate = jnp.dot(xn, wg_ref[...], preferred_element_type=F32)
    val = jnp.dot(xn, wv_ref[...], preferred_element_type=F32)
    tn = gate.shape[-1]

    def finish(prev, n_seq, seqlen):
        gc = _causal_conv(gate, prev, cw_ref[...], cb_ref[...], n_seq, seqlen)
        act_ref[...] = (_silu(gc) * val).astype(act_ref.dtype)
        return gate.reshape(n_seq, seqlen, tn)[:, seqlen - prev.shape[1]:, :]

    @pl.when(i == 0)
    def _():
        stp_ref[j] = jnp.zeros(stp_ref.shape[1:], F32)

    @pl.when(i < n_prompt_tiles)
    def _():
        stp_ref[j] = finish(stp_ref[j], 1, tm)

    @pl.when(i >= n_prompt_tiles)
    def _():
        s0 = (i - n_prompt_tiles) * nseq
        sts_ref[j, pl.ds(s0, nseq)] = finish(prev_ref[0], nseq, CHUNK)


def conv_ffn_up(x, g, w_up, conv_w, conv_b, prev_s, *, n_prompt_rows, tm, tn):
    m, d = x.shape
    ff = w_up.shape[1] // 2
    nj = ff // tn
    ns, wm1 = prev_s.shape[1], prev_s.shape[2]
    npt = n_prompt_rows // tm
    nseq = tm // CHUNK
    return pl.pallas_call(
        functools.partial(_ffn_up_body, n_prompt_tiles=npt, nseq=nseq),
        grid=(m // tm, nj),
        in_specs=[pl.BlockSpec((tm, d), lambda i, j: (i, 0)),
                  pl.BlockSpec((1, d), lambda i, j: (0, 0)),
                  pl.BlockSpec((d, tn), lambda i, j: (0, j)),
                  pl.BlockSpec((d, tn), lambda i, j: (0, j + nj)),
                  pl.BlockSpec((wm1 + 1, tn), lambda i, j: (0, j)),
                  pl.BlockSpec((1, tn), lambda i, j: (0, j)),
                  pl.BlockSpec((1, nseq, wm1, tn), lambda i, j: (j, jnp.maximum(i - npt, 0), 0, 0))],
        out_specs=[pl.BlockSpec((tm, tn), lambda i, j: (i, j)),
                   pl.BlockSpec((nj, 1, wm1, tn), lambda i, j: (0, 0, 0, 0)),
                   pl.BlockSpec((nj, ns, wm1, tn), lambda i, j: (0, 0, 0, 0))],
        out_shape=[jax.ShapeDtypeStruct((m, ff), BF16),
                   jax.ShapeDtypeStruct((nj, 1, wm1, tn), F32),
                   jax.ShapeDtypeStruct((nj, ns, wm1, tn), F32)],
        scratch_shapes=[pltpu.VMEM((tm, d), BF16)],
        compiler_params=_cparams(2),
        name="ffn_up",
    )(x, g.reshape(1, d), w_up, w_up, conv_w, conv_b.reshape(1, ff), prev_s)


def _rglru_body(x_ref, gate_ref, cw_ref, cb_ref, wa_ref, ba_ref, wx_ref, bx_ref, lam_ref,
                cv0_ref, h0_ref, o_ref, cvp_ref, hp_ref, cvs_ref, hs_ref, *, n_prompt_tiles, nseq, heads):
    i = pl.program_id(0)
    tm, ch = x_ref.shape
    bd = ch // heads
    x = x_ref[...]

    def run(prev, h0, n_seq, seqlen, first_row):
        xc = _causal_conv(x, prev, cw_ref[...], cb_ref[...], n_seq, seqlen)
        ra, rx = [], []
        for h in range(heads):
            xh = xc[:, h * bd:(h + 1) * bd].astype(BF16)
            ra.append(jnp.dot(xh, wa_ref[h], preferred_element_type=F32))
            rx.append(jnp.dot(xh, wx_ref[h], preferred_element_type=F32))
        r = _sigmoid(jnp.concatenate(ra, axis=-1) + ba_ref[...])
        ig = _sigmoid(jnp.concatenate(rx, axis=-1) + bx_ref[...])
        log_a = RG_C * r * _log_sigmoid(lam_ref[...])
        a = jnp.exp(log_a)
        mult = jnp.sqrt(-jnp.tanh(log_a) * (a * a + 1.0))
        row = lax.broadcasted_iota(jnp.int32, (n_seq, seqlen, ch), 1)
        if first_row is not None:
            mult = jnp.where(row.reshape(tm, ch) + first_row == 0, 1.0, mult)
        u = xc * ig * mult
        a3 = a.reshape(n_seq, seqlen, ch)
        u3 = u.reshape(n_seq, seqlen, ch)
        u3 = jnp.where(row == 0, u3 + a3 * h0, u3)
        a2, u2 = a3.reshape(tm, ch), u3.reshape(tm, ch)
        row2 = row.reshape(tm, ch)
        d = 1
        while d < seqlen:
            live = row2 >= d
            a_sh = jnp.where(live, pltpu.roll(a2, d, axis=0), 1.0)
            u_sh = jnp.where(live, pltpu.roll(u2, d, axis=0), 0.0)
            u2 = u2 + a2 * u_sh
            a2 = a2 * a_sh
            d *= 2
        o_ref[...] = (u2 * _gelu_tanh(gate_ref[...])).astype(o_ref.dtype)
        hs3 = u2.reshape(n_seq, seqlen, ch)
        x3 = x.reshape(n_seq, seqlen, ch)
        return x3[:, seqlen - prev.shape[1]:, :], hs3[:, seqlen - 1:, :]

    @pl.when(i == 0)
    def _():
        cvp_ref[...] = jnp.zeros(cvp_ref.shape, F32)
        hp_ref[...] = jnp.zeros(hp_ref.shape, F32)

    @pl.when(i < n_prompt_tiles)
    def _():
        cv, hl = run(cvp_ref[...], hp_ref[...], 1, tm, i * tm)
        cvp_ref[...] = cv
        hp_ref[...] = hl

    @pl.when(i >= n_prompt_tiles)
    def _():
        cv, hl = run(cv0_ref[...], h0_ref[...], nseq, CHUNK, None)
        s0 = (i - n_prompt_tiles) * nseq
        cvs_ref[pl.ds(s0, nseq)] = cv
        hs_ref[pl.ds(s0, nseq)] = hl


def rglru(proj, x_col, gate_col, conv_w, conv_b, w_a, b_a, w_x, b_x, lam, cv0, h0, *, n_prompt_rows, tm):
    m = proj.shape[0]
    heads, bd, _ = w_a.shape
    ch = heads * bd
    ns, wm1 = cv0.shape[0], cv0.shape[1]
    npt = n_prompt_rows // tm
    nseq = tm // CHUNK
    samp = lambda i: (jnp.maximum(i - npt, 0), 0, 0)
    const2 = lambda i: (0, 0)
    const3 = lambda i: (0, 0, 0)
    return pl.pallas_call(
        functools.partial(_rglru_body, n_prompt_tiles=npt, nseq=nseq, heads=heads),
        grid=(m // tm,),
        in_specs=[pl.BlockSpec((tm, ch), lambda i: (i, x_col)),
                  pl.BlockSpec((tm, ch), lambda i: (i, gate_col)),
                  pl.BlockSpec((wm1 + 1, ch), const2),
                  pl.BlockSpec((1, ch), const2),
                  pl.BlockSpec((heads, bd, bd), const3),
                  pl.BlockSpec((1, ch), const2),
                  pl.BlockSpec((heads, bd, bd), const3),
                  pl.BlockSpec((1, ch), const2),
                  pl.BlockSpec((1, ch), const2),
                  pl.BlockSpec((nseq, wm1, ch), samp),
                  pl.BlockSpec((nseq, 1, ch), samp)],
        out_specs=[pl.BlockSpec((tm, ch), lambda i: (i, 0)),
                   pl.BlockSpec((1, wm1, ch), const3),
                   pl.BlockSpec((1, 1, ch), const3),
                   pl.BlockSpec((ns, wm1, ch), const3),
                   pl.BlockSpec((ns, 1, ch), const3)],
        out_shape=[jax.ShapeDtypeStruct((m, ch), BF16),
                   jax.ShapeDtypeStruct((1, wm1, ch), F32),
                   jax.ShapeDtypeStruct((1, 1, ch), F32),
                   jax.ShapeDtypeStruct((ns, wm1, ch), F32),
                   jax.ShapeDtypeStruct((ns, 1, ch), F32)],
        compiler_params=_cparams(1),
        name="rglru",
    )(proj, proj, conv_w, conv_b.reshape(1, ch), w_a.astype(BF16), b_a.reshape(1, ch),
      w_x.astype(BF16), b_x.reshape(1, ch), lam.reshape(1, ch), cv0, h0)


def _gated_chunk(q, k, g, v, st_ref, heads):
    c = q.shape[0]
    dk = q.shape[1] // heads
    dv = v.shape[1] // heads
    nsub = c // SUB
    b = _dot01_left(_tril01(c), g)
    b_end = b[c - 1:c, :]
    qb = (q * jnp.exp(b)).astype(BF16)
    k_end = (k * jnp.exp(b_end - b)).astype(BF16)
    dec_end = jnp.exp(b_end)
    vb = v.astype(BF16)
    lane = lax.broadcasted_iota(jnp.int32, (SUB, c), 1)
    srow = lax.broadcasted_iota(jnp.int32, (SUB, c), 0)
    outs = []
    for h in range(heads):
        sk = slice(h * dk, (h + 1) * dk)
        sv = slice(h * dv, (h + 1) * dv)
        st = st_ref[0, h]
        o = lax.dot_general(qb[:, sk], st.astype(BF16), NT_DIMS, preferred_element_type=F32)
        qh, kh, bh = q[:, sk], k[:, sk], b[:, sk]
        blocks = []
        for i in range(nsub):
            r0 = i * SUB
            qi, ki, bi = qh[r0:r0 + SUB], kh[r0:r0 + SUB], bh[r0:r0 + SUB]
            if i > 0:
                ri = bh[r0:r0 + 1]
                qt = (qi * jnp.exp(bi - ri)).astype(BF16)
                kt = (kh * jnp.exp(jnp.minimum(ri - bh, 0.0))).astype(BF16)
                a = lax.dot_general(qt, kt, NT_DIMS, preferred_element_type=F32)
                a = jnp.where(lane < r0, a, 0.0)
            else:
                a = jnp.zeros((SUB, c), F32)
            for s in range(SUB):
                e = jnp.exp(jnp.minimum(bi - bi[s:s + 1], 0.0))
                col = jnp.sum(qi * (ki[s:s + 1] * e), axis=-1, keepdims=True)
                a = jnp.where((lane == r0 + s) & (srow >= s), col, a)
            blocks.append(a)
        a_full = jnp.concatenate(blocks, axis=0).astype(BF16)
        o = o + jnp.dot(a_full, vb[:, sv], preferred_element_type=F32)
        st_ref[0, h] = st * dec_end[:, sk] + lax.dot_general(
            vb[:, sv], k_end[:, sk], TN_DIMS, preferred_element_type=F32)
        outs.append(o)
    return outs


def _head_norm_gate(outs, ng_ref, gate_ref):
    res = []
    for h, o in enumerate(outs):
        dv = o.shape[-1]
        sv = slice(h * dv, (h + 1) * dv)
        res.append(_rmsnorm_rows(o, ng_ref[:, sv]) * _silu(gate_ref[:, sv]))
    return jnp.concatenate(res, axis=-1)


def _hgrn_body(q_ref, f_ref, v_ref, gate_ref, lb_ref, ng_ref, s0_ref, o_ref, st_ref, *, n_prompt_chunks, heads):
    c = pl.program_id(0)

    @pl.when((c == 0) | (c >= n_prompt_chunks))
    def _():
        st_ref[...] = s0_ref[...]

    lb = lb_ref[...]
    k = (1.0 - lb) / (1.0 + jnp.exp(f_ref[...]))
    g = jnp.log1p(-k)
    outs = _gated_chunk(_silu(q_ref[...]), k, g, v_ref[...], st_ref, heads)
    o_ref[...] = _head_norm_gate(outs, ng_ref, gate_ref).astype(o_ref.dtype)


def _gla_body(q_ref, k_ref, v_ref, gate_ref, gl_ref, gkw_ref, gkb_ref, ng_ref, s0_ref, o_ref, st_ref,
              *, n_prompt_chunks, heads):
    c = pl.program_id(0)

    @pl.when((c == 0) | (c >= n_prompt_chunks))
    def _():
        st_ref[...] = s0_ref[...]

    dk = q_ref.shape[1] // heads
    z = jnp.dot(gl_ref[...].astype(BF16), gkw_ref[...], preferred_element_type=F32) + gkb_ref[...]
    g = _log_sigmoid(z) * (1.0 / D_GATE_NORM)
    outs = _gated_chunk(q_ref[...] * dk ** -0.5, k_ref[...], g, v_ref[...], st_ref, heads)
    o_ref[...] = _head_norm_gate(outs, ng_ref, gate_ref).astype(o_ref.dtype)


def _chunk_state_map(n_prompt_chunks, ndim):
    return lambda c: (jnp.maximum(c - (n_prompt_chunks - 1), 0),) + (0,) * (ndim - 1)


def hgrn2(proj, lb, norm_g, s0, *, n_prompt_chunks):
    m = proj.shape[0]
    _, heads, dv, dk = s0.shape
    kw, vw = heads * dk, heads * dv
    smap = _chunk_state_map(n_prompt_chunks, 4)
    row = lambda c: (0, 0)
    return pl.pallas_call(
        functools.partial(_hgrn_body, n_prompt_chunks=n_prompt_chunks, heads=heads),
        grid=(m // CHUNK,),
        in_specs=[pl.BlockSpec((CHUNK, kw), lambda c: (c, 0)),
                  pl.BlockSpec((CHUNK, kw), lambda c: (c, 1)),
                  pl.BlockSpec((CHUNK, vw), lambda c: (c, 2 * kw // vw)),
                  pl.BlockSpec((CHUNK, vw), lambda c: (c, 2 * kw // vw + 1)),
                  pl.BlockSpec((1, kw), row),
                  pl.BlockSpec((1, vw), row),
                  pl.BlockSpec((1, heads, dv, dk), smap)],
        out_specs=[pl.BlockSpec((CHUNK, vw), lambda c: (c, 0)),
                   pl.BlockSpec((1, heads, dv, dk), smap)],
        out_shape=[jax.ShapeDtypeStruct((m, vw), BF16),
                   jax.ShapeDtypeStruct(s0.shape, F32)],
        compiler_params=_cparams(1),
        name="hgrn2",
    )(proj, proj, proj, proj, lb.reshape(1, kw), norm_g.reshape(1, vw), s0)


def gla(proj, cols, gk_w, gk_b, norm_g, s0, *, n_prompt_chunks):
    m = proj.shape[0]
    _, heads, dv, dk = s0.shape
    kw, vw = heads * dk, heads * dv
    cq, ck, cv, cg, cgl = cols
    smap = _chunk_state_map(n_prompt_chunks, 4)
    row = lambda c: (0, 0)
    return pl.pallas_call(
        functools.partial(_gla_body, n_prompt_chunks=n_prompt_chunks, heads=heads),
        grid=(m // CHUNK,),
        in_specs=[pl.BlockSpec((CHUNK, kw), lambda c: (c, cq)),
                  pl.BlockSpec((CHUNK, kw), lambda c: (c, ck)),
                  pl.BlockSpec((CHUNK, vw), lambda c: (c, cv)),
                  pl.BlockSpec((CHUNK, vw), lambda c: (c, cg)),
                  pl.BlockSpec((CHUNK, LANES), lambda c: (c, cgl)),
                  pl.BlockSpec((LANES, kw), row),
                  pl.BlockSpec((1, kw), row),
                  pl.BlockSpec((1, vw), row),
                  pl.BlockSpec((1, heads, dv, dk), smap)],
        out_specs=[pl.BlockSpec((CHUNK, vw), lambda c: (c, 0)),
                   pl.BlockSpec((1, heads, dv, dk), smap)],
        out_shape=[jax.ShapeDtypeStruct((m, vw), BF16),
                   jax.ShapeDtypeStruct(s0.shape, F32)],
        compiler_params=_cparams(1),
        name="gla",
    )(proj, proj, proj, proj, proj, gk_w, gk_b.reshape(1, kw), norm_g.reshape(1, vw), s0)


def _ssd_body(z_ref, x_ref, bc_ref, dt_ref, cw_ref, cb_ref, dtb_ref, alog_ref, dsk_ref, ng_ref,
              cv0_ref, s0_ref, o_ref, cv_ref, st_ref, *, n_prompt_chunks, heads, groups):
    c = pl.program_id(0)

    @pl.when((c == 0) | (c >= n_prompt_chunks))
    def _():
        st_ref[...] = s0_ref[...]
        cv_ref[...] = cv0_ref[...]

    rows, xw_ = x_ref.shape
    hp = xw_ // heads
    n = bc_ref.shape[1] // (2 * groups)
    gw = xw_ // groups
    x_raw, bc_raw = x_ref[...], bc_ref[...]
    prev = cv_ref[...]
    cw, cb = cw_ref[...], cb_ref[...]
    xs = _silu(_causal_conv(x_raw, prev[:, :, :xw_], cw[:, :xw_], cb[:, :xw_], 1, rows))
    bcs = _silu(_causal_conv(bc_raw, prev[:, :, xw_:], cw[:, xw_:], cb[:, xw_:], 1, rows))
    wm1 = prev.shape[1]
    cv_ref[0, :, :xw_] = x_raw[rows - wm1:, :]
    cv_ref[0, :, xw_:] = bc_raw[rows - wm1:, :]

    erow = lax.broadcasted_iota(jnp.int32, (LANES, xw_), 0)
    ecol = lax.broadcasted_iota(jnp.int32, (LANES, xw_), 1)
    expand = jnp.where(ecol // hp == erow, 1.0, 0.0).astype(BF16)
    dt = _softplus(_dot01_right(dt_ref[...], expand) + dtb_ref[...])
    dta = dt * (-jnp.exp(alog_ref[...]))
    cum = _dot01_left(_tril01(rows), dta)
    cum_end = cum[rows - 1:rows, :]
    xdt = xs * dt
    xdt_b = xdt.astype(BF16)
    x_end = (xdt * jnp.exp(cum_end - cum)).astype(BF16)
    e_cum = jnp.exp(cum)
    dec_end = jnp.exp(cum_end)

    prow = lax.broadcasted_iota(jnp.int32, (rows, LANES), 0)
    plane = lax.broadcasted_iota(jnp.int32, (rows, LANES), 1)
    assert hp == rows and LANES % hp == 0
    pair = LANES // hp
    psrc = plane % hp
    y_parts = []
    for g in range(groups):
        bg = bcs[:, g * n:(g + 1) * n].astype(BF16)
        cg = bcs[:, (groups + g) * n:(groups + g + 1) * n].astype(BF16)
        gl = slice(g * gw, (g + 1) * gw)
        st = st_ref[0, :, gl]
        y_inter = jnp.dot(cg, st.astype(BF16), preferred_element_type=F32) * e_cum[:, gl]
        cb_rep = lax.dot_general(cg, jnp.concatenate([bg] * pair, axis=0), NT_DIMS, preferred_element_type=F32)
        y_intra = []
        for sl in range(gw // LANES):
            lo = g * gw + sl * LANES
            cs = cum[:, lo:lo + LANES]
            cdiag = jnp.sum(jnp.where(prow == psrc, cs, 0.0), axis=0, keepdims=True)
            decay = jnp.exp(jnp.minimum(cs - cdiag, 0.0))
            mt = jnp.where(prow >= psrc, cb_rep * decay, 0.0).astype(BF16)
            xp = xdt_b[:, lo:lo + LANES]
            rhs = jnp.concatenate(
                [jnp.where(plane // hp == p, xp, jnp.zeros_like(xp)) for p in range(pair)], axis=0)
            y_intra.append(jnp.dot(mt, rhs, preferred_element_type=F32))
        y_parts.append(jnp.concatenate(y_intra, axis=-1) + y_inter)
        st_ref[0, :, gl] = st * dec_end[:, gl] + lax.dot_general(
            bg, x_end[:, gl], TN_DIMS, preferred_element_type=F32)
    y = jnp.concatenate(y_parts, axis=-1) + dsk_ref[...] * xs
    yz = y * _silu(z_ref[...])
    res = []
    for g in range(groups):
        gl = slice(g * gw, (g + 1) * gw)
        res.append(_rmsnorm_rows(yz[:, gl], ng_ref[:, gl]))
    o_ref[...] = jnp.concatenate(res, axis=-1).astype(o_ref.dtype)


def ssd(proj, cols, conv_w, conv_b, dt_bias_x, a_log_x, d_skip_x, norm_g, cv0, s0, *, n_prompt_chunks):
    m = proj.shape[0]
    xw_ = s0.shape[2]
    n = s0.shape[1]
    bcw = 2 * C_GROUPS * n
    cz, cx, cbc, cdt = cols
    wm1 = cv0.shape[1]
    smap3 = _chunk_state_map(n_prompt_chunks, 3)
    row = lambda c: (0, 0)
    return pl.pallas_call(
        functools.partial(_ssd_body, n_prompt_chunks=n_prompt_chunks, heads=C_HEADS, groups=C_GROUPS),
        grid=(m // CHUNK,),
        in_specs=[pl.BlockSpec((CHUNK, xw_), lambda c: (c, cz)),
                  pl.BlockSpec((CHUNK, xw_), lambda c: (c, cx)),
                  pl.BlockSpec((CHUNK, bcw), lambda c: (c, cbc)),
                  pl.BlockSpec((CHUNK, LANES), lambda c: (c, cdt)),
                  pl.BlockSpec((wm1 + 1, xw_ + bcw), row),
                  pl.BlockSpec((1, xw_ + bcw), row),
                  pl.BlockSpec((1, xw_), row),
                  pl.BlockSpec((1, xw_), row),
                  pl.BlockSpec((1, xw_), row),
                  pl.BlockSpec((1, xw_), row),
                  pl.BlockSpec((1, wm1, xw_ + bcw), smap3),
                  pl.BlockSpec((1, n, xw_), smap3)],
        out_specs=[pl.BlockSpec((CHUNK, xw_), lambda c: (c, 0)),
                   pl.BlockSpec((1, wm1, xw_ + bcw), smap3),
                   pl.BlockSpec((1, n, xw_), smap3)],
        out_shape=[jax.ShapeDtypeStruct((m, xw_), BF16),
                   jax.ShapeDtypeStruct(cv0.shape, F32),
                   jax.ShapeDtypeStruct(s0.shape, F32)],
        compiler_params=_cparams(1),
        name="ssd",
    )(proj, proj, proj, proj, conv_w, conv_b.reshape(1, xw_ + bcw), dt_bias_x.reshape(1, xw_),
      a_log_x.reshape(1, xw_), d_skip_x.reshape(1, xw_), norm_g.reshape(1, xw_), cv0, s0)


def _with_zero_first(s):
    return jnp.concatenate([jnp.zeros((1,) + s.shape[1:], s.dtype), s], axis=0)


def kernel(x_prompt, x_sample, mem_prompt, cache_mem_k, cache_mem_v, state_hgrn, state_rglru, state_rg_conv, state_ssd, state_ssd_conv, state_gla, state_ffn_conv, norm_mix, norm_xattn, norm_mem, norm_ffn, ev_w_in, hgrn_lb_logits, hgrn_norm, rg_conv_w, rg_conv_b, rg_w_a, rg_b_a, rg_w_x, rg_b_x, rg_lambda, ev_w_out, od_w_in, ssd_conv_w, ssd_conv_b, ssd_dt_bias, ssd_A_log, ssd_D, ssd_norm, gla_gk_w, gla_gk_b, gla_norm, od_w_out, xa_wq, xa_wk, xa_wv, xa_wo, ffn_up, ffn_conv_w, ffn_conv_b, ffn_down, final_norm):
    bp, p_rows, d = x_prompt.shape
    ns, s_len, _ = x_sample.shape
    assert bp == 1 and s_len == CHUNK and p_rows % CHUNK == 0
    depth = norm_mix.shape[0]
    s_rows = ns * s_len
    m = p_rows + s_rows
    npc = p_rows // CHUNK
    nm = mem_prompt.shape[1]
    tm = math.gcd(math.gcd(p_rows, s_rows), 1024)
    tm_seq = math.gcd(tm, 256)

    x = jnp.concatenate([x_prompt.reshape(p_rows, d), x_sample.reshape(s_rows, d)], axis=0)
    lb_all = jnp.cumsum(jax.nn.softmax(hgrn_lb_logits.astype(F32), axis=0), axis=0)

    a_kw = hgrn_lb_logits.shape[1]
    a_vw = hgrn_norm.shape[1]
    b_w = rg_lambda.shape[1]
    c_w = ssd_norm.shape[1]
    c_bc = 2 * C_GROUPS * C_STATE
    c_hd = c_w // C_HEADS
    d_val = gla_norm.shape[1]
    d_key = gla_gk_b.shape[1]
    d_rank = gla_gk_w.shape[1]
    ff = ffn_conv_b.shape[1]
    ffn_tn = 512
    ffp = -(-ff // ffn_tn) * ffn_tn
    nj = ffp // ffn_tn

    mem_k, mem_v = [], []
    hg, rl, rc, ss, sc, gs, fc_p, fc_s = [], [], [], [], [], [], [], []
    for l in range(depth):
        j = l // 2
        if l % 2 == 0:
            proj = norm_matmul(x, norm_mix[l], ev_w_in[j].astype(BF16), tm=tm, tn=512, out_dtype=F32)
            s0 = _with_zero_first(jnp.swapaxes(state_hgrn[j], -1, -2))
            o_a, st = hgrn2(proj, lb_all[l], hgrn_norm[j], s0, n_prompt_chunks=npc)
            hg.append(jnp.swapaxes(st, -1, -2))
            xcol = (2 * a_kw + 2 * a_vw) // b_w
            o_b, cvp, hp_, cvs, hs_ = rglru(
                proj, xcol, xcol + 1, rg_conv_w[j], rg_conv_b[j], rg_w_a[j], rg_b_a[j], rg_w_x[j], rg_b_x[j],
                rg_lambda[j], state_rg_conv[j], state_rglru[j].reshape(ns, 1, b_w), n_prompt_rows=p_rows, tm=tm_seq)
            rc.append((cvp, cvs))
            rl.append((hp_.reshape(1, b_w), hs_.reshape(ns, b_w)))
            w_out = ev_w_out[j].astype(BF16)
            x = matmul_res([o_a, o_b], [w_out[:a_vw], w_out[a_vw:]], x, tm=tm, tn=512)
        else:
            w = od_w_in[j]
            offs = np_cumsum_offsets((c_w, c_w + c_bc, C_HEADS, d_key, d_key, d_val, d_val, d_rank))
            o_z, o_xbc, o_dt, o_q, o_k, o_v, o_g, o_gl = offs
            pad = lambda a: jnp.pad(a, ((0, 0), (0, LANES - a.shape[1])))
            w_re = jnp.concatenate([
                w[:, o_z:o_z + c_w], w[:, o_xbc:o_xbc + c_w], w[:, o_xbc + c_w:o_xbc + c_w + c_bc],
                w[:, o_q:o_q + d_key], w[:, o_k:o_k + d_key],
                pad(w[:, o_dt:o_dt + C_HEADS]), pad(w[:, o_gl:o_gl + d_rank]),
                jnp.zeros((d, 2 * LANES), w.dtype),
                w[:, o_v:o_v + d_val], w[:, o_g:o_g + d_val]], axis=1).astype(BF16)
            proj = norm_matmul(x, norm_mix[l], w_re, tm=tm, tn=512, out_dtype=F32)
            base = 2 * c_w + c_bc
            col_dt = (base + 2 * d_key) // LANES
            cv0 = _with_zero_first(state_ssd_conv[j])
            s0c = _with_zero_first(jnp.transpose(state_ssd[j], (0, 3, 1, 2)).reshape(ns, C_STATE, c_w))
            rep = lambda a: jnp.repeat(a, c_hd)
            o_c, cv, stc = ssd(proj, (0, 1, 2 * c_w // c_bc, col_dt), ssd_conv_w[j], ssd_conv_b[j],
                               rep(ssd_dt_bias[j]), rep(ssd_A_log[j]), rep(ssd_D[j]), ssd_norm[j], cv0, s0c,
                               n_prompt_chunks=npc)
            sc.append(cv)
            ss.append(jnp.transpose(stc.reshape(1 + ns, C_STATE, C_HEADS, c_hd), (0, 2, 3, 1)))
            s0d = _with_zero_first(jnp.swapaxes(state_gla[j], -1, -2))
            gkw = jnp.pad(gla_gk_w[j], ((0, LANES - d_rank), (0, 0))).astype(BF16)
            v_start = base + 2 * d_key + 4 * LANES
            o_d, std = gla(proj, (base // d_key, base // d_key + 1, v_start // d_val, v_start // d_val + 1, col_dt + 1),
                           gkw, gla_gk_b[j], gla_norm[j], s0d, n_prompt_chunks=npc)
            gs.append(jnp.swapaxes(std, -1, -2))
            w_out = od_w_out[j].astype(BF16)
            x = matmul_res([o_c, o_d], [w_out[:c_w], w_out[c_w:]], x, tm=tm, tn=512)

        wkv = jnp.concatenate([xa_wk[l], xa_wv[l]], axis=1).astype(BF16)
        kv = norm_matmul(mem_prompt.reshape(nm, d), norm_mem[l], wkv, tm=nm, tn=512, out_dtype=F32)
        mem_k.append(kv[:, :d].reshape(1, nm, MEM_HEADS, d // MEM_HEADS))
        mem_v.append(kv[:, d:].reshape(1, nm, MEM_HEADS, d // MEM_HEADS))
        q = norm_matmul(x, norm_xattn[l], xa_wq[l].astype(BF16), tm=tm, tn=512, out_dtype=BF16)
        att = mem_attention(q, kv[:, :d].astype(BF16).reshape(1, nm, d), kv[:, d:].astype(BF16).reshape(1, nm, d),
                            cache_mem_k[l].astype(BF16).reshape(ns, nm, d), cache_mem_v[l].astype(BF16).reshape(ns, nm, d),
                            n_prompt_rows=p_rows, tm=tm_seq)
        x = matmul_res([att], [xa_wo[l].astype(BF16)], x, tm=tm, tn=512)

        wu = ffn_up[l]
        padc = lambda a: jnp.pad(a, ((0, 0), (0, ffp - ff)))
        w_up = jnp.concatenate([padc(wu[:, :ff]), padc(wu[:, ff:])], axis=1).astype(BF16)
        w_dn = jnp.pad(ffn_down[l], ((0, ffp - ff), (0, 0))).astype(BF16)
        wm1 = state_ffn_conv.shape[2]
        prev_s = jnp.transpose(jnp.pad(state_ffn_conv[l], ((0, 0), (0, 0), (0, ffp - ff))).reshape(ns, wm1, nj, ffn_tn),
                               (2, 0, 1, 3))
        act, stp, sts = conv_ffn_up(x, norm_ffn[l], w_up, padc(ffn_conv_w[l]), jnp.pad(ffn_conv_b[l], (0, ffp - ff)),
                                    prev_s, n_prompt_rows=p_rows, tm=tm, tn=ffn_tn)
        unblock = lambda a: jnp.transpose(a, (1, 2, 0, 3)).reshape(a.shape[1], wm1, ffp)[:, :, :ff]
        fc_p.append(unblock(stp))
        fc_s.append(unblock(sts))
        x = matmul_res([act], [w_dn], x, tm=tm // 2, tn=512)

    y = rmsnorm_rows(x, final_norm, tm=tm)
    y_prompt = y[:p_rows].reshape(1, p_rows, d)
    y_sample = y[p_rows:].reshape(ns, s_len, d)
    stack_p = lambda lst: jnp.stack([a[:1] for a in lst])
    stack_s = lambda lst: jnp.stack([a[1:] for a in lst])
    return (y_prompt, y_sample, jnp.stack(mem_k), jnp.stack(mem_v),
            stack_p(hg), jnp.stack([a[0] for a in rl]), jnp.stack([a[0] for a in rc]),
            stack_p(ss), stack_p(sc), stack_p(gs), jnp.stack(fc_p),
            stack_s(hg), jnp.stack([a[1] for a in rl]), jnp.stack([a[1] for a in rc]),
            stack_s(ss), stack_s(sc), stack_s(gs), jnp.stack(fc_s))


def np_cumsum_offsets(sizes):
    offs, acc = [], 0
    for s in sizes:
        offs.append(acc)
        acc += s
    return offs
```

```python
import functools
import math

import jax
import jax.numpy as jnp
from jax import lax
from jax.experimental import pallas as pl
from jax.experimental.pallas import tpu as pltpu

F32 = jnp.float32
BF16 = jnp.bfloat16

CHUNK = 64
SUB = 16
EPS = 1e-6
LANES = 128
VMEM_LIMIT = 56 * 1024 * 1024

MEM_HEADS = 4
A_HEADS = 8
B_HEADS = 8
C_HEADS = 16
C_GROUPS = 2
C_STATE = 128
D_HEADS = 4
RG_C = 8.0
D_GATE_NORM = 16.0

NT_DIMS = (((1,), (1,)), ((), ()))
TN_DIMS = (((0,), (0,)), ((), ()))


def _cparams(n_axes):
    return pltpu.CompilerParams(dimension_semantics=("arbitrary",) * n_axes,
                                vmem_limit_bytes=VMEM_LIMIT)


def _sigmoid(x):
    return 1.0 / (1.0 + jnp.exp(-x))


def _silu(x):
    return x * _sigmoid(x)


def _log_sigmoid(x):
    return jnp.minimum(x, 0.0) - jnp.log1p(jnp.exp(-jnp.abs(x)))


def _softplus(x):
    return jnp.maximum(x, 0.0) + jnp.log1p(jnp.exp(-jnp.abs(x)))


def _gelu_tanh(x):
    return 0.5 * x * (1.0 + jnp.tanh(math.sqrt(2.0 / math.pi) * (x + 0.044715 * (x * x * x))))


def _split3(x):
    hi = x.astype(BF16)
    r = x - hi.astype(F32)
    mid = r.astype(BF16)
    lo = (r - mid.astype(F32)).astype(BF16)
    return hi, mid, lo


def _dot01_left(t01, x):
    return sum(jnp.dot(t01, p, preferred_element_type=F32) for p in _split3(x))


def _dot01_right(x, e01):
    return sum(jnp.dot(p, e01, preferred_element_type=F32) for p in _split3(x))


def _tril01(n):
    r = lax.broadcasted_iota(jnp.int32, (n, n), 0)
    c = lax.broadcasted_iota(jnp.int32, (n, n), 1)
    return jnp.where(r >= c, 1.0, 0.0).astype(BF16)


def _causal_conv(x, prev, w, b, nseq, seqlen):
    width = w.shape[0]
    ch = x.shape[-1]
    head = 8
    tap = lambda d: w[width - 1 - d:width - d].reshape(1, 1, ch)
    x3 = x.reshape(nseq, seqlen, ch)
    rolled = [pltpu.roll(x, d, axis=0).reshape(nseq, seqlen, ch) for d in range(1, width)]
    y = b + tap(0) * x3
    for d in range(1, width):
        y = y + tap(d) * rolled[d - 1]
    row = lax.broadcasted_iota(jnp.int32, (nseq, head, ch), 1)
    yh = b + tap(0) * x3[:, :head]
    for d in range(1, width):
        xs = rolled[d - 1][:, :head]
        for r in range(d):
            p = width - 1 - d + r
            xs = jnp.where(row == r, prev[:, p:p + 1, :], xs)
        yh = yh + tap(d) * xs
    y = jnp.concatenate([yh, y[:, head:]], axis=1)
    return y.reshape(nseq * seqlen, ch)


def _rmsnorm_rows(x, g):
    ms = jnp.mean(x * x, axis=-1, keepdims=True)
    return x * lax.rsqrt(ms + EPS) * g


def _row_specs(rows, tm, width, col=lambda j: 0):
    blk = (pl.Element(tm), pl.Element(width))
    row = lambda t: pl.multiple_of(t * tm, tm)
    if not isinstance(rows, tuple):
        return [pl.BlockSpec(blk, lambda i, j: (row(i), col(j)))], rows.shape[0], None
    npt = rows[0].shape[0] // tm
    specs = [pl.BlockSpec(blk, lambda i, j: (row(jnp.minimum(i, npt - 1)), col(j))),
             pl.BlockSpec(blk, lambda i, j: (row(jnp.maximum(i - npt, 0)), col(j)))]
    return specs, rows[0].shape[0] + rows[1].shape[0], npt


def _read_rows(refs, npt):
    if len(refs) == 1:
        return refs[0][...]
    return lax.cond(pl.program_id(0) < npt, lambda: refs[0][...], lambda: refs[1][...])


def _norm_matmul_body(*refs, npt):
    n_x = 1 if npt is None else 2
    x_refs = refs[:n_x]
    g_ref, w_ref, o_ref, xn_ref = refs[n_x:]

    @pl.when(pl.program_id(1) == 0)
    def _():
        xn_ref[...] = _rmsnorm_rows(_read_rows(x_refs, npt), g_ref[...]).astype(BF16)

    o_ref[...] = jnp.dot(xn_ref[...], w_ref[...], preferred_element_type=F32).astype(o_ref.dtype)


def _weight_spec(w, layer, k_rows, k_block, tn):
    if layer is None:
        return pl.BlockSpec((k_rows, tn), lambda i, j: (k_block, j))
    return pl.BlockSpec((None, k_rows, tn), lambda i, j: (layer, k_block, j))


def norm_matmul(x, g, w, layer=None, *, tm, tn, out_dtype):
    d, n = w.shape[-2:]
    x_specs, m, npt = _row_specs(x, tm, d)
    xs = x if isinstance(x, tuple) else (x,)
    return pl.pallas_call(
        functools.partial(_norm_matmul_body, npt=npt),
        grid=(m // tm, n // tn),
        in_specs=x_specs + [pl.BlockSpec((1, d), lambda i, j: (0, 0)), _weight_spec(w, layer, d, 0, tn)],
        out_specs=pl.BlockSpec((tm, tn), lambda i, j: (i, j)),
        out_shape=jax.ShapeDtypeStruct((m, n), out_dtype),
        scratch_shapes=[pltpu.VMEM((tm, d), BF16)],
        compiler_params=_cparams(2),
        name="norm_matmul",
    )(*xs, g.reshape(1, d), w)


def _matmul_res_body(*refs, arity, npt):
    pos = 0
    terms = []
    for n_a in arity[:-1]:
        terms.append((refs[pos:pos + n_a], refs[pos + n_a]))
        pos += n_a + 1
    r_refs, o_ref = refs[pos:pos + arity[-1]], refs[-1]
    acc = _read_rows(r_refs, npt)
    for a_refs, w_ref in terms:
        acc = acc + jnp.dot(_read_rows(a_refs, npt), w_ref[0], preferred_element_type=F32)
    o_ref[...] = acc


def matmul_res(terms, w, layer, res, *, tm, tn):
    n = w.shape[-1]
    el = pl.Element
    in_specs, operands, arity, npts = [], [], [], set()
    for a, a_col, k, w_row in terms:
        specs, m, npt = _row_specs(a, tm, k, col=lambda j, c=a_col: c)
        in_specs += specs + [pl.BlockSpec((el(1), el(k), el(tn)),
                                          lambda i, j, r=w_row: (layer, r, pl.multiple_of(j * tn, tn)))]
        operands += list(a if isinstance(a, tuple) else (a,)) + [w]
        arity.append(len(specs))
        npts.add(npt)
    r_specs, m, npt = _row_specs(res, tm, tn, col=lambda j: pl.multiple_of(j * tn, tn))
    arity.append(len(r_specs))
    npts = (npts | {npt}) - {None}
    assert len(npts) <= 1
    return pl.pallas_call(
        functools.partial(_matmul_res_body, arity=tuple(arity), npt=npts.pop() if npts else None),
        grid=(m // tm, n // tn),
        in_specs=in_specs + r_specs,
        out_specs=pl.BlockSpec((tm, tn), lambda i, j: (i, j)),
        out_shape=jax.ShapeDtypeStruct((m, n), F32),
        compiler_params=_cparams(2),
        name="matmul_res",
    )(*operands, *(res if isinstance(res, tuple) else (res,)))


def _rmsnorm_body(x_ref, g_ref, op_ref, os_ref, *, npt):
    y = _rmsnorm_rows(x_ref[...], g_ref[...])

    @pl.when(pl.program_id(0) < npt)
    def _():
        op_ref[...] = y

    @pl.when(pl.program_id(0) >= npt)
    def _():
        os_ref[...] = y


def rmsnorm_rows(x, g, *, n_prompt_rows, tm):
    m, d = x.shape
    npt = n_prompt_rows // tm
    return pl.pallas_call(
        functools.partial(_rmsnorm_body, npt=npt),
        grid=(m // tm,),
        in_specs=[pl.BlockSpec((tm, d), lambda i: (i, 0)), pl.BlockSpec((1, d), lambda i: (0, 0))],
        out_specs=[pl.BlockSpec((tm, d), lambda i: (jnp.minimum(i, npt - 1), 0)),
                   pl.BlockSpec((tm, d), lambda i: (jnp.maximum(i - npt, 0), 0))],
        out_shape=[jax.ShapeDtypeStruct((n_prompt_rows, d), F32),
                   jax.ShapeDtypeStruct((m - n_prompt_rows, d), F32)],
        compiler_params=_cparams(1),
        name="final_norm",
    )(x, g.reshape(1, d))


def _attn_head(qh, kh, vh):
    s = lax.dot_general(qh, kh, NT_DIMS, preferred_element_type=F32) * qh.shape[-1] ** -0.5
    p = jnp.exp(s - jnp.max(s, axis=-1, keepdims=True))
    p = p / jnp.sum(p, axis=-1, keepdims=True)
    return jnp.dot(p.astype(BF16), vh, preferred_element_type=F32)


def _attn_prompt_body(q_ref, k_ref, v_ref, o_ref, *, heads):
    hd = q_ref.shape[1] // heads
    outs = []
    for h in range(heads):
        sl = slice(h * hd, (h + 1) * hd)
        outs.append(_attn_head(q_ref[:, sl], k_ref[:, sl].astype(BF16), v_ref[:, sl].astype(BF16)))
    o_ref[...] = jnp.concatenate(outs, axis=-1).astype(o_ref.dtype)


def _attn_sample_body(q_ref, k_ref, v_ref, o_ref, *, heads):
    hd = q_ref.shape[1] // heads
    outs = []
    for h in range(heads):
        outs.append(_attn_head(q_ref[:, h * hd:(h + 1) * hd], k_ref[:, h, :].astype(BF16),
                               v_ref[:, h, :].astype(BF16)))
    o_ref[...] = jnp.concatenate(outs, axis=-1).astype(o_ref.dtype)


def mem_attention(q, k_prompt, v_prompt, cache_k, cache_v, layer, *, n_prompt_rows, tm):
    m, d = q.shape
    nm = k_prompt.shape[0]
    ns, heads, hd = cache_k.shape[1], cache_k.shape[3], cache_k.shape[4]
    npc = n_prompt_rows // CHUNK
    att = pl.pallas_call(
        functools.partial(_attn_prompt_body, heads=heads),
        grid=(n_prompt_rows // tm,),
        in_specs=[pl.BlockSpec((tm, d), lambda i: (i, 0)),
                  pl.BlockSpec((nm, d), lambda i: (0, 0)),
                  pl.BlockSpec((nm, d), lambda i: (0, 0))],
        out_specs=pl.BlockSpec((tm, d), lambda i: (i, 0)),
        out_shape=jax.ShapeDtypeStruct((n_prompt_rows, d), BF16),
        compiler_params=_cparams(1),
        name="mem_attention_prompt",
    )(q, k_prompt, v_prompt)
    cache_spec = pl.BlockSpec((None, None, nm, heads, hd), lambda s: (layer, s, 0, 0, 0))
    att_s = pl.pallas_call(
        functools.partial(_attn_sample_body, heads=heads),
        grid=(ns,),
        in_specs=[pl.BlockSpec((CHUNK, d), lambda s: (npc + s, 0)), cache_spec, cache_spec],
        out_specs=pl.BlockSpec((CHUNK, d), lambda s: (s, 0)),
        out_shape=jax.ShapeDtypeStruct((m - n_prompt_rows, d), BF16),
        compiler_params=_cparams(1),
        name="mem_attention_sample",
    )(q, cache_k, cache_v)
    return att, att_s


def _ffn_up_body(x_ref, g_ref, wg_ref, wv_ref, cw_ref, cb_ref, prev_ref, act_ref, stp_ref, sts_ref,
                 xn_ref, *, n_prompt_tiles, nseq):
    i = pl.program_id(0)
    j = pl.program_id(1)
    tm = x_ref.shape[0]

    @pl.when(j == 0)
    def _():
        xn_ref[...] = _rmsnorm_rows(x_ref[...], g_ref[...]).astype(BF16)

    xn = xn_ref[...]
    gate = jnp.dot(xn, wg_ref[...], preferred_element_type=F32)
    val = jnp.dot(xn, wv_ref[...], preferred_element_type=F32)
    tn = gate.shape[-1]

    def finish(prev, n_seq, seqlen):
        gc = _causal_conv(gate, prev, cw_ref[...], cb_ref[...], n_seq, seqlen)
        act_ref[...] = (_silu(gc) * val).astype(act_ref.dtype)
        return gate.reshape(n_seq, seqlen, tn)[:, seqlen - prev.shape[1]:, :]

    @pl.when(i == 0)
    def _():
        stp_ref[j] = jnp.zeros(stp_ref.shape[1:], F32)

    @pl.when(i < n_prompt_tiles)
    def _():
        stp_ref[j] = finish(stp_ref[j], 1, tm)

    @pl.when(i >= n_prompt_tiles)
    def _():
        s0 = (i - n_prompt_tiles) * nseq
        sts_ref[j, pl.ds(s0, nseq)] = finish(prev_ref[0], nseq, CHUNK)


def conv_ffn_up(x, g, w_up, conv_w, conv_b, prev_s, *, n_prompt_rows, tm, tn):
    m, d = x.shape
    ff = w_up.shape[1] // 2
    nj = -(-ff // tn)
    ns, wm1 = prev_s.shape[1], prev_s.shape[2]
    npt = n_prompt_rows // tm
    nseq = tm // CHUNK
    col = lambda j, base=0: pl.multiple_of(base + jnp.minimum(j * tn, ff - tn), LANES)
    assert ff % LANES == 0 and tn % LANES == 0
    el = pl.Element
    return pl.pallas_call(
        functools.partial(_ffn_up_body, n_prompt_tiles=npt, nseq=nseq),
        grid=(m // tm, nj),
        in_specs=[pl.BlockSpec((tm, d), lambda i, j: (i, 0)),
                  pl.BlockSpec((1, d), lambda i, j: (0, 0)),
                  pl.BlockSpec((el(d), el(tn)), lambda i, j: (0, col(j))),
                  pl.BlockSpec((el(d), el(tn)), lambda i, j: (0, col(j, ff))),
                  pl.BlockSpec((el(wm1 + 1), el(tn)), lambda i, j: (0, col(j))),
                  pl.BlockSpec((el(1), el(tn)), lambda i, j: (0, col(j))),
                  pl.BlockSpec((1, nseq, wm1, tn), lambda i, j: (j, jnp.maximum(i - npt, 0), 0, 0))],
        out_specs=[pl.BlockSpec((tm, tn), lambda i, j: (i, j)),
                   pl.BlockSpec((nj, 1, wm1, tn), lambda i, j: (0, 0, 0, 0)),
                   pl.BlockSpec((nj, ns, wm1, tn), lambda i, j: (0, 0, 0, 0))],
        out_shape=[jax.ShapeDtypeStruct((m, nj * tn), BF16),
                   jax.ShapeDtypeStruct((nj, 1, wm1, tn), F32),
                   jax.ShapeDtypeStruct((nj, ns, wm1, tn), F32)],
        scratch_shapes=[pltpu.VMEM((tm, d), BF16)],
        compiler_params=_cparams(2),
        name="ffn_up",
    )(x, g.reshape(1, d), w_up, w_up, conv_w, conv_b.reshape(1, ff), prev_s)


def _rglru_body(x_ref, gate_ref, cw_ref, cb_ref, wa_ref, ba_ref, wx_ref, bx_ref, lam_ref,
                cv0_ref, h0_ref, o_ref, cvp_ref, hp_ref, cvs_ref, hs_ref, *, n_prompt_tiles, nseq, heads):
    i = pl.program_id(0)
    tm, ch = x_ref.shape
    bd = ch // heads
    x = x_ref[...]

    def run(prev, h0, n_seq, seqlen, first_row):
        xc = _causal_conv(x, prev, cw_ref[...], cb_ref[...], n_seq, seqlen)
        ra, rx = [], []
        for h in range(heads):
            xh = xc[:, h * bd:(h + 1) * bd].astype(BF16)
            ra.append(jnp.dot(xh, wa_ref[h], preferred_element_type=F32))
            rx.append(jnp.dot(xh, wx_ref[h], preferred_element_type=F32))
        r = _sigmoid(jnp.concatenate(ra, axis=-1) + ba_ref[...])
        ig = _sigmoid(jnp.concatenate(rx, axis=-1) + bx_ref[...])
        log_a = RG_C * r * _log_sigmoid(lam_ref[...])
        a = jnp.exp(log_a)
        mult = jnp.sqrt(-jnp.tanh(log_a) * (a * a + 1.0))
        row = lax.broadcasted_iota(jnp.int32, (n_seq, seqlen, ch), 1)
        if first_row is not None:
            mult = jnp.where(row.reshape(tm, ch) + first_row == 0, 1.0, mult)
        u = xc * ig * mult
        a3 = a.reshape(n_seq, seqlen, ch)
        u3 = u.reshape(n_seq, seqlen, ch)
        u3 = jnp.where(row == 0, u3 + a3 * h0, u3)
        a2, u2 = a3.reshape(tm, ch), u3.reshape(tm, ch)
        row2 = row.reshape(tm, ch)
        d = 1
        while d < seqlen:
            live = row2 >= d
            a_sh = jnp.where(live, pltpu.roll(a2, d, axis=0), 1.0)
            u_sh = jnp.where(live, pltpu.roll(u2, d, axis=0), 0.0)
            u2 = u2 + a2 * u_sh
            a2 = a2 * a_sh
            d *= 2
        o_ref[...] = (u2 * _gelu_tanh(gate_ref[...])).astype(o_ref.dtype)
        hs3 = u2.reshape(n_seq, seqlen, ch)
        x3 = x.reshape(n_seq, seqlen, ch)
        return x3[:, seqlen - prev.shape[1]:, :], hs3[:, seqlen - 1:, :]

    @pl.when(i == 0)
    def _():
        cvp_ref[...] = jnp.zeros(cvp_ref.shape, F32)
        hp_ref[...] = jnp.zeros(hp_ref.shape, F32)

    @pl.when(i < n_prompt_tiles)
    def _():
        cv, hl = run(cvp_ref[...], hp_ref[...], 1, tm, i * tm)
        cvp_ref[...] = cv
        hp_ref[...] = hl

    @pl.when(i >= n_prompt_tiles)
    def _():
        cv, hl = run(cv0_ref[...], h0_ref[...], nseq, CHUNK, None)
        s0 = (i - n_prompt_tiles) * nseq
        cvs_ref[pl.ds(s0, nseq)] = cv
        hs_ref[pl.ds(s0, nseq)] = hl


def rglru(proj, x_col, gate_col, conv_w, conv_b, w_a, b_a, w_x, b_x, lam, cv0, h0, *, n_prompt_rows, tm):
    m = proj.shape[0]
    heads, bd, _ = w_a.shape
    ch = heads * bd
    ns, wm1 = cv0.shape[0], cv0.shape[1]
    npt = n_prompt_rows // tm
    nseq = tm // CHUNK
    samp = lambda i: (jnp.maximum(i - npt, 0), 0, 0)
    const2 = lambda i: (0, 0)
    const3 = lambda i: (0, 0, 0)
    return pl.pallas_call(
        functools.partial(_rglru_body, n_prompt_tiles=npt, nseq=nseq, heads=heads),
        grid=(m // tm,),
        in_specs=[pl.BlockSpec((tm, ch), lambda i: (i, x_col)),
                  pl.BlockSpec((tm, ch), lambda i: (i, gate_col)),
                  pl.BlockSpec((wm1 + 1, ch), const2),
                  pl.BlockSpec((1, ch), const2),
                  pl.BlockSpec((heads, bd, bd), const3),
                  pl.BlockSpec((1, ch), const2),
                  pl.BlockSpec((heads, bd, bd), const3),
                  pl.BlockSpec((1, ch), const2),
                  pl.BlockSpec((1, ch), const2),
                  pl.BlockSpec((nseq, wm1, ch), samp),
                  pl.BlockSpec((nseq, 1, ch), samp)],
        out_specs=[pl.BlockSpec((tm, ch), lambda i: (i, 0)),
                   pl.BlockSpec((1, wm1, ch), const3),
                   pl.BlockSpec((1, 1, ch), const3),
                   pl.BlockSpec((ns, wm1, ch), const3),
                   pl.BlockSpec((ns, 1, ch), const3)],
        out_shape=[jax.ShapeDtypeStruct((m, ch), BF16),
                   jax.ShapeDtypeStruct((1, wm1, ch), F32),
                   jax.ShapeDtypeStruct((1, 1, ch), F32),
                   jax.ShapeDtypeStruct((ns, wm1, ch), F32),
                   jax.ShapeDtypeStruct((ns, 1, ch), F32)],
        compiler_params=_cparams(1),
        name="rglru",
    )(proj, proj, conv_w, conv_b.reshape(1, ch), w_a.astype(BF16), b_a.reshape(1, ch),
      w_x.astype(BF16), b_x.reshape(1, ch), lam.reshape(1, ch), cv0, h0)


def _gated_chunk(q, k, g, v, st_ref, heads):
    c = q.shape[0]
    dk = q.shape[1] // heads
    dv = v.shape[1] // heads
    nsub = c // SUB
    b = _dot01_left(_tril01(c), g)
    b_end = b[c - 1:c, :]
    qb = (q * jnp.exp(b)).astype(BF16)
    k_end = (k * jnp.exp(b_end - b)).astype(BF16)
    dec_end = jnp.exp(b_end)
    vb = v.astype(BF16)
    lane = lax.broadcasted_iota(jnp.int32, (SUB, c), 1)
    srow = lax.broadcasted_iota(jnp.int32, (SUB, c), 0)
    outs = []
    for h in range(heads):
        sk = slice(h * dk, (h + 1) * dk)
        sv = slice(h * dv, (h + 1) * dv)
        st = st_ref[0, h]
        o = lax.dot_general(qb[:, sk], st.astype(BF16), NT_DIMS, preferred_element_type=F32)
        qh, kh, bh = q[:, sk], k[:, sk], b[:, sk]
        blocks = []
        for i in range(nsub):
            r0 = i * SUB
            qi, ki, bi = qh[r0:r0 + SUB], kh[r0:r0 + SUB], bh[r0:r0 + SUB]
            if i > 0:
                ri = bh[r0:r0 + 1]
                qt = (qi * jnp.exp(bi - ri)).astype(BF16)
                kt = (kh * jnp.exp(jnp.minimum(ri - bh, 0.0))).astype(BF16)
                a = lax.dot_general(qt, kt, NT_DIMS, preferred_element_type=F32)
                a = jnp.where(lane < r0, a, 0.0)
            else:
                a = jnp.zeros((SUB, c), F32)
            for s in range(SUB):
                e = jnp.exp(jnp.minimum(bi - bi[s:s + 1], 0.0))
                col = jnp.sum(qi * (ki[s:s + 1] * e), axis=-1, keepdims=True)
                a = jnp.where((lane == r0 + s) & (srow >= s), col, a)
            blocks.append(a)
        a_full = jnp.concatenate(blocks, axis=0).astype(BF16)
        o = o + jnp.dot(a_full, vb[:, sv], preferred_element_type=F32)
        st_ref[0, h] = st * dec_end[:, sk] + lax.dot_general(
            vb[:, sv], k_end[:, sk], TN_DIMS, preferred_element_type=F32)
        outs.append(o)
    return outs


def _head_norm_gate(outs, ng_ref, gate_ref):
    res = []
    for h, o in enumerate(outs):
        dv = o.shape[-1]
        sv = slice(h * dv, (h + 1) * dv)
        res.append(_rmsnorm_rows(o, ng_ref[:, sv]) * _silu(gate_ref[:, sv]))
    return jnp.concatenate(res, axis=-1)


def _hgrn_body(q_ref, f_ref, v_ref, gate_ref, lb_ref, ng_ref, s0_ref, o_ref, st_ref, *, n_prompt_chunks, heads):
    c = pl.program_id(0)

    @pl.when((c == 0) | (c >= n_prompt_chunks))
    def _():
        st_ref[...] = s0_ref[...]

    lb = lb_ref[...]
    k = (1.0 - lb) / (1.0 + jnp.exp(f_ref[...]))
    g = jnp.log1p(-k)
    outs = _gated_chunk(_silu(q_ref[...]), k, g, v_ref[...], st_ref, heads)
    o_ref[...] = _head_norm_gate(outs, ng_ref, gate_ref).astype(o_ref.dtype)


def _gla_body(q_ref, k_ref, v_ref, gate_ref, gl_ref, gkw_ref, gkb_ref, ng_ref, s0_ref, o_ref, st_ref,
              *, n_prompt_chunks, heads):
    c = pl.program_id(0)

    @pl.when((c == 0) | (c >= n_prompt_chunks))
    def _():
        st_ref[...] = s0_ref[...]

    dk = q_ref.shape[1] // heads
    z = jnp.dot(gl_ref[...].astype(BF16), gkw_ref[...], preferred_element_type=F32) + gkb_ref[...]
    g = _log_sigmoid(z) * (1.0 / D_GATE_NORM)
    outs = _gated_chunk(q_ref[...] * dk ** -0.5, k_ref[...], g, v_ref[...], st_ref, heads)
    o_ref[...] = _head_norm_gate(outs, ng_ref, gate_ref).astype(o_ref.dtype)


def _chunk_state_map(n_prompt_chunks, ndim):
    return lambda c: (jnp.maximum(c - (n_prompt_chunks - 1), 0),) + (0,) * (ndim - 1)


def hgrn2(proj, lb, norm_g, s0, *, n_prompt_chunks):
    m = proj.shape[0]
    _, heads, dv, dk = s0.shape
    kw, vw = heads * dk, heads * dv
    smap = _chunk_state_map(n_prompt_chunks, 4)
    row = lambda c: (0, 0)
    return pl.pallas_call(
        functools.partial(_hgrn_body, n_prompt_chunks=n_prompt_chunks, heads=heads),
        grid=(m // CHUNK,),
        in_specs=[pl.BlockSpec((CHUNK, kw), lambda c: (c, 0)),
                  pl.BlockSpec((CHUNK, kw), lambda c: (c, 1)),
                  pl.BlockSpec((CHUNK, vw), lambda c: (c, 2 * kw // vw)),
                  pl.BlockSpec((CHUNK, vw), lambda c: (c, 2 * kw // vw + 1)),
                  pl.BlockSpec((1, kw), row),
                  pl.BlockSpec((1, vw), row),
                  pl.BlockSpec((1, heads, dv, dk), smap)],
        out_specs=[pl.BlockSpec((CHUNK, vw), lambda c: (c, 0)),
                   pl.BlockSpec((1, heads, dv, dk), smap)],
        out_shape=[jax.ShapeDtypeStruct((m, vw), BF16),
                   jax.ShapeDtypeStruct(s0.shape, F32)],
        compiler_params=_cparams(1),
        name="hgrn2",
    )(proj, proj, proj, proj, lb.reshape(1, kw), norm_g.reshape(1, vw), s0)


def gla(proj, cols, gk_w, gk_b, norm_g, s0, *, n_prompt_chunks):
    m = proj.shape[0]
    _, heads, dv, dk = s0.shape
    kw, vw = heads * dk, heads * dv
    cq, ck, cv, cg, cgl = cols
    smap = _chunk_state_map(n_prompt_chunks, 4)
    row = lambda c: (0, 0)
    return pl.pallas_call(
        functools.partial(_gla_body, n_prompt_chunks=n_prompt_chunks, heads=heads),
        grid=(m // CHUNK,),
        in_specs=[pl.BlockSpec((CHUNK, kw), lambda c: (c, cq)),
                  pl.BlockSpec((CHUNK, kw), lambda c: (c, ck)),
                  pl.BlockSpec((CHUNK, vw), lambda c: (c, cv)),
                  pl.BlockSpec((CHUNK, vw), lambda c: (c, cg)),
                  pl.BlockSpec((CHUNK, LANES), lambda c: (c, cgl)),
                  pl.BlockSpec((LANES, kw), row),
                  pl.BlockSpec((1, kw), row),
                  pl.BlockSpec((1, vw), row),
                  pl.BlockSpec((1, heads, dv, dk), smap)],
        out_specs=[pl.BlockSpec((CHUNK, vw), lambda c: (c, 0)),
                   pl.BlockSpec((1, heads, dv, dk), smap)],
        out_shape=[jax.ShapeDtypeStruct((m, vw), BF16),
                   jax.ShapeDtypeStruct(s0.shape, F32)],
        compiler_params=_cparams(1),
        name="gla",
    )(proj, proj, proj, proj, proj, gk_w, gk_b.reshape(1, kw), norm_g.reshape(1, vw), s0)


def _ssd_body(z_ref, x_ref, bc_ref, dt_ref, cw_ref, cb_ref, dtb_ref, alog_ref, dsk_ref, ng_ref,
              cv0_ref, s0_ref, o_ref, cv_ref, st_ref, *, n_prompt_chunks, heads, groups):
    c = pl.program_id(0)

    @pl.when((c == 0) | (c >= n_prompt_chunks))
    def _():
        st_ref[...] = s0_ref[...]
        cv_ref[...] = cv0_ref[...]

    rows, xw_ = x_ref.shape
    hp = xw_ // heads
    n = bc_ref.shape[1] // (2 * groups)
    gw = xw_ // groups
    x_raw, bc_raw = x_ref[...], bc_ref[...]
    prev = cv_ref[...]
    cw, cb = cw_ref[...], cb_ref[...]
    xs = _silu(_causal_conv(x_raw, prev[:, :, :xw_], cw[:, :xw_], cb[:, :xw_], 1, rows))
    bcs = _silu(_causal_conv(bc_raw, prev[:, :, xw_:], cw[:, xw_:], cb[:, xw_:], 1, rows))
    wm1 = prev.shape[1]
    cv_ref[0, :, :xw_] = x_raw[rows - wm1:, :]
    cv_ref[0, :, xw_:] = bc_raw[rows - wm1:, :]

    erow = lax.broadcasted_iota(jnp.int32, (LANES, xw_), 0)
    ecol = lax.broadcasted_iota(jnp.int32, (LANES, xw_), 1)
    expand = jnp.where(ecol // hp == erow, 1.0, 0.0).astype(BF16)
    dt = _softplus(_dot01_right(dt_ref[...], expand) + dtb_ref[...])
    dta = dt * (-jnp.exp(alog_ref[...]))
    cum = _dot01_left(_tril01(rows), dta)
    cum_end = cum[rows - 1:rows, :]
    xdt = xs * dt
    xdt_b = xdt.astype(BF16)
    x_end = (xdt * jnp.exp(cum_end - cum)).astype(BF16)
    e_cum = jnp.exp(cum)
    dec_end = jnp.exp(cum_end)

    prow = lax.broadcasted_iota(jnp.int32, (rows, LANES), 0)
    plane = lax.broadcasted_iota(jnp.int32, (rows, LANES), 1)
    assert hp == rows and LANES % hp == 0
    pair = LANES // hp
    psrc = plane % hp
    y_parts = []
    for g in range(groups):
        bg = bcs[:, g * n:(g + 1) * n].astype(BF16)
        cg = bcs[:, (groups + g) * n:(groups + g + 1) * n].astype(BF16)
        gl = slice(g * gw, (g + 1) * gw)
        st = st_ref[0, :, gl]
        y_inter = jnp.dot(cg, st.astype(BF16), preferred_element_type=F32) * e_cum[:, gl]
        cb_rep = lax.dot_general(cg, jnp.concatenate([bg] * pair, axis=0), NT_DIMS, preferred_element_type=F32)
        y_intra = []
        for sl in range(gw // LANES):
            lo = g * gw + sl * LANES
            cs = cum[:, lo:lo + LANES]
            cdiag = jnp.sum(jnp.where(prow == psrc, cs, 0.0), axis=0, keepdims=True)
            decay = jnp.exp(jnp.minimum(cs - cdiag, 0.0))
            mt = jnp.where(prow >= psrc, cb_rep * decay, 0.0).astype(BF16)
            xp = xdt_b[:, lo:lo + LANES]
            rhs = jnp.concatenate(
                [jnp.where(plane // hp == p, xp, jnp.zeros_like(xp)) for p in range(pair)], axis=0)
            y_intra.append(jnp.dot(mt, rhs, preferred_element_type=F32))
        y_parts.append(jnp.concatenate(y_intra, axis=-1) + y_inter)
        st_ref[0, :, gl] = st * dec_end[:, gl] + lax.dot_general(
            bg, x_end[:, gl], TN_DIMS, preferred_element_type=F32)
    y = jnp.concatenate(y_parts, axis=-1) + dsk_ref[...] * xs
    yz = y * _silu(z_ref[...])
    res = []
    for g in range(groups):
        gl = slice(g * gw, (g + 1) * gw)
        res.append(_rmsnorm_rows(yz[:, gl], ng_ref[:, gl]))
    o_ref[...] = jnp.concatenate(res, axis=-1).astype(o_ref.dtype)


def ssd(proj, cols, conv_w, conv_b, dt_bias_x, a_log_x, d_skip_x, norm_g, cv0, s0, *, n_prompt_chunks):
    m = proj.shape[0]
    xw_ = s0.shape[2]
    n = s0.shape[1]
    bcw = 2 * C_GROUPS * n
    cz, cx, cbc, cdt = cols
    wm1 = cv0.shape[1]
    smap3 = _chunk_state_map(n_prompt_chunks, 3)
    row = lambda c: (0, 0)
    return pl.pallas_call(
        functools.partial(_ssd_body, n_prompt_chunks=n_prompt_chunks, heads=C_HEADS, groups=C_GROUPS),
        grid=(m // CHUNK,),
        in_specs=[pl.BlockSpec((CHUNK, xw_), lambda c: (c, cz)),
                  pl.BlockSpec((CHUNK, xw_), lambda c: (c, cx)),
                  pl.BlockSpec((CHUNK, bcw), lambda c: (c, cbc)),
                  pl.BlockSpec((CHUNK, LANES), lambda c: (c, cdt)),
                  pl.BlockSpec((wm1 + 1, xw_ + bcw), row),
                  pl.BlockSpec((1, xw_ + bcw), row),
                  pl.BlockSpec((1, xw_), row),
                  pl.BlockSpec((1, xw_), row),
                  pl.BlockSpec((1, xw_), row),
                  pl.BlockSpec((1, xw_), row),
                  pl.BlockSpec((1, wm1, xw_ + bcw), smap3),
                  pl.BlockSpec((1, n, xw_), smap3)],
        out_specs=[pl.BlockSpec((CHUNK, xw_), lambda c: (c, 0)),
                   pl.BlockSpec((1, wm1, xw_ + bcw), smap3),
                   pl.BlockSpec((1, n, xw_), smap3)],
        out_shape=[jax.ShapeDtypeStruct((m, xw_), BF16),
                   jax.ShapeDtypeStruct(cv0.shape, F32),
                   jax.ShapeDtypeStruct(s0.shape, F32)],
        compiler_params=_cparams(1),
        name="ssd",
    )(proj, proj, proj, proj, conv_w, conv_b.reshape(1, xw_ + bcw), dt_bias_x.reshape(1, xw_),
      a_log_x.reshape(1, xw_), d_skip_x.reshape(1, xw_), norm_g.reshape(1, xw_), cv0, s0)


def _with_zero_first(s):
    return jnp.concatenate([jnp.zeros((1,) + s.shape[1:], s.dtype), s], axis=0)


def _col_tile_starts(width, tn):
    return [min(j * tn, width - tn) for j in range(-(-width // tn))]


def _block_cols(a, width, tn):
    return jnp.stack([a[..., o:o + tn] for o in _col_tile_starts(width, tn)], axis=0)


def _unblock_cols(b, width, tn):
    return jnp.concatenate([b[j][..., j * tn - o:] for j, o in enumerate(_col_tile_starts(width, tn))], axis=-1)


def kernel(x_prompt, x_sample, mem_prompt, cache_mem_k, cache_mem_v, state_hgrn, state_rglru, state_rg_conv, state_ssd, state_ssd_conv, state_gla, state_ffn_conv, norm_mix, norm_xattn, norm_mem, norm_ffn, ev_w_in, hgrn_lb_logits, hgrn_norm, rg_conv_w, rg_conv_b, rg_w_a, rg_b_a, rg_w_x, rg_b_x, rg_lambda, ev_w_out, od_w_in, ssd_conv_w, ssd_conv_b, ssd_dt_bias, ssd_A_log, ssd_D, ssd_norm, gla_gk_w, gla_gk_b, gla_norm, od_w_out, xa_wq, xa_wk, xa_wv, xa_wo, ffn_up, ffn_conv_w, ffn_conv_b, ffn_down, final_norm):
    bp, p_rows, d = x_prompt.shape
    ns, s_len, _ = x_sample.shape
    assert bp == 1 and s_len == CHUNK and p_rows % CHUNK == 0
    depth = norm_mix.shape[0]
    s_rows = ns * s_len
    m = p_rows + s_rows
    npc = p_rows // CHUNK
    nm = mem_prompt.shape[1]
    tm = math.gcd(math.gcd(p_rows, s_rows), 1024)
    tm_seq = math.gcd(tm, 256)

    x = (x_prompt.reshape(p_rows, d), x_sample.reshape(s_rows, d))
    lb_all = jnp.cumsum(jax.nn.softmax(hgrn_lb_logits.astype(F32), axis=0), axis=0)
    ev_w_in_b, ev_w_out_b, od_w_out_b = ev_w_in.astype(BF16), ev_w_out.astype(BF16), od_w_out.astype(BF16)
    wq_b, wk_b, wv_b, wo_b = xa_wq.astype(BF16), xa_wk.astype(BF16), xa_wv.astype(BF16), xa_wo.astype(BF16)
    w_down_b = ffn_down.astype(BF16)

    a_kw = hgrn_lb_logits.shape[1]
    a_vw = hgrn_norm.shape[1]
    b_w = rg_lambda.shape[1]
    c_w = ssd_norm.shape[1]
    c_bc = 2 * C_GROUPS * C_STATE
    c_hd = c_w // C_HEADS
    d_val = gla_norm.shape[1]
    d_key = gla_gk_b.shape[1]
    d_rank = gla_gk_w.shape[1]
    ff = ffn_conv_b.shape[1]
    ffn_tn = 512

    mem_k, mem_v = [], []
    hg, rl, rc, ss, sc, gs, fc_p, fc_s = [], [], [], [], [], [], [], []
    for l in range(depth):
        j = l // 2
        if l % 2 == 0:
            proj = norm_matmul(x, norm_mix[l], ev_w_in_b, j, tm=tm, tn=512, out_dtype=F32)
            s0 = _with_zero_first(jnp.swapaxes(state_hgrn[j], -1, -2))
            o_a, st = hgrn2(proj, lb_all[l], hgrn_norm[j], s0, n_prompt_chunks=npc)
            hg.append(jnp.swapaxes(st, -1, -2))
            xcol = (2 * a_kw + 2 * a_vw) // b_w
            o_b, cvp, hp_, cvs, hs_ = rglru(
                proj, xcol, xcol + 1, rg_conv_w[j], rg_conv_b[j], rg_w_a[j], rg_b_a[j], rg_w_x[j], rg_b_x[j],
                rg_lambda[j], state_rg_conv[j], state_rglru[j].reshape(ns, 1, b_w), n_prompt_rows=p_rows, tm=tm_seq)
            rc.append((cvp, cvs))
            rl.append((hp_.reshape(1, b_w), hs_.reshape(ns, b_w)))
            x = matmul_res([(o_a, 0, a_vw, 0), (o_b, 0, b_w, a_vw)], ev_w_out_b, j, x, tm=tm, tn=512)
        else:
            w = od_w_in[j]
            offs = np_cumsum_offsets((c_w, c_w + c_bc, C_HEADS, d_key, d_key, d_val, d_val, d_rank))
            o_z, o_xbc, o_dt, o_q, o_k, o_v, o_g, o_gl = offs
            pad = lambda a: jnp.pad(a, ((0, 0), (0, LANES - a.shape[1])))
            w_re = jnp.concatenate([
                w[:, o_z:o_z + c_w], w[:, o_xbc:o_xbc + c_w], w[:, o_xbc + c_w:o_xbc + c_w + c_bc],
                w[:, o_q:o_q + d_key], w[:, o_k:o_k + d_key],
                pad(w[:, o_dt:o_dt + C_HEADS]), pad(w[:, o_gl:o_gl + d_rank]),
                jnp.zeros((d, 2 * LANES), w.dtype),
                w[:, o_v:o_v + d_val], w[:, o_g:o_g + d_val]], axis=1).astype(BF16)
            proj = norm_matmul(x, norm_mix[l], w_re, tm=tm, tn=512, out_dtype=F32)
            assert c_w == d_val
            base = 2 * c_w + c_bc
            col_dt = (base + 2 * d_key) // LANES
            cv0 = _with_zero_first(state_ssd_conv[j])
            s0c = _with_zero_first(jnp.transpose(state_ssd[j], (0, 3, 1, 2)).reshape(ns, C_STATE, c_w))
            rep = lambda a: jnp.repeat(a, c_hd)
            o_c, cv, stc = ssd(proj, (0, 1, 2 * c_w // c_bc, col_dt), ssd_conv_w[j], ssd_conv_b[j],
                               rep(ssd_dt_bias[j]), rep(ssd_A_log[j]), rep(ssd_D[j]), ssd_norm[j], cv0, s0c,
                               n_prompt_chunks=npc)
            sc.append(cv)
            ss.append(jnp.transpose(stc.reshape(1 + ns, C_STATE, C_HEADS, c_hd), (0, 2, 3, 1)))
            s0d = _with_zero_first(jnp.swapaxes(state_gla[j], -1, -2))
            gkw = jnp.pad(gla_gk_w[j], ((0, LANES - d_rank), (0, 0))).astype(BF16)
            v_start = base + 2 * d_key + 4 * LANES
            o_d, std = gla(proj, (base // d_key, base // d_key + 1, v_start // d_val, v_start // d_val + 1, col_dt + 1),
                           gkw, gla_gk_b[j], gla_norm[j], s0d, n_prompt_chunks=npc)
            gs.append(jnp.swapaxes(std, -1, -2))
            x = matmul_res([(o_c, 0, c_w, 0), (o_d, 0, d_val, c_w)], od_w_out_b, j, x, tm=tm, tn=512)

        mem = mem_prompt.reshape(nm, d)
        k_p = norm_matmul(mem, norm_mem[l], wk_b, l, tm=nm, tn=512, out_dtype=F32)
        v_p = norm_matmul(mem, norm_mem[l], wv_b, l, tm=nm, tn=512, out_dtype=F32)
        mem_k.append(k_p.reshape(1, nm, MEM_HEADS, d // MEM_HEADS))
        mem_v.append(v_p.reshape(1, nm, MEM_HEADS, d // MEM_HEADS))
        q = norm_matmul(x, norm_xattn[l], wq_b, l, tm=tm, tn=512, out_dtype=BF16)
        att = mem_attention(q, k_p, v_p, cache_mem_k, cache_mem_v, l, n_prompt_rows=p_rows, tm=tm_seq)
        x = matmul_res([(att, 0, d, 0)], wo_b, l, x, tm=tm, tn=512)

        prev_s = _block_cols(state_ffn_conv[l], ff, ffn_tn)
        act, stp, sts = conv_ffn_up(x, norm_ffn[l], ffn_up[l].astype(BF16), ffn_conv_w[l], ffn_conv_b[l],
                                    prev_s, n_prompt_rows=p_rows, tm=tm, tn=ffn_tn)
        fc_p.append(_unblock_cols(stp, ff, ffn_tn))
        fc_s.append(_unblock_cols(sts, ff, ffn_tn))
        starts = _col_tile_starts(ff, ffn_tn)
        body = (len(starts) - 1) * ffn_tn
        down_terms = [(act, 0, body, 0), (act, body + (body - starts[-1]), ff - body, body)]
        x = matmul_res(down_terms, w_down_b, l, x, tm=tm // 2, tn=512)

    y_prompt, y_sample = rmsnorm_rows(x, final_norm, n_prompt_rows=p_rows, tm=tm)
    y_prompt = y_prompt.reshape(1, p_rows, d)
    y_sample = y_sample.reshape(ns, s_len, d)
    stack_p = lambda lst: jnp.stack([a[:1] for a in lst])
    stack_s = lambda lst: jnp.stack([a[1:] for a in lst])
    return (y_prompt, y_sample, jnp.stack(mem_k), jnp.stack(mem_v),
            stack_p(hg), jnp.stack([a[0] for a in rl]), jnp.stack([a[0] for a in rc]),
            stack_p(ss), stack_p(sc), stack_p(gs), jnp.stack(fc_p),
            stack_s(hg), jnp.stack([a[1] for a in rl]), jnp.stack([a[1] for a in rc]),
            stack_s(ss), stack_s(sc), stack_s(gs), jnp.stack(fc_s))


def np_cumsum_offsets(sizes):
    offs, acc = [], 0
    for s in sizes:
        offs.append(acc)
        acc += s
    return offs
```

```python
import functools
import math

import jax
import jax.numpy as jnp
from jax import lax
from jax.experimental import pallas as pl
from jax.experimental.pallas import tpu as pltpu

F32 = jnp.float32
BF16 = jnp.bfloat16

CHUNK = 64
SUB = 16
EPS = 1e-6
LOG2_E = 1.4426950408889634
LANES = 128
SUBLANES = 8
VMEM_LIMIT = 56 * 1024 * 1024

MEM_HEADS = 4
A_HEADS = 8
B_HEADS = 8
C_HEADS = 16
C_GROUPS = 2
C_STATE = 128
D_HEADS = 4
RG_C = 8.0
D_GATE_NORM = 16.0

NT_DIMS = (((1,), (1,)), ((), ()))
TN_DIMS = (((0,), (0,)), ((), ()))


def _cparams(n_axes):
    return pltpu.CompilerParams(dimension_semantics=("arbitrary",) * n_axes,
                                vmem_limit_bytes=VMEM_LIMIT)


def _sigmoid(x):
    return 1.0 / (1.0 + jnp.exp(-x))


def _silu(x):
    return x * _sigmoid(x)


def _log_sigmoid(x):
    return jnp.minimum(x, 0.0) - jnp.log1p(jnp.exp(-jnp.abs(x)))


def _softplus(x):
    return jnp.maximum(x, 0.0) + jnp.log1p(jnp.exp(-jnp.abs(x)))


def _gelu_tanh(x):
    return 0.5 * x * (1.0 + jnp.tanh(math.sqrt(2.0 / math.pi) * (x + 0.044715 * (x * x * x))))


def _split3(x):
    hi = x.astype(BF16)
    r = x - hi.astype(F32)
    mid = r.astype(BF16)
    lo = (r - mid.astype(F32)).astype(BF16)
    return hi, mid, lo


def _dot01_left(t01, x):
    return sum(jnp.dot(t01, p, preferred_element_type=F32) for p in _split3(x))


def _dot01_right(x, e01):
    return sum(jnp.dot(p, e01, preferred_element_type=F32) for p in _split3(x))


def _tril01(n):
    r = lax.broadcasted_iota(jnp.int32, (n, n), 0)
    c = lax.broadcasted_iota(jnp.int32, (n, n), 1)
    return jnp.where(r >= c, 1.0, 0.0).astype(BF16)


def _causal_conv(x, prev, w, b, nseq, seqlen, fresh=None):
    width = w.shape[0]
    ch = x.shape[-1]
    head = 8
    tap = lambda d: w[width - 1 - d:width - d].reshape(1, 1, ch)
    x3 = x.reshape(nseq, seqlen, ch)
    rolled = [pltpu.roll(x, d, axis=0).reshape(nseq, seqlen, ch) for d in range(1, width)]
    y = b + tap(0) * x3
    for d in range(1, width):
        y = y + tap(d) * rolled[d - 1]
    row = lax.broadcasted_iota(jnp.int32, (nseq, head, ch), 1)
    yh = b + tap(0) * x3[:, :head]
    for d in range(1, width):
        xs = rolled[d - 1][:, :head]
        for r in range(d):
            p = width - 1 - d + r
            xs = jnp.where(row == r, prev[:, p:p + 1, :], xs)
        yh = yh + tap(d) * xs
    if fresh is not None:
        yh = jnp.where(fresh, yh, y[:, :head])
    y = jnp.concatenate([yh, y[:, head:]], axis=1)
    return y.reshape(nseq * seqlen, ch)


def _rmsnorm_rows(x, g):
    ms = jnp.mean(x * x, axis=-1, keepdims=True)
    return x * lax.rsqrt(ms + EPS) * g


def _row_specs(rows, tm, width, col=lambda j: 0):
    blk = (pl.Element(tm), pl.Element(width))
    row = lambda t: pl.multiple_of(t * tm, tm)
    if not isinstance(rows, tuple):
        return [pl.BlockSpec(blk, lambda i, j: (row(i), col(j)))], rows.shape[0], None
    npt = rows[0].shape[0] // tm
    specs = [pl.BlockSpec(blk, lambda i, j: (row(jnp.minimum(i, npt - 1)), col(j))),
             pl.BlockSpec(blk, lambda i, j: (row(jnp.maximum(i - npt, 0)), col(j)))]
    return specs, rows[0].shape[0] + rows[1].shape[0], npt


def _for_row_source(npt, fn, also=True):
    if npt is None and also is True:
        fn(0)
    elif npt is None:
        pl.when(also)(lambda: fn(0))
    else:
        i = pl.program_id(0)
        pl.when(also & (i < npt))(lambda: fn(0))
        pl.when(also & (i >= npt))(lambda: fn(1))


def _norm_matmul_body(*refs, npt):
    n_x = 1 if npt is None else 2
    x_refs = refs[:n_x]
    g_ref, w_ref, o_ref, xn_ref = refs[n_x:]

    def normalize(k):
        xn_ref[...] = _rmsnorm_rows(x_refs[k][...], g_ref[...]).astype(BF16)

    _for_row_source(npt, normalize, also=pl.program_id(1) == 0)
    o_ref[...] = jnp.dot(xn_ref[...], w_ref[...], preferred_element_type=F32).astype(o_ref.dtype)


def _weight_spec(w, layer, k_rows, k_block, tn):
    if layer is None:
        return pl.BlockSpec((k_rows, tn), lambda i, j: (k_block, j))
    return pl.BlockSpec((None, k_rows, tn), lambda i, j: (layer, k_block, j))


def norm_matmul(x, g, w, layer=None, *, tm, tn, out_dtype):
    d, n = w.shape[-2:]
    x_specs, m, npt = _row_specs(x, tm, d)
    xs = x if isinstance(x, tuple) else (x,)
    return pl.pallas_call(
        functools.partial(_norm_matmul_body, npt=npt),
        grid=(m // tm, n // tn),
        in_specs=x_specs + [pl.BlockSpec((1, d), lambda i, j: (0, 0)), _weight_spec(w, layer, d, 0, tn)],
        out_specs=pl.BlockSpec((tm, tn), lambda i, j: (i, j)),
        out_shape=jax.ShapeDtypeStruct((m, n), out_dtype),
        scratch_shapes=[pltpu.VMEM((tm, d), BF16)],
        compiler_params=_cparams(2),
        name="norm_matmul",
    )(*xs, g.reshape(1, d), w)


def _matmul_res_body(*refs, arity, npt):
    pos = 0
    terms = []
    for n_a in arity[:-1]:
        terms.append((refs[pos:pos + n_a], refs[pos + n_a]))
        pos += n_a + 1
    r_refs, o_ref = refs[pos:pos + arity[-1]], refs[-1]

    def compute(k):
        acc = r_refs[min(k, len(r_refs) - 1)][...]
        for a_refs, w_ref in terms:
            acc = acc + jnp.dot(a_refs[min(k, len(a_refs) - 1)][...], w_ref[0], preferred_element_type=F32)
        o_ref[...] = acc

    _for_row_source(npt, compute)


def matmul_res(terms, w, layer, res, *, tm, tn):
    n = w.shape[-1]
    el = pl.Element
    in_specs, operands, arity, npts = [], [], [], set()
    for a, a_col, k, w_row in terms:
        specs, m, npt = _row_specs(a, tm, k, col=lambda j, c=a_col: c)
        in_specs += specs + [pl.BlockSpec((el(1), el(k), el(tn)),
                                          lambda i, j, r=w_row: (layer, r, pl.multiple_of(j * tn, tn)))]
        operands += list(a if isinstance(a, tuple) else (a,)) + [w]
        arity.append(len(specs))
        npts.add(npt)
    r_specs, m, npt = _row_specs(res, tm, tn, col=lambda j: pl.multiple_of(j * tn, tn))
    arity.append(len(r_specs))
    npts = (npts | {npt}) - {None}
    assert len(npts) <= 1
    return pl.pallas_call(
        functools.partial(_matmul_res_body, arity=tuple(arity), npt=npts.pop() if npts else None),
        grid=(m // tm, n // tn),
        in_specs=in_specs + r_specs,
        out_specs=pl.BlockSpec((tm, tn), lambda i, j: (i, j)),
        out_shape=jax.ShapeDtypeStruct((m, n), F32),
        compiler_params=_cparams(2),
        name="matmul_res",
    )(*operands, *(res if isinstance(res, tuple) else (res,)))


def _rmsnorm_body(x_ref, g_ref, op_ref, os_ref, *, npt):
    y = _rmsnorm_rows(x_ref[...], g_ref[...])

    @pl.when(pl.program_id(0) < npt)
    def _():
        op_ref[...] = y

    @pl.when(pl.program_id(0) >= npt)
    def _():
        os_ref[...] = y


def rmsnorm_rows(x, g, *, n_prompt_rows, tm):
    m, d = x.shape
    npt = n_prompt_rows // tm
    return pl.pallas_call(
        functools.partial(_rmsnorm_body, npt=npt),
        grid=(m // tm,),
        in_specs=[pl.BlockSpec((tm, d), lambda i: (i, 0)), pl.BlockSpec((1, d), lambda i: (0, 0))],
        out_specs=[pl.BlockSpec((tm, d), lambda i: (jnp.minimum(i, npt - 1), 0)),
                   pl.BlockSpec((tm, d), lambda i: (jnp.maximum(i - npt, 0), 0))],
        out_shape=[jax.ShapeDtypeStruct((n_prompt_rows, d), F32),
                   jax.ShapeDtypeStruct((m - n_prompt_rows, d), F32)],
        compiler_params=_cparams(1),
        name="final_norm",
    )(x, g.reshape(1, d))


def _attn_head(qh, kh, vh):
    s = lax.dot_general(qh, kh, NT_DIMS, preferred_element_type=F32) * qh.shape[-1] ** -0.5
    p = jnp.exp(s - jnp.max(s, axis=-1, keepdims=True))
    p = p / jnp.sum(p, axis=-1, keepdims=True)
    return jnp.dot(p.astype(BF16), vh, preferred_element_type=F32)


def _attn_prompt_body(q_ref, k_ref, v_ref, o_ref, *, heads):
    hd = q_ref.shape[1] // heads
    outs = []
    for h in range(heads):
        sl = slice(h * hd, (h + 1) * hd)
        outs.append(_attn_head(q_ref[:, sl], k_ref[:, sl].astype(BF16), v_ref[:, sl].astype(BF16)))
    o_ref[...] = jnp.concatenate(outs, axis=-1).astype(o_ref.dtype)


def _attn_sample_body(q_ref, k_ref, v_ref, o_ref, *, heads):
    hd = q_ref.shape[1] // heads
    outs = []
    for h in range(heads):
        outs.append(_attn_head(q_ref[:, h * hd:(h + 1) * hd], k_ref[:, h, :].astype(BF16),
                               v_ref[:, h, :].astype(BF16)))
    o_ref[...] = jnp.concatenate(outs, axis=-1).astype(o_ref.dtype)


def mem_attention(q, k_prompt, v_prompt, cache_k, cache_v, layer, *, n_prompt_rows, tm):
    m, d = q.shape
    nm = k_prompt.shape[0]
    ns, heads, hd = cache_k.shape[1], cache_k.shape[3], cache_k.shape[4]
    npc = n_prompt_rows // CHUNK
    att = pl.pallas_call(
        functools.partial(_attn_prompt_body, heads=heads),
        grid=(n_prompt_rows // tm,),
        in_specs=[pl.BlockSpec((tm, d), lambda i: (i, 0)),
                  pl.BlockSpec((nm, d), lambda i: (0, 0)),
                  pl.BlockSpec((nm, d), lambda i: (0, 0))],
        out_specs=pl.BlockSpec((tm, d), lambda i: (i, 0)),
        out_shape=jax.ShapeDtypeStruct((n_prompt_rows, d), BF16),
        compiler_params=_cparams(1),
        name="mem_attention_prompt",
    )(q, k_prompt, v_prompt)
    cache_spec = pl.BlockSpec((None, None, nm, heads, hd), lambda s: (layer, s, 0, 0, 0))
    att_s = pl.pallas_call(
        functools.partial(_attn_sample_body, heads=heads),
        grid=(ns,),
        in_specs=[pl.BlockSpec((CHUNK, d), lambda s: (npc + s, 0)), cache_spec, cache_spec],
        out_specs=pl.BlockSpec((CHUNK, d), lambda s: (s, 0)),
        out_shape=jax.ShapeDtypeStruct((m - n_prompt_rows, d), BF16),
        compiler_params=_cparams(1),
        name="mem_attention_sample",
    )(q, cache_k, cache_v)
    return att, att_s


def _ffn_up_body(x_ref, g_ref, wg_ref, wv_ref, cw_ref, cb_ref, prev_ref, act_ref, stp_ref, sts_ref,
                 xn_ref, *, n_prompt_tiles, nseq, col_splits):
    i = pl.program_id(0)
    j = pl.program_id(1)
    tn = act_ref.shape[1]
    wm1 = stp_ref.shape[2]
    sw = tn // col_splits

    @pl.when((i == 0) & (j == 0))
    def _():
        stp_ref[...] = jnp.zeros(stp_ref.shape, F32)

    @pl.when(j == 0)
    def _():
        xn_ref[...] = _rmsnorm_rows(x_ref[...], g_ref[...]).astype(BF16)

    xn = xn_ref[...]
    sample = jnp.where(i >= n_prompt_tiles, 1, 0)
    chunk = lax.broadcasted_iota(jnp.int32, (nseq, 1, sw), 0)
    from_state = chunk * 0 + sample == 1
    fresh = chunk * (1 - sample) == 0
    hist = []
    for k in range(col_splits):
        cs = slice(k * sw, (k + 1) * sw)
        gate = jnp.dot(xn, wg_ref[0, :, cs], preferred_element_type=F32)
        val = jnp.dot(xn, wv_ref[0, :, cs], preferred_element_type=F32)
        prev = jnp.where(from_state, prev_ref[0, :, :, cs], stp_ref[j, :, :, cs])
        gc = _causal_conv(gate, prev, cw_ref[:, cs], cb_ref[:, cs], nseq, CHUNK, fresh=fresh)
        act_ref[:, cs] = (_silu(gc) * val).astype(act_ref.dtype)
        hist.append(gate.reshape(nseq, CHUNK, sw)[:, CHUNK - wm1:, :])
    hist = jnp.concatenate(hist, axis=-1)

    @pl.when(i < n_prompt_tiles)
    def _():
        stp_ref[j] = hist[nseq - 1:]

    @pl.when(i >= n_prompt_tiles)
    def _():
        sts_ref[j, pl.ds((i - n_prompt_tiles) * nseq, nseq)] = hist


def conv_ffn_up(x, g, w_up, layer, conv_w, conv_b, prev_s, *, n_prompt_rows, tm, tn):
    m, d = x.shape
    ff = w_up.shape[2] // 2
    nj = -(-ff // tn)
    ns, wm1 = prev_s.shape[1], prev_s.shape[2]
    npt = n_prompt_rows // tm
    nseq = tm // CHUNK
    col = lambda j, base=0: pl.multiple_of(base + jnp.minimum(j * tn, ff - tn), LANES)
    assert ff % LANES == 0 and tn % LANES == 0
    el = pl.Element
    return pl.pallas_call(
        functools.partial(_ffn_up_body, n_prompt_tiles=npt, nseq=nseq, col_splits=2),
        grid=(m // tm, nj),
        in_specs=[pl.BlockSpec((tm, d), lambda i, j: (i, 0)),
                  pl.BlockSpec((1, d), lambda i, j: (0, 0)),
                  pl.BlockSpec((el(1), el(d), el(tn)), lambda i, j: (layer, 0, col(j))),
                  pl.BlockSpec((el(1), el(d), el(tn)), lambda i, j: (layer, 0, col(j, ff))),
                  pl.BlockSpec((el(wm1 + 1), el(tn)), lambda i, j: (0, col(j))),
                  pl.BlockSpec((el(1), el(tn)), lambda i, j: (0, col(j))),
                  pl.BlockSpec((1, nseq, wm1, tn), lambda i, j: (j, jnp.maximum(i - npt, 0), 0, 0))],
        out_specs=[pl.BlockSpec((tm, tn), lambda i, j: (i, j)),
                   pl.BlockSpec((nj, 1, wm1, tn), lambda i, j: (0, 0, 0, 0)),
                   pl.BlockSpec((nj, ns, wm1, tn), lambda i, j: (0, 0, 0, 0))],
        out_shape=[jax.ShapeDtypeStruct((m, nj * tn), BF16),
                   jax.ShapeDtypeStruct((nj, 1, wm1, tn), F32),
                   jax.ShapeDtypeStruct((nj, ns, wm1, tn), F32)],
        scratch_shapes=[pltpu.VMEM((tm, d), BF16)],
        compiler_params=_cparams(2),
        name="ffn_up",
    )(x, g.reshape(1, d), w_up, w_up, conv_w, conv_b.reshape(1, ff), prev_s)


def _rglru_body(x_ref, gate_ref, cw_ref, cb_ref, wa_ref, ba_ref, wx_ref, bx_ref, lam_ref,
                cv0_ref, h0_ref, o_ref, cvp_ref, hp_ref, cvs_ref, hs_ref, *, n_prompt_tiles, nseq, heads):
    i = pl.program_id(0)
    tm, ch = x_ref.shape
    bd = ch // heads
    x = x_ref[...]

    def run(prev, h0, n_seq, seqlen, first_row):
        xc = _causal_conv(x, prev, cw_ref[...], cb_ref[...], n_seq, seqlen)
        ra, rx = [], []
        for h in range(heads):
            xh = xc[:, h * bd:(h + 1) * bd].astype(BF16)
            ra.append(jnp.dot(xh, wa_ref[h], preferred_element_type=F32))
            rx.append(jnp.dot(xh, wx_ref[h], preferred_element_type=F32))
        r = _sigmoid(jnp.concatenate(ra, axis=-1) + ba_ref[...])
        ig = _sigmoid(jnp.concatenate(rx, axis=-1) + bx_ref[...])
        log_a = RG_C * r * _log_sigmoid(lam_ref[...])
        a = jnp.exp(log_a)
        m2 = -jnp.tanh(log_a) * (a * a + 1.0)
        mult = jnp.where(m2 > 0.0, m2 * lax.rsqrt(m2), 0.0)
        row = lax.broadcasted_iota(jnp.int32, (n_seq, seqlen, ch), 1)
        if first_row is not None:
            mult = jnp.where(row.reshape(tm, ch) + first_row == 0, 1.0, mult)
        u = xc * ig * mult
        ng = tm // SUBLANES
        ag = a.reshape(ng, SUBLANES, ch)
        ug = u.reshape(ng, SUBLANES, ch)
        grow = lax.broadcasted_iota(jnp.int32, (ng, SUBLANES, ch), 1)
        d = 1
        while d < SUBLANES:
            live = grow >= d
            a_sh = jnp.where(live, pltpu.roll(ag, d, axis=1), 1.0)
            u_sh = jnp.where(live, pltpu.roll(ug, d, axis=1), 0.0)
            ug = ug + ag * u_sh
            ag = ag * a_sh
            d *= 2
        groups_per_seq = seqlen // SUBLANES
        hs, last = [], []
        for gi in range(ng):
            if gi % groups_per_seq == 0:
                carry = h0[gi // groups_per_seq]
            hg = ug[gi] + ag[gi] * carry
            carry = hg[SUBLANES - 1:]
            hs.append(hg)
            if (gi + 1) % groups_per_seq == 0:
                last.append(carry)
        h_all = jnp.concatenate(hs, axis=0)
        o_ref[...] = (h_all * _gelu_tanh(gate_ref[...])).astype(o_ref.dtype)
        x3 = x.reshape(n_seq, seqlen, ch)
        return x3[:, seqlen - prev.shape[1]:, :], jnp.stack(last, axis=0)

    @pl.when(i == 0)
    def _():
        cvp_ref[...] = jnp.zeros(cvp_ref.shape, F32)
        hp_ref[...] = jnp.zeros(hp_ref.shape, F32)

    @pl.when(i < n_prompt_tiles)
    def _():
        cv, hl = run(cvp_ref[...], hp_ref[...], 1, tm, i * tm)
        cvp_ref[...] = cv
        hp_ref[...] = hl

    @pl.when(i >= n_prompt_tiles)
    def _():
        cv, hl = run(cv0_ref[...], h0_ref[...], nseq, CHUNK, None)
        s0 = (i - n_prompt_tiles) * nseq
        cvs_ref[pl.ds(s0, nseq)] = cv
        hs_ref[pl.ds(s0, nseq)] = hl


def rglru(proj, x_col, gate_col, conv_w, conv_b, w_a, b_a, w_x, b_x, lam, cv0, h0, *, n_prompt_rows, tm):
    m = proj.shape[0]
    heads, bd, _ = w_a.shape
    ch = heads * bd
    ns, wm1 = cv0.shape[0], cv0.shape[1]
    npt = n_prompt_rows // tm
    nseq = tm // CHUNK
    samp = lambda i: (jnp.maximum(i - npt, 0), 0, 0)
    const2 = lambda i: (0, 0)
    const3 = lambda i: (0, 0, 0)
    return pl.pallas_call(
        functools.partial(_rglru_body, n_prompt_tiles=npt, nseq=nseq, heads=heads),
        grid=(m // tm,),
        in_specs=[pl.BlockSpec((tm, ch), lambda i: (i, x_col)),
                  pl.BlockSpec((tm, ch), lambda i: (i, gate_col)),
                  pl.BlockSpec((wm1 + 1, ch), const2),
                  pl.BlockSpec((1, ch), const2),
                  pl.BlockSpec((heads, bd, bd), const3),
                  pl.BlockSpec((1, ch), const2),
                  pl.BlockSpec((heads, bd, bd), const3),
                  pl.BlockSpec((1, ch), const2),
                  pl.BlockSpec((1, ch), const2),
                  pl.BlockSpec((nseq, wm1, ch), samp),
                  pl.BlockSpec((nseq, 1, ch), samp)],
        out_specs=[pl.BlockSpec((tm, ch), lambda i: (i, 0)),
                   pl.BlockSpec((1, wm1, ch), const3),
                   pl.BlockSpec((1, 1, ch), const3),
                   pl.BlockSpec((ns, wm1, ch), const3),
                   pl.BlockSpec((ns, 1, ch), const3)],
        out_shape=[jax.ShapeDtypeStruct((m, ch), BF16),
                   jax.ShapeDtypeStruct((1, wm1, ch), F32),
                   jax.ShapeDtypeStruct((1, 1, ch), F32),
                   jax.ShapeDtypeStruct((ns, wm1, ch), F32),
                   jax.ShapeDtypeStruct((ns, 1, ch), F32)],
        compiler_params=_cparams(1),
        name="rglru",
    )(proj, proj, conv_w, conv_b.reshape(1, ch), w_a.astype(BF16), b_a.reshape(1, ch),
      w_x.astype(BF16), b_x.reshape(1, ch), lam.reshape(1, ch), cv0, h0)


def _gated_chunk(q, k, g, v, st_ref, kb_ref, heads):
    c = q.shape[0]
    dk = q.shape[1] // heads
    dv = v.shape[1] // heads
    nsub = c // SUB
    b = _dot01_left(_tril01(c), g) * LOG2_E
    b_end = b[c - 1:c, :]
    qb = (q * jnp.exp2(b)).astype(BF16)
    k_end = (k * jnp.exp2(b_end - b)).astype(BF16)
    dec_end = jnp.exp2(b_end)
    vb = v.astype(BF16)
    kb_ref[0] = k
    kb_ref[1] = b
    lane = lax.broadcasted_iota(jnp.int32, (SUB, c), 1)
    srow = lax.broadcasted_iota(jnp.int32, (SUB, c), 0)
    outs = []
    for h in range(heads):
        sk = slice(h * dk, (h + 1) * dk)
        sv = slice(h * dv, (h + 1) * dv)
        st = st_ref[0, h]
        o = lax.dot_general(qb[:, sk], st.astype(BF16), NT_DIMS, preferred_element_type=F32)
        qh, kh, bh, vh = q[:, sk], k[:, sk], b[:, sk], vb[:, sv]
        blocks = []
        for i in range(nsub):
            r0 = i * SUB
            qi, ki, bi = qh[r0:r0 + SUB], kh[r0:r0 + SUB], bh[r0:r0 + SUB]
            a = jnp.zeros((SUB, c), F32)
            for s in range(SUB):
                e = jnp.exp2(bi - kb_ref[1, r0 + s:r0 + s + 1, sk])
                col = jnp.sum(qi * (kb_ref[0, r0 + s:r0 + s + 1, sk] * e), axis=-1, keepdims=True)
                a = jnp.where(lane == r0 + s, col, a)
            a = jnp.where(srow >= lane - r0, a, 0.0)
            if i > 0:
                ri = bh[r0:r0 + 1]
                qt = (qi * jnp.exp2(bi - ri)).astype(BF16)
                kt = (kh * jnp.exp2(ri - bh)).astype(BF16)
                a = jnp.where(lane < r0, lax.dot_general(qt, kt, NT_DIMS, preferred_element_type=F32), a)
            blocks.append(a)
        a_full = jnp.concatenate(blocks, axis=0).astype(BF16)
        o = o + jnp.dot(a_full, vh, preferred_element_type=F32)
        st_ref[0, h] = st * dec_end[:, sk] + lax.dot_general(
            vh, k_end[:, sk], TN_DIMS, preferred_element_type=F32)
        outs.append(o)
    return outs


def _head_norm_gate(outs, ng_ref, gate_ref):
    res = []
    for h, o in enumerate(outs):
        dv = o.shape[-1]
        sv = slice(h * dv, (h + 1) * dv)
        res.append(_rmsnorm_rows(o, ng_ref[:, sv]) * _silu(gate_ref[:, sv]))
    return jnp.concatenate(res, axis=-1)


def _hgrn_body(q_ref, f_ref, v_ref, gate_ref, lb_ref, ng_ref, s0_ref, o_ref, st_ref, kb_ref,
               *, n_prompt_chunks, heads):
    c = pl.program_id(0)

    @pl.when((c == 0) | (c >= n_prompt_chunks))
    def _():
        st_ref[...] = s0_ref[...]

    lb = lb_ref[...]
    k = (1.0 - lb) / (1.0 + jnp.exp(f_ref[...]))
    g = jnp.log1p(-k)
    outs = _gated_chunk(_silu(q_ref[...]), k, g, v_ref[...], st_ref, kb_ref, heads)
    o_ref[...] = _head_norm_gate(outs, ng_ref, gate_ref).astype(o_ref.dtype)


def _gla_body(q_ref, k_ref, v_ref, gate_ref, gl_ref, gkw_ref, gkb_ref, ng_ref, s0_ref, o_ref, st_ref, kb_ref,
              *, n_prompt_chunks, heads):
    c = pl.program_id(0)

    @pl.when((c == 0) | (c >= n_prompt_chunks))
    def _():
        st_ref[...] = s0_ref[...]

    dk = q_ref.shape[1] // heads
    z = jnp.dot(gl_ref[...].astype(BF16), gkw_ref[...], preferred_element_type=F32) + gkb_ref[...]
    g = _log_sigmoid(z) * (1.0 / D_GATE_NORM)
    outs = _gated_chunk(q_ref[...] * dk ** -0.5, k_ref[...], g, v_ref[...], st_ref, kb_ref, heads)
    o_ref[...] = _head_norm_gate(outs, ng_ref, gate_ref).astype(o_ref.dtype)


def _chunk_state_map(n_prompt_chunks, ndim):
    return lambda c: (jnp.maximum(c - (n_prompt_chunks - 1), 0),) + (0,) * (ndim - 1)


def hgrn2(proj, lb, norm_g, s0, *, n_prompt_chunks):
    m = proj.shape[0]
    _, heads, dv, dk = s0.shape
    kw, vw = heads * dk, heads * dv
    smap = _chunk_state_map(n_prompt_chunks, 4)
    row = lambda c: (0, 0)
    return pl.pallas_call(
        functools.partial(_hgrn_body, n_prompt_chunks=n_prompt_chunks, heads=heads),
        grid=(m // CHUNK,),
        in_specs=[pl.BlockSpec((CHUNK, kw), lambda c: (c, 0)),
                  pl.BlockSpec((CHUNK, kw), lambda c: (c, 1)),
                  pl.BlockSpec((CHUNK, vw), lambda c: (c, 2 * kw // vw)),
                  pl.BlockSpec((CHUNK, vw), lambda c: (c, 2 * kw // vw + 1)),
                  pl.BlockSpec((1, kw), row),
                  pl.BlockSpec((1, vw), row),
                  pl.BlockSpec((1, heads, dv, dk), smap)],
        out_specs=[pl.BlockSpec((CHUNK, vw), lambda c: (c, 0)),
                   pl.BlockSpec((1, heads, dv, dk), smap)],
        out_shape=[jax.ShapeDtypeStruct((m, vw), BF16),
                   jax.ShapeDtypeStruct(s0.shape, F32)],
        scratch_shapes=[pltpu.VMEM((2, CHUNK, kw), F32)],
        compiler_params=_cparams(1),
        name="hgrn2",
    )(proj, proj, proj, proj, lb.reshape(1, kw), norm_g.reshape(1, vw), s0)


def gla(proj, cols, gk_w, gk_b, norm_g, s0, *, n_prompt_chunks):
    m = proj.shape[0]
    _, heads, dv, dk = s0.shape
    kw, vw = heads * dk, heads * dv
    cq, ck, cv, cg, cgl = cols
    smap = _chunk_state_map(n_prompt_chunks, 4)
    row = lambda c: (0, 0)
    return pl.pallas_call(
        functools.partial(_gla_body, n_prompt_chunks=n_prompt_chunks, heads=heads),
        grid=(m // CHUNK,),
        in_specs=[pl.BlockSpec((CHUNK, kw), lambda c: (c, cq)),
                  pl.BlockSpec((CHUNK, kw), lambda c: (c, ck)),
                  pl.BlockSpec((CHUNK, vw), lambda c: (c, cv)),
                  pl.BlockSpec((CHUNK, vw), lambda c: (c, cg)),
                  pl.BlockSpec((CHUNK, LANES), lambda c: (c, cgl)),
                  pl.BlockSpec((LANES, kw), row),
                  pl.BlockSpec((1, kw), row),
                  pl.BlockSpec((1, vw), row),
                  pl.BlockSpec((1, heads, dv, dk), smap)],
        out_specs=[pl.BlockSpec((CHUNK, vw), lambda c: (c, 0)),
                   pl.BlockSpec((1, heads, dv, dk), smap)],
        out_shape=[jax.ShapeDtypeStruct((m, vw), BF16),
                   jax.ShapeDtypeStruct(s0.shape, F32)],
        scratch_shapes=[pltpu.VMEM((2, CHUNK, kw), F32)],
        compiler_params=_cparams(1),
        name="gla",
    )(proj, proj, proj, proj, proj, gk_w, gk_b.reshape(1, kw), norm_g.reshape(1, vw), s0)


def _ssd_body(z_ref, x_ref, bc_ref, dt_ref, cw_ref, cb_ref, dtb_ref, alog_ref, dsk_ref, ng_ref,
              cv0_ref, s0_ref, o_ref, cv_ref, st_ref, *, n_prompt_chunks, heads, groups):
    c = pl.program_id(0)

    @pl.when((c == 0) | (c >= n_prompt_chunks))
    def _():
        st_ref[...] = s0_ref[...]
        cv_ref[...] = cv0_ref[...]

    rows, xw_ = x_ref.shape
    hp = xw_ // heads
    n = bc_ref.shape[1] // (2 * groups)
    gw = xw_ // groups
    x_raw, bc_raw = x_ref[...], bc_ref[...]
    prev = cv_ref[...]
    cw, cb = cw_ref[...], cb_ref[...]
    xs = _silu(_causal_conv(x_raw, prev[:, :, :xw_], cw[:, :xw_], cb[:, :xw_], 1, rows))
    bcs = _silu(_causal_conv(bc_raw, prev[:, :, xw_:], cw[:, xw_:], cb[:, xw_:], 1, rows))
    wm1 = prev.shape[1]
    cv_ref[0, :, :xw_] = x_raw[rows - wm1:, :]
    cv_ref[0, :, xw_:] = bc_raw[rows - wm1:, :]

    erow = lax.broadcasted_iota(jnp.int32, (LANES, xw_), 0)
    ecol = lax.broadcasted_iota(jnp.int32, (LANES, xw_), 1)
    expand = jnp.where(ecol // hp == erow, 1.0, 0.0).astype(BF16)
    dt = _softplus(_dot01_right(dt_ref[...], expand) + dtb_ref[...])
    dta = dt * (-jnp.exp(alog_ref[...]))
    cum = _dot01_left(_tril01(rows), dta)
    cum_end = cum[rows - 1:rows, :]
    xdt = xs * dt
    xdt_b = xdt.astype(BF16)
    x_end = (xdt * jnp.exp(cum_end - cum)).astype(BF16)
    e_cum = jnp.exp(cum)
    dec_end = jnp.exp(cum_end)

    prow = lax.broadcasted_iota(jnp.int32, (rows, LANES), 0)
    plane = lax.broadcasted_iota(jnp.int32, (rows, LANES), 1)
    assert hp == rows and LANES % hp == 0
    pair = LANES // hp
    psrc = plane % hp
    y_parts = []
    for g in range(groups):
        bg = bcs[:, g * n:(g + 1) * n].astype(BF16)
        cg = bcs[:, (groups + g) * n:(groups + g + 1) * n].astype(BF16)
        gl = slice(g * gw, (g + 1) * gw)
        st = st_ref[0, :, gl]
        y_inter = jnp.dot(cg, st.astype(BF16), preferred_element_type=F32) * e_cum[:, gl]
        cb_rep = lax.dot_general(cg, jnp.concatenate([bg] * pair, axis=0), NT_DIMS, preferred_element_type=F32)
        y_intra = []
        for sl in range(gw // LANES):
            lo = g * gw + sl * LANES
            cs = cum[:, lo:lo + LANES]
            cdiag = jnp.sum(jnp.where(prow == psrc, cs, 0.0), axis=0, keepdims=True)
            decay = jnp.exp(jnp.minimum(cs - cdiag, 0.0))
            mt = jnp.where(prow >= psrc, cb_rep * decay, 0.0).astype(BF16)
            xp = xdt_b[:, lo:lo + LANES]
            rhs = jnp.concatenate(
                [jnp.where(plane // hp == p, xp, jnp.zeros_like(xp)) for p in range(pair)], axis=0)
            y_intra.append(jnp.dot(mt, rhs, preferred_element_type=F32))
        y_parts.append(jnp.concatenate(y_intra, axis=-1) + y_inter)
        st_ref[0, :, gl] = st * dec_end[:, gl] + lax.dot_general(
            bg, x_end[:, gl], TN_DIMS, preferred_element_type=F32)
    y = jnp.concatenate(y_parts, axis=-1) + dsk_ref[...] * xs
    yz = y * _silu(z_ref[...])
    res = []
    for g in range(groups):
        gl = slice(g * gw, (g + 1) * gw)
        res.append(_rmsnorm_rows(yz[:, gl], ng_ref[:, gl]))
    o_ref[...] = jnp.concatenate(res, axis=-1).astype(o_ref.dtype)


def ssd(proj, cols, conv_w, conv_b, dt_bias_x, a_log_x, d_skip_x, norm_g, cv0, s0, *, n_prompt_chunks):
    m = proj.shape[0]
    xw_ = s0.shape[2]
    n = s0.shape[1]
    bcw = 2 * C_GROUPS * n
    cz, cx, cbc, cdt = cols
    wm1 = cv0.shape[1]
    smap3 = _chunk_state_map(n_prompt_chunks, 3)
    row = lambda c: (0, 0)
    return pl.pallas_call(
        functools.partial(_ssd_body, n_prompt_chunks=n_prompt_chunks, heads=C_HEADS, groups=C_GROUPS),
        grid=(m // CHUNK,),
        in_specs=[pl.BlockSpec((CHUNK, xw_), lambda c: (c, cz)),
                  pl.BlockSpec((CHUNK, xw_), lambda c: (c, cx)),
                  pl.BlockSpec((CHUNK, bcw), lambda c: (c, cbc)),
                  pl.BlockSpec((CHUNK, LANES), lambda c: (c, cdt)),
                  pl.BlockSpec((wm1 + 1, xw_ + bcw), row),
                  pl.BlockSpec((1, xw_ + bcw), row),
                  pl.BlockSpec((1, xw_), row),
                  pl.BlockSpec((1, xw_), row),
                  pl.BlockSpec((1, xw_), row),
                  pl.BlockSpec((1, xw_), row),
                  pl.BlockSpec((1, wm1, xw_ + bcw), smap3),
                  pl.BlockSpec((1, n, xw_), smap3)],
        out_specs=[pl.BlockSpec((CHUNK, xw_), lambda c: (c, 0)),
                   pl.BlockSpec((1, wm1, xw_ + bcw), smap3),
                   pl.BlockSpec((1, n, xw_), smap3)],
        out_shape=[jax.ShapeDtypeStruct((m, xw_), BF16),
                   jax.ShapeDtypeStruct(cv0.shape, F32),
                   jax.ShapeDtypeStruct(s0.shape, F32)],
        compiler_params=_cparams(1),
        name="ssd",
    )(proj, proj, proj, proj, conv_w, conv_b.reshape(1, xw_ + bcw), dt_bias_x.reshape(1, xw_),
      a_log_x.reshape(1, xw_), d_skip_x.reshape(1, xw_), norm_g.reshape(1, xw_), cv0, s0)


def _with_zero_first(s):
    return jnp.concatenate([jnp.zeros((1,) + s.shape[1:], s.dtype), s], axis=0)


def _col_tile_starts(width, tn):
    return [min(j * tn, width - tn) for j in range(-(-width // tn))]


def _block_cols(a, width, tn):
    return jnp.stack([a[..., o:o + tn] for o in _col_tile_starts(width, tn)], axis=0)


def _unblock_cols(b, width, tn):
    return jnp.concatenate([b[j][..., j * tn - o:] for j, o in enumerate(_col_tile_starts(width, tn))], axis=-1)


def kernel(x_prompt, x_sample, mem_prompt, cache_mem_k, cache_mem_v, state_hgrn, state_rglru, state_rg_conv, state_ssd, state_ssd_conv, state_gla, state_ffn_conv, norm_mix, norm_xattn, norm_mem, norm_ffn, ev_w_in, hgrn_lb_logits, hgrn_norm, rg_conv_w, rg_conv_b, rg_w_a, rg_b_a, rg_w_x, rg_b_x, rg_lambda, ev_w_out, od_w_in, ssd_conv_w, ssd_conv_b, ssd_dt_bias, ssd_A_log, ssd_D, ssd_norm, gla_gk_w, gla_gk_b, gla_norm, od_w_out, xa_wq, xa_wk, xa_wv, xa_wo, ffn_up, ffn_conv_w, ffn_conv_b, ffn_down, final_norm):
    bp, p_rows, d = x_prompt.shape
    ns, s_len, _ = x_sample.shape
    assert bp == 1 and s_len == CHUNK and p_rows % CHUNK == 0
    depth = norm_mix.shape[0]
    s_rows = ns * s_len
    m = p_rows + s_rows
    npc = p_rows // CHUNK
    nm = mem_prompt.shape[1]
    tm = math.gcd(math.gcd(p_rows, s_rows), 1024)
    tm_seq = math.gcd(tm, 256)

    x = (x_prompt.reshape(p_rows, d), x_sample.reshape(s_rows, d))
    lb_all = jnp.cumsum(jax.nn.softmax(hgrn_lb_logits.astype(F32), axis=0), axis=0)
    ev_w_in_b, ev_w_out_b, od_w_out_b = ev_w_in.astype(BF16), ev_w_out.astype(BF16), od_w_out.astype(BF16)
    wq_b, wk_b, wv_b, wo_b = xa_wq.astype(BF16), xa_wk.astype(BF16), xa_wv.astype(BF16), xa_wo.astype(BF16)
    w_up_b, w_down_b = ffn_up.astype(BF16), ffn_down.astype(BF16)

    a_kw = hgrn_lb_logits.shape[1]
    a_vw = hgrn_norm.shape[1]
    b_w = rg_lambda.shape[1]
    c_w = ssd_norm.shape[1]
    c_bc = 2 * C_GROUPS * C_STATE
    c_hd = c_w // C_HEADS
    d_val = gla_norm.shape[1]
    d_key = gla_gk_b.shape[1]
    d_rank = gla_gk_w.shape[1]
    ff = ffn_conv_b.shape[1]
    ffn_tn = 512

    mem_k, mem_v = [], []
    hg, rl, rc, ss, sc, gs, fc_p, fc_s = [], [], [], [], [], [], [], []
    for l in range(depth):
        j = l // 2
        if l % 2 == 0:
            proj = norm_matmul(x, norm_mix[l], ev_w_in_b, j, tm=tm, tn=512, out_dtype=F32)
            s0 = _with_zero_first(jnp.swapaxes(state_hgrn[j], -1, -2))
            o_a, st = hgrn2(proj, lb_all[l], hgrn_norm[j], s0, n_prompt_chunks=npc)
            hg.append(jnp.swapaxes(st, -1, -2))
            xcol = (2 * a_kw + 2 * a_vw) // b_w
            o_b, cvp, hp_, cvs, hs_ = rglru(
                proj, xcol, xcol + 1, rg_conv_w[j], rg_conv_b[j], rg_w_a[j], rg_b_a[j], rg_w_x[j], rg_b_x[j],
                rg_lambda[j], state_rg_conv[j], state_rglru[j].reshape(ns, 1, b_w), n_prompt_rows=p_rows, tm=tm_seq)
            rc.append((cvp, cvs))
            rl.append((hp_.reshape(1, b_w), hs_.reshape(ns, b_w)))
            x = matmul_res([(o_a, 0, a_vw, 0), (o_b, 0, b_w, a_vw)], ev_w_out_b, j, x, tm=tm, tn=512)
        else:
            w = od_w_in[j]
            offs = np_cumsum_offsets((c_w, c_w + c_bc, C_HEADS, d_key, d_key, d_val, d_val, d_rank))
            o_z, o_xbc, o_dt, o_q, o_k, o_v, o_g, o_gl = offs
            pad = lambda a: jnp.pad(a, ((0, 0), (0, LANES - a.shape[1])))
            w_re = jnp.concatenate([
                w[:, o_z:o_z + c_w], w[:, o_xbc:o_xbc + c_w], w[:, o_xbc + c_w:o_xbc + c_w + c_bc],
                w[:, o_q:o_q + d_key], w[:, o_k:o_k + d_key],
                pad(w[:, o_dt:o_dt + C_HEADS]), pad(w[:, o_gl:o_gl + d_rank]),
                jnp.zeros((d, 2 * LANES), w.dtype),
                w[:, o_v:o_v + d_val], w[:, o_g:o_g + d_val]], axis=1).astype(BF16)
            proj = norm_matmul(x, norm_mix[l], w_re, tm=tm, tn=512, out_dtype=F32)
            assert c_w == d_val
            base = 2 * c_w + c_bc
            col_dt = (base + 2 * d_key) // LANES
            cv0 = _with_zero_first(state_ssd_conv[j])
            s0c = _with_zero_first(jnp.transpose(state_ssd[j], (0, 3, 1, 2)).reshape(ns, C_STATE, c_w))
            rep = lambda a: jnp.repeat(a, c_hd)
            o_c, cv, stc = ssd(proj, (0, 1, 2 * c_w // c_bc, col_dt), ssd_conv_w[j], ssd_conv_b[j],
                               rep(ssd_dt_bias[j]), rep(ssd_A_log[j]), rep(ssd_D[j]), ssd_norm[j], cv0, s0c,
                               n_prompt_chunks=npc)
            sc.append(cv)
            ss.append(jnp.transpose(stc.reshape(1 + ns, C_STATE, C_HEADS, c_hd), (0, 2, 3, 1)))
            s0d = _with_zero_first(jnp.swapaxes(state_gla[j], -1, -2))
            gkw = jnp.pad(gla_gk_w[j], ((0, LANES - d_rank), (0, 0))).astype(BF16)
            v_start = base + 2 * d_key + 4 * LANES
            o_d, std = gla(proj, (base // d_key, base // d_key + 1, v_start // d_val, v_start // d_val + 1, col_dt + 1),
                           gkw, gla_gk_b[j], gla_norm[j], s0d, n_prompt_chunks=npc)
            gs.append(jnp.swapaxes(std, -1, -2))
            x = matmul_res([(o_c, 0, c_w, 0), (o_d, 0, d_val, c_w)], od_w_out_b, j, x, tm=tm, tn=512)

        mem = mem_prompt.reshape(nm, d)
        k_p = norm_matmul(mem, norm_mem[l], wk_b, l, tm=nm, tn=512, out_dtype=F32)
        v_p = norm_matmul(mem, norm_mem[l], wv_b, l, tm=nm, tn=512, out_dtype=F32)
        mem_k.append(k_p.reshape(1, nm, MEM_HEADS, d // MEM_HEADS))
        mem_v.append(v_p.reshape(1, nm, MEM_HEADS, d // MEM_HEADS))
        q = norm_matmul(x, norm_xattn[l], wq_b, l, tm=tm, tn=512, out_dtype=BF16)
        att = mem_attention(q, k_p, v_p, cache_mem_k, cache_mem_v, l, n_prompt_rows=p_rows, tm=tm_seq)
        x = matmul_res([(att, 0, d, 0)], wo_b, l, x, tm=tm, tn=512)

        prev_s = _block_cols(state_ffn_conv[l], ff, ffn_tn)
        act, stp, sts = conv_ffn_up(x, norm_ffn[l], w_up_b, l, ffn_conv_w[l], ffn_conv_b[l],
                                    prev_s, n_prompt_rows=p_rows, tm=tm, tn=ffn_tn)
        fc_p.append(_unblock_cols(stp, ff, ffn_tn))
        fc_s.append(_unblock_cols(sts, ff, ffn_tn))
        starts = _col_tile_starts(ff, ffn_tn)
        body = (len(starts) - 1) * ffn_tn
        down_terms = [(act, 0, body, 0), (act, body + (body - starts[-1]), ff - body, body)]
        x = matmul_res(down_terms, w_down_b, l, x, tm=tm // 2, tn=512)

    y_prompt, y_sample = rmsnorm_rows(x, final_norm, n_prompt_rows=p_rows, tm=tm)
    y_prompt = y_prompt.reshape(1, p_rows, d)
    y_sample = y_sample.reshape(ns, s_len, d)
    stack_p = lambda lst: jnp.stack([a[:1] for a in lst])
    stack_s = lambda lst: jnp.stack([a[1:] for a in lst])
    return (y_prompt, y_sample, jnp.stack(mem_k), jnp.stack(mem_v),
            stack_p(hg), jnp.stack([a[0] for a in rl]), jnp.stack([a[0] for a in rc]),
            stack_p(ss), stack_p(sc), stack_p(gs), jnp.stack(fc_p),
            stack_s(hg), jnp.stack([a[1] for a in rl]), jnp.stack([a[1] for a in rc]),
            stack_s(ss), stack_s(sc), stack_s(gs), jnp.stack(fc_s))


def np_cumsum_offsets(sizes):
    offs, acc = [], 0
    for s in sizes:
        offs.append(acc)
        acc += s
    return offs
```

```python
import functools
import math

import jax
import jax.numpy as jnp
from jax import lax
from jax.experimental import pallas as pl
from jax.experimental.pallas import tpu as pltpu

F32 = jnp.float32
BF16 = jnp.bfloat16
PROJ_DTYPE = BF16

CHUNK = 64
SUB = 16
EPS = 1e-6
LOG2_E = 1.4426950408889634
LANES = 128
SUBLANES = 8
VMEM_LIMIT = 56 * 1024 * 1024

MEM_HEADS = 4
A_HEADS = 8
B_HEADS = 8
C_HEADS = 16
C_GROUPS = 2
C_STATE = 128
D_HEADS = 4
RG_C = 8.0
D_GATE_NORM = 16.0

NT_DIMS = (((1,), (1,)), ((), ()))
TN_DIMS = (((0,), (0,)), ((), ()))


def _cparams(n_axes):
    return pltpu.CompilerParams(dimension_semantics=("arbitrary",) * n_axes,
                                vmem_limit_bytes=VMEM_LIMIT)


def _sigmoid(x):
    return 1.0 / (1.0 + jnp.exp(-x))


def _silu(x):
    return x * _sigmoid(x)


def _log_sigmoid(x):
    return jnp.minimum(x, 0.0) - jnp.log1p(jnp.exp(-jnp.abs(x)))


def _softplus(x):
    return jnp.maximum(x, 0.0) + jnp.log1p(jnp.exp(-jnp.abs(x)))


def _gelu_tanh(x):
    return 0.5 * x * (1.0 + jnp.tanh(math.sqrt(2.0 / math.pi) * (x + 0.044715 * (x * x * x))))


def _split3(x):
    hi = x.astype(BF16)
    r = x - hi.astype(F32)
    mid = r.astype(BF16)
    lo = (r - mid.astype(F32)).astype(BF16)
    return hi, mid, lo


def _dot01_left(t01, x):
    return sum(jnp.dot(t01, p, preferred_element_type=F32) for p in _split3(x))


def _dot01_right(x, e01):
    return sum(jnp.dot(p, e01, preferred_element_type=F32) for p in _split3(x))


def _tril01(n):
    r = lax.broadcasted_iota(jnp.int32, (n, n), 0)
    c = lax.broadcasted_iota(jnp.int32, (n, n), 1)
    return jnp.where(r >= c, 1.0, 0.0).astype(BF16)


def _causal_conv(x, prev, w, b, nseq, seqlen, fresh=None):
    width = w.shape[0]
    ch = x.shape[-1]
    head = 8
    tap = lambda d: w[width - 1 - d:width - d].reshape(1, 1, ch)
    x3 = x.reshape(nseq, seqlen, ch)
    rolled = [pltpu.roll(x, d, axis=0).reshape(nseq, seqlen, ch) for d in range(1, width)]
    y = b + tap(0) * x3
    for d in range(1, width):
        y = y + tap(d) * rolled[d - 1]
    row = lax.broadcasted_iota(jnp.int32, (nseq, head, ch), 1)
    yh = b + tap(0) * x3[:, :head]
    for d in range(1, width):
        xs = rolled[d - 1][:, :head]
        for r in range(d):
            p = width - 1 - d + r
            xs = jnp.where(row == r, prev[:, p:p + 1, :], xs)
        yh = yh + tap(d) * xs
    if fresh is not None:
        yh = jnp.where(fresh, yh, y[:, :head])
    y = jnp.concatenate([yh, y[:, head:]], axis=1)
    return y.reshape(nseq * seqlen, ch)


def _rmsnorm_rows(x, g):
    ms = jnp.mean(x * x, axis=-1, keepdims=True)
    return x * lax.rsqrt(ms + EPS) * g


def _row_specs(rows, tm, width, col=lambda j: 0):
    blk = (pl.Element(tm), pl.Element(width))
    row = lambda t: pl.multiple_of(t * tm, tm)
    if not isinstance(rows, tuple):
        return [pl.BlockSpec(blk, lambda i, j: (row(i), col(j)))], rows.shape[0], None
    npt = rows[0].shape[0] // tm
    specs = [pl.BlockSpec(blk, lambda i, j: (row(jnp.minimum(i, npt - 1)), col(j))),
             pl.BlockSpec(blk, lambda i, j: (row(jnp.maximum(i - npt, 0)), col(j)))]
    return specs, rows[0].shape[0] + rows[1].shape[0], npt


def _for_row_source(npt, fn, also=True):
    if npt is None and also is True:
        fn(0)
    elif npt is None:
        pl.when(also)(lambda: fn(0))
    else:
        i = pl.program_id(0)
        pl.when(also & (i < npt))(lambda: fn(0))
        pl.when(also & (i >= npt))(lambda: fn(1))


def _norm_matmul_body(*refs, npt):
    n_x = 1 if npt is None else 2
    x_refs = refs[:n_x]
    g_ref, w_ref, o_ref, xn_ref = refs[n_x:]

    def normalize(k):
        xn_ref[...] = _rmsnorm_rows(x_refs[k][...], g_ref[...]).astype(BF16)

    _for_row_source(npt, normalize, also=pl.program_id(1) == 0)
    o_ref[...] = jnp.dot(xn_ref[...], w_ref[...], preferred_element_type=F32).astype(o_ref.dtype)


def _weight_spec(w, layer, k_rows, k_block, tn):
    if layer is None:
        return pl.BlockSpec((k_rows, tn), lambda i, j: (k_block, j))
    return pl.BlockSpec((None, k_rows, tn), lambda i, j: (layer, k_block, j))


def norm_matmul(x, g, w, layer=None, *, tm, tn, out_dtype):
    d, n = w.shape[-2:]
    x_specs, m, npt = _row_specs(x, tm, d)
    xs = x if isinstance(x, tuple) else (x,)
    return pl.pallas_call(
        functools.partial(_norm_matmul_body, npt=npt),
        grid=(m // tm, n // tn),
        in_specs=x_specs + [pl.BlockSpec((1, d), lambda i, j: (0, 0)), _weight_spec(w, layer, d, 0, tn)],
        out_specs=pl.BlockSpec((tm, tn), lambda i, j: (i, j)),
        out_shape=jax.ShapeDtypeStruct((m, n), out_dtype),
        scratch_shapes=[pltpu.VMEM((tm, d), BF16)],
        compiler_params=_cparams(2),
        name="norm_matmul",
    )(*xs, g.reshape(1, d), w)


def _matmul_res_body(*refs, arity, npt):
    pos = 0
    terms = []
    for n_a in arity[:-1]:
        terms.append((refs[pos:pos + n_a], refs[pos + n_a]))
        pos += n_a + 1
    r_refs, o_ref = refs[pos:pos + arity[-1]], refs[-1]

    def compute(k):
        acc = r_refs[min(k, len(r_refs) - 1)][...]
        for a_refs, w_ref in terms:
            acc = acc + jnp.dot(a_refs[min(k, len(a_refs) - 1)][...], w_ref[0], preferred_element_type=F32)
        o_ref[...] = acc

    _for_row_source(npt, compute)


def matmul_res(terms, w, layer, res, *, tm, tn):
    n = w.shape[-1]
    el = pl.Element
    in_specs, operands, arity, npts = [], [], [], set()
    for a, a_col, k, w_row in terms:
        specs, m, npt = _row_specs(a, tm, k, col=lambda j, c=a_col: c)
        in_specs += specs + [pl.BlockSpec((el(1), el(k), el(tn)),
                                          lambda i, j, r=w_row: (layer, r, pl.multiple_of(j * tn, tn)))]
        operands += list(a if isinstance(a, tuple) else (a,)) + [w]
        arity.append(len(specs))
        npts.add(npt)
    r_specs, m, npt = _row_specs(res, tm, tn, col=lambda j: pl.multiple_of(j * tn, tn))
    arity.append(len(r_specs))
    npts = (npts | {npt}) - {None}
    assert len(npts) <= 1
    return pl.pallas_call(
        functools.partial(_matmul_res_body, arity=tuple(arity), npt=npts.pop() if npts else None),
        grid=(m // tm, n // tn),
        in_specs=in_specs + r_specs,
        out_specs=pl.BlockSpec((tm, tn), lambda i, j: (i, j)),
        out_shape=jax.ShapeDtypeStruct((m, n), F32),
        compiler_params=_cparams(2),
        name="matmul_res",
    )(*operands, *(res if isinstance(res, tuple) else (res,)))


def _rmsnorm_body(x_ref, g_ref, op_ref, os_ref, *, npt):
    y = _rmsnorm_rows(x_ref[...], g_ref[...])

    @pl.when(pl.program_id(0) < npt)
    def _():
        op_ref[...] = y

    @pl.when(pl.program_id(0) >= npt)
    def _():
        os_ref[...] = y


def rmsnorm_rows(x, g, *, n_prompt_rows, tm):
    m, d = x.shape
    npt = n_prompt_rows // tm
    return pl.pallas_call(
        functools.partial(_rmsnorm_body, npt=npt),
        grid=(m // tm,),
        in_specs=[pl.BlockSpec((tm, d), lambda i: (i, 0)), pl.BlockSpec((1, d), lambda i: (0, 0))],
        out_specs=[pl.BlockSpec((tm, d), lambda i: (jnp.minimum(i, npt - 1), 0)),
                   pl.BlockSpec((tm, d), lambda i: (jnp.maximum(i - npt, 0), 0))],
        out_shape=[jax.ShapeDtypeStruct((n_prompt_rows, d), F32),
                   jax.ShapeDtypeStruct((m - n_prompt_rows, d), F32)],
        compiler_params=_cparams(1),
        name="final_norm",
    )(x, g.reshape(1, d))


def _attn_head(qh, kh, vh):
    s = lax.dot_general(qh, kh, NT_DIMS, preferred_element_type=F32) * qh.shape[-1] ** -0.5
    p = jnp.exp(s - jnp.max(s, axis=-1, keepdims=True))
    p = p / jnp.sum(p, axis=-1, keepdims=True)
    return jnp.dot(p.astype(BF16), vh, preferred_element_type=F32)


def _attn_prompt_body(q_ref, k_ref, v_ref, o_ref, *, heads):
    hd = q_ref.shape[1] // heads
    outs = []
    for h in range(heads):
        sl = slice(h * hd, (h + 1) * hd)
        outs.append(_attn_head(q_ref[:, sl], k_ref[:, sl].astype(BF16), v_ref[:, sl].astype(BF16)))
    o_ref[...] = jnp.concatenate(outs, axis=-1).astype(o_ref.dtype)


def mem_attention(q, k_prompt, v_prompt, cache_k, cache_v, layer, *, heads, n_prompt_rows, tm):
    m, d = q.shape
    nm = k_prompt.shape[0]
    ns = cache_k.shape[1]
    npc = n_prompt_rows // CHUNK
    att = pl.pallas_call(
        functools.partial(_attn_prompt_body, heads=heads),
        grid=(n_prompt_rows // tm,),
        in_specs=[pl.BlockSpec((tm, d), lambda i: (i, 0)),
                  pl.BlockSpec((nm, d), lambda i: (0, 0)),
                  pl.BlockSpec((nm, d), lambda i: (0, 0))],
        out_specs=pl.BlockSpec((tm, d), lambda i: (i, 0)),
        out_shape=jax.ShapeDtypeStruct((n_prompt_rows, d), BF16),
        compiler_params=_cparams(1),
        name="mem_attention_prompt",
    )(q, k_prompt, v_prompt)
    cache_spec = pl.BlockSpec((None, None, nm, d), lambda s: (layer, s, 0, 0))
    att_s = pl.pallas_call(
        functools.partial(_attn_prompt_body, heads=heads),
        grid=(ns,),
        in_specs=[pl.BlockSpec((CHUNK, d), lambda s: (npc + s, 0)), cache_spec, cache_spec],
        out_specs=pl.BlockSpec((CHUNK, d), lambda s: (s, 0)),
        out_shape=jax.ShapeDtypeStruct((m - n_prompt_rows, d), BF16),
        compiler_params=_cparams(1),
        name="mem_attention_sample",
    )(q, cache_k, cache_v)
    return att, att_s


def _conv_ffn_body(x_ref, g_ref, wg_ref, wv_ref, wd_ref, cw_ref, cb_ref, prev_ref, o_ref, stp_ref, sts_ref,
                   xn_ref, *, n_prompt_tiles, nseq, col_splits, nj, repeat_cols):
    i = pl.program_id(0)
    j = pl.program_id(1)
    tn = wg_ref.shape[2]
    wm1 = stp_ref.shape[2]
    sw = tn // col_splits

    @pl.when((i == 0) & (j == 0))
    def _():
        stp_ref[...] = jnp.zeros(stp_ref.shape, F32)

    @pl.when(j == 0)
    def _():
        x = x_ref[...]
        xn_ref[...] = _rmsnorm_rows(x, g_ref[...]).astype(BF16)
        o_ref[...] = x

    xn = xn_ref[...]
    sample = jnp.where(i >= n_prompt_tiles, 1, 0)
    chunk = lax.broadcasted_iota(jnp.int32, (nseq, 1, sw), 0)
    from_state = chunk * 0 + sample == 1
    fresh = chunk * (1 - sample) == 0
    lane = lax.broadcasted_iota(jnp.int32, (1, sw), 1)
    last = jnp.where(j == nj - 1, 1, 0)
    acc = o_ref[...]
    hist = []
    for k in range(col_splits):
        cs = slice(k * sw, (k + 1) * sw)
        gate = jnp.dot(xn, wg_ref[0, :, cs], preferred_element_type=F32)
        val = jnp.dot(xn, wv_ref[0, :, cs], preferred_element_type=F32)
        prev = jnp.where(from_state, prev_ref[0, :, :, cs], stp_ref[j, :, :, cs])
        gc = _causal_conv(gate, prev, cw_ref[:, cs], cb_ref[:, cs], nseq, CHUNK, fresh=fresh)
        act = _silu(gc) * val
        if k * sw < repeat_cols:
            act = jnp.where((lane + k * sw) * last < repeat_cols * last, 0.0, act)
        acc = acc + jnp.dot(act.astype(BF16), wd_ref[0, cs, :], preferred_element_type=F32)
        hist.append(gate.reshape(nseq, CHUNK, sw)[:, CHUNK - wm1:, :])
    o_ref[...] = acc
    hist = jnp.concatenate(hist, axis=-1)

    @pl.when(i < n_prompt_tiles)
    def _():
        stp_ref[j] = hist[nseq - 1:]

    @pl.when(i >= n_prompt_tiles)
    def _():
        sts_ref[j, pl.ds((i - n_prompt_tiles) * nseq, nseq)] = hist


def conv_ffn(x, g, w_up, w_down, layer, conv_w, conv_b, prev_s, *, n_prompt_rows, tm, tn):
    m, d = x.shape
    ff = w_up.shape[2] // 2
    nj = -(-ff // tn)
    ns, wm1 = prev_s.shape[1], prev_s.shape[2]
    npt = n_prompt_rows // tm
    nseq = tm // CHUNK
    col = lambda j, base=0: pl.multiple_of(base + jnp.minimum(j * tn, ff - tn), LANES)
    assert ff % LANES == 0 and tn % LANES == 0
    el = pl.Element
    return pl.pallas_call(
        functools.partial(_conv_ffn_body, n_prompt_tiles=npt, nseq=nseq, col_splits=2, nj=nj,
                          repeat_cols=nj * tn - ff),
        grid=(m // tm, nj),
        in_specs=[pl.BlockSpec((tm, d), lambda i, j: (i, 0), pipeline_mode=pl.Buffered(1)),
                  pl.BlockSpec((1, d), lambda i, j: (0, 0)),
                  pl.BlockSpec((el(1), el(d), el(tn)), lambda i, j: (layer, 0, col(j))),
                  pl.BlockSpec((el(1), el(d), el(tn)), lambda i, j: (layer, 0, col(j, ff))),
                  pl.BlockSpec((el(1), el(tn), el(d)), lambda i, j: (layer, col(j), 0)),
                  pl.BlockSpec((el(wm1 + 1), el(tn)), lambda i, j: (0, col(j))),
                  pl.BlockSpec((el(1), el(tn)), lambda i, j: (0, col(j))),
                  pl.BlockSpec((1, nseq, wm1, tn), lambda i, j: (j, jnp.maximum(i - npt, 0), 0, 0))],
        out_specs=[pl.BlockSpec((tm, d), lambda i, j: (i, 0)),
                   pl.BlockSpec((nj, 1, wm1, tn), lambda i, j: (0, 0, 0, 0)),
                   pl.BlockSpec((nj, ns, wm1, tn), lambda i, j: (0, 0, 0, 0))],
        out_shape=[jax.ShapeDtypeStruct((m, d), F32),
                   jax.ShapeDtypeStruct((nj, 1, wm1, tn), F32),
                   jax.ShapeDtypeStruct((nj, ns, wm1, tn), F32)],
        scratch_shapes=[pltpu.VMEM((tm, d), BF16)],
        compiler_params=_cparams(2),
        name="conv_ffn",
    )(x, g.reshape(1, d), w_up, w_up, w_down, conv_w, conv_b.reshape(1, ff), prev_s)


def _rglru_body(x_ref, gate_ref, cw_ref, cb_ref, wa_ref, ba_ref, wx_ref, bx_ref, lam_ref,
                cv0_ref, h0_ref, o_ref, cvp_ref, hp_ref, cvs_ref, hs_ref, *, n_prompt_tiles, nseq, heads):
    i = pl.program_id(0)
    tm, ch = x_ref.shape
    bd = ch // heads
    x = x_ref[...].astype(F32)

    def run(prev, h0, n_seq, seqlen, first_row):
        xc = _causal_conv(x, prev, cw_ref[...], cb_ref[...], n_seq, seqlen)
        ra, rx = [], []
        for h in range(heads):
            xh = xc[:, h * bd:(h + 1) * bd].astype(BF16)
            ra.append(jnp.dot(xh, wa_ref[h], preferred_element_type=F32))
            rx.append(jnp.dot(xh, wx_ref[h], preferred_element_type=F32))
        r = _sigmoid(jnp.concatenate(ra, axis=-1) + ba_ref[...])
        ig = _sigmoid(jnp.concatenate(rx, axis=-1) + bx_ref[...])
        log_a = RG_C * r * _log_sigmoid(lam_ref[...])
        a = jnp.exp(log_a)
        m2 = -jnp.tanh(log_a) * (a * a + 1.0)
        mult = jnp.where(m2 > 0.0, m2 * lax.rsqrt(m2), 0.0)
        row = lax.broadcasted_iota(jnp.int32, (n_seq, seqlen, ch), 1)
        if first_row is not None:
            mult = jnp.where(row.reshape(tm, ch) + first_row == 0, 1.0, mult)
        u = xc * ig * mult
        ng = tm // SUBLANES
        ag = a.reshape(ng, SUBLANES, ch)
        ug = u.reshape(ng, SUBLANES, ch)
        grow = lax.broadcasted_iota(jnp.int32, (ng, SUBLANES, ch), 1)
        d = 1
        while d < SUBLANES:
            live = grow >= d
            a_sh = jnp.where(live, pltpu.roll(ag, d, axis=1), 1.0)
            u_sh = jnp.where(live, pltpu.roll(ug, d, axis=1), 0.0)
            ug = ug + ag * u_sh
            ag = ag * a_sh
            d *= 2
        groups_per_seq = seqlen // SUBLANES
        hs, last = [], []
        for gi in range(ng):
            if gi % groups_per_seq == 0:
                carry = h0[gi // groups_per_seq]
            hg = ug[gi] + ag[gi] * carry
            carry = hg[SUBLANES - 1:]
            hs.append(hg)
            if (gi + 1) % groups_per_seq == 0:
                last.append(carry)
        h_all = jnp.concatenate(hs, axis=0)
        o_ref[...] = (h_all * _gelu_tanh(gate_ref[...].astype(F32))).astype(o_ref.dtype)
        x3 = x.reshape(n_seq, seqlen, ch)
        return x3[:, seqlen - prev.shape[1]:, :], jnp.stack(last, axis=0)

    @pl.when(i == 0)
    def _():
        cvp_ref[...] = jnp.zeros(cvp_ref.shape, F32)
        hp_ref[...] = jnp.zeros(hp_ref.shape, F32)

    @pl.when(i < n_prompt_tiles)
    def _():
        cv, hl = run(cvp_ref[...], hp_ref[...], 1, tm, i * tm)
        cvp_ref[...] = cv
        hp_ref[...] = hl

    @pl.when(i >= n_prompt_tiles)
    def _():
        cv, hl = run(cv0_ref[...], h0_ref[...], nseq, CHUNK, None)
        s0 = (i - n_prompt_tiles) * nseq
        cvs_ref[pl.ds(s0, nseq)] = cv
        hs_ref[pl.ds(s0, nseq)] = hl


def rglru(proj, x_col, gate_col, conv_w, conv_b, w_a, b_a, w_x, b_x, lam, cv0, h0, *, n_prompt_rows, tm):
    m = proj.shape[0]
    heads, bd, _ = w_a.shape
    ch = heads * bd
    ns, wm1 = cv0.shape[0], cv0.shape[1]
    npt = n_prompt_rows // tm
    nseq = tm // CHUNK
    samp = lambda i: (jnp.maximum(i - npt, 0), 0, 0)
    const2 = lambda i: (0, 0)
    const3 = lambda i: (0, 0, 0)
    return pl.pallas_call(
        functools.partial(_rglru_body, n_prompt_tiles=npt, nseq=nseq, heads=heads),
        grid=(m // tm,),
        in_specs=[pl.BlockSpec((tm, ch), lambda i: (i, x_col)),
                  pl.BlockSpec((tm, ch), lambda i: (i, gate_col)),
                  pl.BlockSpec((wm1 + 1, ch), const2),
                  pl.BlockSpec((1, ch), const2),
                  pl.BlockSpec((heads, bd, bd), const3),
                  pl.BlockSpec((1, ch), const2),
                  pl.BlockSpec((heads, bd, bd), const3),
                  pl.BlockSpec((1, ch), const2),
                  pl.BlockSpec((1, ch), const2),
                  pl.BlockSpec((nseq, wm1, ch), samp),
                  pl.BlockSpec((nseq, 1, ch), samp)],
        out_specs=[pl.BlockSpec((tm, ch), lambda i: (i, 0)),
                   pl.BlockSpec((1, wm1, ch), const3),
                   pl.BlockSpec((1, 1, ch), const3),
                   pl.BlockSpec((ns, wm1, ch), const3),
                   pl.BlockSpec((ns, 1, ch), const3)],
        out_shape=[jax.ShapeDtypeStruct((m, ch), BF16),
                   jax.ShapeDtypeStruct((1, wm1, ch), F32),
                   jax.ShapeDtypeStruct((1, 1, ch), F32),
                   jax.ShapeDtypeStruct((ns, wm1, ch), F32),
                   jax.ShapeDtypeStruct((ns, 1, ch), F32)],
        compiler_params=_cparams(1),
        name="rglru",
    )(proj, proj, conv_w, conv_b.reshape(1, ch), w_a.astype(BF16), b_a.reshape(1, ch),
      w_x.astype(BF16), b_x.reshape(1, ch), lam.reshape(1, ch), cv0, h0)


def _gated_chunk(q, k, g, v, st_ref, kb_ref, heads):
    c = q.shape[0]
    dk = q.shape[1] // heads
    dv = v.shape[1] // heads
    nsub = c // SUB
    b = _dot01_left(_tril01(c), g) * LOG2_E
    b_end = b[c - 1:c, :]
    qb = (q * jnp.exp2(b)).astype(BF16)
    k_end = (k * jnp.exp2(b_end - b)).astype(BF16)
    dec_end = jnp.exp2(b_end)
    vb = v.astype(BF16)
    kb_ref[0] = k
    kb_ref[1] = b
    lane = lax.broadcasted_iota(jnp.int32, (SUB, c), 1)
    srow = lax.broadcasted_iota(jnp.int32, (SUB, c), 0)
    outs = []
    for h in range(heads):
        sk = slice(h * dk, (h + 1) * dk)
        sv = slice(h * dv, (h + 1) * dv)
        st = st_ref[0, h]
        o = lax.dot_general(qb[:, sk], st.astype(BF16), NT_DIMS, preferred_element_type=F32)
        qh, kh, bh, vh = q[:, sk], k[:, sk], b[:, sk], vb[:, sv]
        blocks = []
        for i in range(nsub):
            r0 = i * SUB
            qi, ki, bi = qh[r0:r0 + SUB], kh[r0:r0 + SUB], bh[r0:r0 + SUB]
            a = jnp.zeros((SUB, c), F32)
            for s in range(SUB):
                e = jnp.exp2(bi - kb_ref[1, r0 + s:r0 + s + 1, sk])
                col = jnp.sum(qi * (kb_ref[0, r0 + s:r0 + s + 1, sk] * e), axis=-1, keepdims=True)
                a = jnp.where(lane == r0 + s, col, a)
            a = jnp.where(srow >= lane - r0, a, 0.0)
            if i > 0:
                ri = bh[r0:r0 + 1]
                qt = (qi * jnp.exp2(bi - ri)).astype(BF16)
                kt = (kh * jnp.exp2(ri - bh)).astype(BF16)
                a = jnp.where(lane < r0, lax.dot_general(qt, kt, NT_DIMS, preferred_element_type=F32), a)
            blocks.append(a)
        a_full = jnp.concatenate(blocks, axis=0).astype(BF16)
        o = o + jnp.dot(a_full, vh, preferred_element_type=F32)
        st_ref[0, h] = st * dec_end[:, sk] + lax.dot_general(
            vh, k_end[:, sk], TN_DIMS, preferred_element_type=F32)
        outs.append(o)
    return outs


def _head_norm_gate(outs, ng_ref, gate_ref):
    res = []
    for h, o in enumerate(outs):
        dv = o.shape[-1]
        sv = slice(h * dv, (h + 1) * dv)
        res.append(_rmsnorm_rows(o, ng_ref[:, sv]) * _silu(gate_ref[:, sv].astype(F32)))
    return jnp.concatenate(res, axis=-1)


def _hgrn_body(q_ref, f_ref, v_ref, gate_ref, lb_ref, ng_ref, s0_ref, o_ref, st_ref, kb_ref,
               *, n_prompt_chunks, heads):
    c = pl.program_id(0)

    @pl.when((c == 0) | (c >= n_prompt_chunks))
    def _():
        st_ref[...] = s0_ref[...]

    lb = lb_ref[...]
    k = (1.0 - lb) / (1.0 + jnp.exp(f_ref[...].astype(F32)))
    g = jnp.log1p(-k)
    outs = _gated_chunk(_silu(q_ref[...].astype(F32)), k, g, v_ref[...], st_ref, kb_ref, heads)
    o_ref[...] = _head_norm_gate(outs, ng_ref, gate_ref).astype(o_ref.dtype)


def _gla_body(q_ref, k_ref, v_ref, gate_ref, gl_ref, gkw_ref, gkb_ref, ng_ref, s0_ref, o_ref, st_ref, kb_ref,
              *, n_prompt_chunks, heads):
    c = pl.program_id(0)

    @pl.when((c == 0) | (c >= n_prompt_chunks))
    def _():
        st_ref[...] = s0_ref[...]

    dk = q_ref.shape[1] // heads
    z = jnp.dot(gl_ref[...].astype(BF16), gkw_ref[...], preferred_element_type=F32) + gkb_ref[...]
    g = _log_sigmoid(z) * (1.0 / D_GATE_NORM)
    outs = _gated_chunk(q_ref[...].astype(F32) * dk ** -0.5, k_ref[...].astype(F32), g, v_ref[...], st_ref, kb_ref, heads)
    o_ref[...] = _head_norm_gate(outs, ng_ref, gate_ref).astype(o_ref.dtype)


def _chunk_state_map(n_prompt_chunks, ndim):
    return lambda c: (jnp.maximum(c - (n_prompt_chunks - 1), 0),) + (0,) * (ndim - 1)


def hgrn2(proj, lb, norm_g, s0, *, n_prompt_chunks):
    m = proj.shape[0]
    _, heads, dv, dk = s0.shape
    kw, vw = heads * dk, heads * dv
    smap = _chunk_state_map(n_prompt_chunks, 4)
    row = lambda c: (0, 0)
    return pl.pallas_call(
        functools.partial(_hgrn_body, n_prompt_chunks=n_prompt_chunks, heads=heads),
        grid=(m // CHUNK,),
        in_specs=[pl.BlockSpec((CHUNK, kw), lambda c: (c, 0)),
                  pl.BlockSpec((CHUNK, kw), lambda c: (c, 1)),
                  pl.BlockSpec((CHUNK, vw), lambda c: (c, 2 * kw // vw)),
                  pl.BlockSpec((CHUNK, vw), lambda c: (c, 2 * kw // vw + 1)),
                  pl.BlockSpec((1, kw), row),
                  pl.BlockSpec((1, vw), row),
                  pl.BlockSpec((1, heads, dv, dk), smap)],
        out_specs=[pl.BlockSpec((CHUNK, vw), lambda c: (c, 0)),
                   pl.BlockSpec((1, heads, dv, dk), smap)],
        out_shape=[jax.ShapeDtypeStruct((m, vw), BF16),
                   jax.ShapeDtypeStruct(s0.shape, F32)],
        scratch_shapes=[pltpu.VMEM((2, CHUNK, kw), F32)],
        compiler_params=_cparams(1),
        name="hgrn2",
    )(proj, proj, proj, proj, lb.reshape(1, kw), norm_g.reshape(1, vw), s0)


def gla(proj, cols, gk_w, gk_b, norm_g, s0, *, n_prompt_chunks):
    m = proj.shape[0]
    _, heads, dv, dk = s0.shape
    kw, vw = heads * dk, heads * dv
    cq, ck, cv, cg, cgl = cols
    smap = _chunk_state_map(n_prompt_chunks, 4)
    row = lambda c: (0, 0)
    return pl.pallas_call(
        functools.partial(_gla_body, n_prompt_chunks=n_prompt_chunks, heads=heads),
        grid=(m // CHUNK,),
        in_specs=[pl.BlockSpec((CHUNK, kw), lambda c: (c, cq)),
                  pl.BlockSpec((CHUNK, kw), lambda c: (c, ck)),
                  pl.BlockSpec((CHUNK, vw), lambda c: (c, cv)),
                  pl.BlockSpec((CHUNK, vw), lambda c: (c, cg)),
                  pl.BlockSpec((CHUNK, LANES), lambda c: (c, cgl)),
                  pl.BlockSpec((LANES, kw), row),
                  pl.BlockSpec((1, kw), row),
                  pl.BlockSpec((1, vw), row),
                  pl.BlockSpec((1, heads, dv, dk), smap)],
        out_specs=[pl.BlockSpec((CHUNK, vw), lambda c: (c, 0)),
                   pl.BlockSpec((1, heads, dv, dk), smap)],
        out_shape=[jax.ShapeDtypeStruct((m, vw), BF16),
                   jax.ShapeDtypeStruct(s0.shape, F32)],
        scratch_shapes=[pltpu.VMEM((2, CHUNK, kw), F32)],
        compiler_params=_cparams(1),
        name="gla",
    )(proj, proj, proj, proj, proj, gk_w, gk_b.reshape(1, kw), norm_g.reshape(1, vw), s0)


def _ssd_body(z_ref, x_ref, bc_ref, dt_ref, cw_ref, cb_ref, dtb_ref, alog_ref, dsk_ref, ng_ref,
              cv0_ref, s0_ref, o_ref, cv_ref, st_ref, *, n_prompt_chunks, heads, groups):
    c = pl.program_id(0)

    @pl.when((c == 0) | (c >= n_prompt_chunks))
    def _():
        st_ref[...] = s0_ref[...]
        cv_ref[...] = cv0_ref[...]

    rows, xw_ = x_ref.shape
    hp = xw_ // heads
    n = bc_ref.shape[1] // (2 * groups)
    gw = xw_ // groups
    x_raw, bc_raw = x_ref[...].astype(F32), bc_ref[...].astype(F32)
    prev = cv_ref[...]
    cw, cb = cw_ref[...], cb_ref[...]
    xs = _silu(_causal_conv(x_raw, prev[:, :, :xw_], cw[:, :xw_], cb[:, :xw_], 1, rows))
    bcs = _silu(_causal_conv(bc_raw, prev[:, :, xw_:], cw[:, xw_:], cb[:, xw_:], 1, rows))
    wm1 = prev.shape[1]
    cv_ref[0, :, :xw_] = x_raw[rows - wm1:, :]
    cv_ref[0, :, xw_:] = bc_raw[rows - wm1:, :]

    erow = lax.broadcasted_iota(jnp.int32, (LANES, xw_), 0)
    ecol = lax.broadcasted_iota(jnp.int32, (LANES, xw_), 1)
    expand = jnp.where(ecol // hp == erow, 1.0, 0.0).astype(BF16)
    dt = _softplus(_dot01_right(dt_ref[...].astype(F32), expand) + dtb_ref[...])
    dta = dt * (-jnp.exp(alog_ref[...]))
    cum = _dot01_left(_tril01(rows), dta)
    cum_end = cum[rows - 1:rows, :]
    xdt = xs * dt
    xdt_b = xdt.astype(BF16)
    x_end = (xdt * jnp.exp(cum_end - cum)).astype(BF16)
    e_cum = jnp.exp(cum)
    dec_end = jnp.exp(cum_end)

    prow = lax.broadcasted_iota(jnp.int32, (rows, LANES), 0)
    plane = lax.broadcasted_iota(jnp.int32, (rows, LANES), 1)
    assert hp == rows and LANES % hp == 0
    pair = LANES // hp
    psrc = plane % hp
    y_parts = []
    for g in range(groups):
        bg = bcs[:, g * n:(g + 1) * n].astype(BF16)
        cg = bcs[:, (groups + g) * n:(groups + g + 1) * n].astype(BF16)
        gl = slice(g * gw, (g + 1) * gw)
        st = st_ref[0, :, gl]
        y_inter = jnp.dot(cg, st.astype(BF16), preferred_element_type=F32) * e_cum[:, gl]
        cb_rep = lax.dot_general(cg, jnp.concatenate([bg] * pair, axis=0), NT_DIMS, preferred_element_type=F32)
        y_intra = []
        for sl in range(gw // LANES):
            lo = g * gw + sl * LANES
            cs = cum[:, lo:lo + LANES]
            cdiag = jnp.sum(jnp.where(prow == psrc, cs, 0.0), axis=0, keepdims=True)
            decay = jnp.exp(jnp.minimum(cs - cdiag, 0.0))
            mt = jnp.where(prow >= psrc, cb_rep * decay, 0.0).astype(BF16)
            xp = xdt_b[:, lo:lo + LANES]
            rhs = jnp.concatenate(
                [jnp.where(plane // hp == p, xp, jnp.zeros_like(xp)) for p in range(pair)], axis=0)
            y_intra.append(jnp.dot(mt, rhs, preferred_element_type=F32))
        y_parts.append(jnp.concatenate(y_intra, axis=-1) + y_inter)
        st_ref[0, :, gl] = st * dec_end[:, gl] + lax.dot_general(
            bg, x_end[:, gl], TN_DIMS, preferred_element_type=F32)
    y = jnp.concatenate(y_parts, axis=-1) + dsk_ref[...] * xs
    yz = y * _silu(z_ref[...].astype(F32))
    res = []
    for g in range(groups):
        gl = slice(g * gw, (g + 1) * gw)
        res.append(_rmsnorm_rows(yz[:, gl], ng_ref[:, gl]))
    o_ref[...] = jnp.concatenate(res, axis=-1).astype(o_ref.dtype)


def ssd(proj, cols, conv_w, conv_b, dt_bias_x, a_log_x, d_skip_x, norm_g, cv0, s0, *, n_prompt_chunks):
    m = proj.shape[0]
    xw_ = s0.shape[2]
    n = s0.shape[1]
    bcw = 2 * C_GROUPS * n
    cz, cx, cbc, cdt = cols
    wm1 = cv0.shape[1]
    smap3 = _chunk_state_map(n_prompt_chunks, 3)
    row = lambda c: (0, 0)
    return pl.pallas_call(
        functools.partial(_ssd_body, n_prompt_chunks=n_prompt_chunks, heads=C_HEADS, groups=C_GROUPS),
        grid=(m // CHUNK,),
        in_specs=[pl.BlockSpec((CHUNK, xw_), lambda c: (c, cz)),
                  pl.BlockSpec((CHUNK, xw_), lambda c: (c, cx)),
                  pl.BlockSpec((CHUNK, bcw), lambda c: (c, cbc)),
                  pl.BlockSpec((CHUNK, LANES), lambda c: (c, cdt)),
                  pl.BlockSpec((wm1 + 1, xw_ + bcw), row),
                  pl.BlockSpec((1, xw_ + bcw), row),
                  pl.BlockSpec((1, xw_), row),
                  pl.BlockSpec((1, xw_), row),
                  pl.BlockSpec((1, xw_), row),
                  pl.BlockSpec((1, xw_), row),
                  pl.BlockSpec((1, wm1, xw_ + bcw), smap3),
                  pl.BlockSpec((1, n, xw_), smap3)],
        out_specs=[pl.BlockSpec((CHUNK, xw_), lambda c: (c, 0)),
                   pl.BlockSpec((1, wm1, xw_ + bcw), smap3),
                   pl.BlockSpec((1, n, xw_), smap3)],
        out_shape=[jax.ShapeDtypeStruct((m, xw_), BF16),
                   jax.ShapeDtypeStruct(cv0.shape, F32),
                   jax.ShapeDtypeStruct(s0.shape, F32)],
        compiler_params=_cparams(1),
        name="ssd",
    )(proj, proj, proj, proj, conv_w, conv_b.reshape(1, xw_ + bcw), dt_bias_x.reshape(1, xw_),
      a_log_x.reshape(1, xw_), d_skip_x.reshape(1, xw_), norm_g.reshape(1, xw_), cv0, s0)


def _with_zero_first(s):
    return jnp.concatenate([jnp.zeros((1,) + s.shape[1:], s.dtype), s], axis=0)


def _col_tile_starts(width, tn):
    return [min(j * tn, width - tn) for j in range(-(-width // tn))]


def _block_cols(a, width, tn):
    return jnp.stack([a[..., o:o + tn] for o in _col_tile_starts(width, tn)], axis=0)


def _unblock_cols(b, width, tn):
    return jnp.concatenate([b[j][..., j * tn - o:] for j, o in enumerate(_col_tile_starts(width, tn))], axis=-1)


def kernel(x_prompt, x_sample, mem_prompt, cache_mem_k, cache_mem_v, state_hgrn, state_rglru, state_rg_conv, state_ssd, state_ssd_conv, state_gla, state_ffn_conv, norm_mix, norm_xattn, norm_mem, norm_ffn, ev_w_in, hgrn_lb_logits, hgrn_norm, rg_conv_w, rg_conv_b, rg_w_a, rg_b_a, rg_w_x, rg_b_x, rg_lambda, ev_w_out, od_w_in, ssd_conv_w, ssd_conv_b, ssd_dt_bias, ssd_A_log, ssd_D, ssd_norm, gla_gk_w, gla_gk_b, gla_norm, od_w_out, xa_wq, xa_wk, xa_wv, xa_wo, ffn_up, ffn_conv_w, ffn_conv_b, ffn_down, final_norm):
    bp, p_rows, d = x_prompt.shape
    ns, s_len, _ = x_sample.shape
    assert bp == 1 and s_len == CHUNK and p_rows % CHUNK == 0
    depth = norm_mix.shape[0]
    s_rows = ns * s_len
    m = p_rows + s_rows
    npc = p_rows // CHUNK
    nm = mem_prompt.shape[1]
    tm = math.gcd(math.gcd(p_rows, s_rows), 1024)
    tm_seq = math.gcd(tm, 256)

    x = (x_prompt.reshape(p_rows, d), x_sample.reshape(s_rows, d))
    lb_all = jnp.cumsum(jax.nn.softmax(hgrn_lb_logits.astype(F32), axis=0), axis=0)
    ev_w_in_b, ev_w_out_b, od_w_out_b = ev_w_in.astype(BF16), ev_w_out.astype(BF16), od_w_out.astype(BF16)
    wq_b, wk_b, wv_b, wo_b = xa_wq.astype(BF16), xa_wk.astype(BF16), xa_wv.astype(BF16), xa_wo.astype(BF16)
    w_up_b, w_down_b = ffn_up.astype(BF16), ffn_down.astype(BF16)
    cache_k_b = cache_mem_k.reshape(depth, ns, nm, d).astype(BF16)
    cache_v_b = cache_mem_v.reshape(depth, ns, nm, d).astype(BF16)

    a_kw = hgrn_lb_logits.shape[1]
    a_vw = hgrn_norm.shape[1]
    b_w = rg_lambda.shape[1]
    c_w = ssd_norm.shape[1]
    c_bc = 2 * C_GROUPS * C_STATE
    c_hd = c_w // C_HEADS
    d_val = gla_norm.shape[1]
    d_key = gla_gk_b.shape[1]
    d_rank = gla_gk_w.shape[1]
    ff = ffn_conv_b.shape[1]
    ffn_tn = 512

    mem_k, mem_v = [], []
    hg, rl, rc, ss, sc, gs, fc_p, fc_s = [], [], [], [], [], [], [], []
    for l in range(depth):
        j = l // 2
        if l % 2 == 0:
            proj = norm_matmul(x, norm_mix[l], ev_w_in_b, j, tm=tm, tn=512, out_dtype=PROJ_DTYPE)
            s0 = _with_zero_first(jnp.swapaxes(state_hgrn[j], -1, -2))
            o_a, st = hgrn2(proj, lb_all[l], hgrn_norm[j], s0, n_prompt_chunks=npc)
            hg.append(jnp.swapaxes(st, -1, -2))
            xcol = (2 * a_kw + 2 * a_vw) // b_w
            o_b, cvp, hp_, cvs, hs_ = rglru(
                proj, xcol, xcol + 1, rg_conv_w[j], rg_conv_b[j], rg_w_a[j], rg_b_a[j], rg_w_x[j], rg_b_x[j],
                rg_lambda[j], state_rg_conv[j], state_rglru[j].reshape(ns, 1, b_w), n_prompt_rows=p_rows, tm=tm_seq)
            rc.append((cvp, cvs))
            rl.append((hp_.reshape(1, b_w), hs_.reshape(ns, b_w)))
            x = matmul_res([(o_a, 0, a_vw, 0), (o_b, 0, b_w, a_vw)], ev_w_out_b, j, x, tm=tm, tn=512)
        else:
            w = od_w_in[j]
            offs = np_cumsum_offsets((c_w, c_w + c_bc, C_HEADS, d_key, d_key, d_val, d_val, d_rank))
            o_z, o_xbc, o_dt, o_q, o_k, o_v, o_g, o_gl = offs
            pad = lambda a: jnp.pad(a, ((0, 0), (0, LANES - a.shape[1])))
            w_re = jnp.concatenate([
                w[:, o_z:o_z + c_w], w[:, o_xbc:o_xbc + c_w], w[:, o_xbc + c_w:o_xbc + c_w + c_bc],
                w[:, o_q:o_q + d_key], w[:, o_k:o_k + d_key],
                pad(w[:, o_dt:o_dt + C_HEADS]), pad(w[:, o_gl:o_gl + d_rank]),
                jnp.zeros((d, 2 * LANES), w.dtype),
                w[:, o_v:o_v + d_val], w[:, o_g:o_g + d_val]], axis=1).astype(BF16)
            proj = norm_matmul(x, norm_mix[l], w_re, tm=tm, tn=512, out_dtype=PROJ_DTYPE)
            assert c_w == d_val
            base = 2 * c_w + c_bc
            col_dt = (base + 2 * d_key) // LANES
            cv0 = _with_zero_first(state_ssd_conv[j])
            s0c = _with_zero_first(jnp.transpose(state_ssd[j], (0, 3, 1, 2)).reshape(ns, C_STATE, c_w))
            rep = lambda a: jnp.repeat(a, c_hd)
            o_c, cv, stc = ssd(proj, (0, 1, 2 * c_w // c_bc, col_dt), ssd_conv_w[j], ssd_conv_b[j],
                               rep(ssd_dt_bias[j]), rep(ssd_A_log[j]), rep(ssd_D[j]), ssd_norm[j], cv0, s0c,
                               n_prompt_chunks=npc)
            sc.append(cv)
            ss.append(jnp.transpose(stc.reshape(1 + ns, C_STATE, C_HEADS, c_hd), (0, 2, 3, 1)))
            s0d = _with_zero_first(jnp.swapaxes(state_gla[j], -1, -2))
            gkw = jnp.pad(gla_gk_w[j], ((0, LANES - d_rank), (0, 0))).astype(BF16)
            v_start = base + 2 * d_key + 4 * LANES
            o_d, std = gla(proj, (base // d_key, base // d_key + 1, v_start // d_val, v_start // d_val + 1, col_dt + 1),
                           gkw, gla_gk_b[j], gla_norm[j], s0d, n_prompt_chunks=npc)
            gs.append(jnp.swapaxes(std, -1, -2))
            x = matmul_res([(o_c, 0, c_w, 0), (o_d, 0, d_val, c_w)], od_w_out_b, j, x, tm=tm, tn=512)

        mem = mem_prompt.reshape(nm, d)
        k_p = norm_matmul(mem, norm_mem[l], wk_b, l, tm=nm, tn=512, out_dtype=F32)
        v_p = norm_matmul(mem, norm_mem[l], wv_b, l, tm=nm, tn=512, out_dtype=F32)
        mem_k.append(k_p.reshape(1, nm, MEM_HEADS, d // MEM_HEADS))
        mem_v.append(v_p.reshape(1, nm, MEM_HEADS, d // MEM_HEADS))
        q = norm_matmul(x, norm_xattn[l], wq_b, l, tm=tm, tn=512, out_dtype=BF16)
        att = mem_attention(q, k_p, v_p, cache_k_b, cache_v_b, l, heads=MEM_HEADS, n_prompt_rows=p_rows, tm=tm_seq)
        x = matmul_res([(att, 0, d, 0)], wo_b, l, x, tm=tm, tn=512)

        prev_s = _block_cols(state_ffn_conv[l], ff, ffn_tn)
        x, stp, sts = conv_ffn(x, norm_ffn[l], w_up_b, w_down_b, l, ffn_conv_w[l], ffn_conv_b[l],
                               prev_s, n_prompt_rows=p_rows, tm=tm, tn=ffn_tn)
        fc_p.append(_unblock_cols(stp, ff, ffn_tn))
        fc_s.append(_unblock_cols(sts, ff, ffn_tn))

    y_prompt, y_sample = rmsnorm_rows(x, final_norm, n_prompt_rows=p_rows, tm=tm)
    y_prompt = y_prompt.reshape(1, p_rows, d)
    y_sample = y_sample.reshape(ns, s_len, d)
    stack_p = lambda lst: jnp.stack([a[:1] for a in lst])
    stack_s = lambda lst: jnp.stack([a[1:] for a in lst])
    return (y_prompt, y_sample, jnp.stack(mem_k), jnp.stack(mem_v),
            stack_p(hg), jnp.stack([a[0] for a in rl]), jnp.stack([a[0] for a in rc]),
            stack_p(ss), stack_p(sc), stack_p(gs), jnp.stack(fc_p),
            stack_s(hg), jnp.stack([a[1] for a in rl]), jnp.stack([a[1] for a in rc]),
            stack_s(ss), stack_s(sc), stack_s(gs), jnp.stack(fc_s))


def np_cumsum_offsets(sizes):
    offs, acc = [], 0
    for s in sizes:
        offs.append(acc)
        acc += s
    return offs
```

```python
import functools
import math

import jax
import jax.numpy as jnp
from jax import lax
from jax.experimental import pallas as pl
from jax.experimental.pallas import tpu as pltpu

F32 = jnp.float32
BF16 = jnp.bfloat16
PROJ_DTYPE = F32

CHUNK = 64
SUB = 16
EPS = 1e-6
LOG2_E = 1.4426950408889634
LANES = 128
SUBLANES = 8
VMEM_LIMIT = 56 * 1024 * 1024

MEM_HEADS = 4
A_HEADS = 8
B_HEADS = 8
C_HEADS = 16
C_GROUPS = 2
C_STATE = 128
D_HEADS = 4
RG_C = 8.0
D_GATE_NORM = 16.0

NT_DIMS = (((1,), (1,)), ((), ()))
TN_DIMS = (((0,), (0,)), ((), ()))


def _cparams(n_axes):
    return pltpu.CompilerParams(dimension_semantics=("arbitrary",) * n_axes,
                                vmem_limit_bytes=VMEM_LIMIT)


def _sigmoid(x):
    return 1.0 / (1.0 + jnp.exp(-x))


def _silu(x):
    return x * _sigmoid(x)


def _log_sigmoid(x):
    return jnp.minimum(x, 0.0) - jnp.log1p(jnp.exp(-jnp.abs(x)))


def _softplus(x):
    return jnp.maximum(x, 0.0) + jnp.log1p(jnp.exp(-jnp.abs(x)))


def _gelu_tanh(x):
    return 0.5 * x * (1.0 + jnp.tanh(math.sqrt(2.0 / math.pi) * (x + 0.044715 * (x * x * x))))


def _split3(x):
    hi = x.astype(BF16)
    r = x - hi.astype(F32)
    mid = r.astype(BF16)
    lo = (r - mid.astype(F32)).astype(BF16)
    return hi, mid, lo


def _dot01_left(t01, x):
    return sum(jnp.dot(t01, p, preferred_element_type=F32) for p in _split3(x))


def _dot01_right(x, e01):
    return sum(jnp.dot(p, e01, preferred_element_type=F32) for p in _split3(x))


def _tril01(n):
    r = lax.broadcasted_iota(jnp.int32, (n, n), 0)
    c = lax.broadcasted_iota(jnp.int32, (n, n), 1)
    return jnp.where(r >= c, 1.0, 0.0).astype(BF16)


def _causal_conv(x, prev, w, b, nseq, seqlen, fresh=None):
    width = w.shape[0]
    ch = x.shape[-1]
    head = 8
    tap = lambda d: w[width - 1 - d:width - d].reshape(1, 1, ch)
    x3 = x.reshape(nseq, seqlen, ch)
    rolled = [pltpu.roll(x, d, axis=0).reshape(nseq, seqlen, ch) for d in range(1, width)]
    y = b + tap(0) * x3
    for d in range(1, width):
        y = y + tap(d) * rolled[d - 1]
    row = lax.broadcasted_iota(jnp.int32, (nseq, head, ch), 1)
    yh = b + tap(0) * x3[:, :head]
    for d in range(1, width):
        xs = rolled[d - 1][:, :head]
        for r in range(d):
            p = width - 1 - d + r
            xs = jnp.where(row == r, prev[:, p:p + 1, :], xs)
        yh = yh + tap(d) * xs
    if fresh is not None:
        yh = jnp.where(fresh, yh, y[:, :head])
    y = jnp.concatenate([yh, y[:, head:]], axis=1)
    return y.reshape(nseq * seqlen, ch)


def _rmsnorm_rows(x, g):
    ms = jnp.mean(x * x, axis=-1, keepdims=True)
    return x * lax.rsqrt(ms + EPS) * g


def _row_specs(rows, tm, width, col=lambda j: 0):
    blk = (pl.Element(tm), pl.Element(width))
    row = lambda t: pl.multiple_of(t * tm, tm)
    if not isinstance(rows, tuple):
        return [pl.BlockSpec(blk, lambda i, j: (row(i), col(j)))], rows.shape[0], None
    npt = rows[0].shape[0] // tm
    specs = [pl.BlockSpec(blk, lambda i, j: (row(jnp.minimum(i, npt - 1)), col(j))),
             pl.BlockSpec(blk, lambda i, j: (row(jnp.maximum(i - npt, 0)), col(j)))]
    return specs, rows[0].shape[0] + rows[1].shape[0], npt


def _for_row_source(npt, fn, also=True):
    if npt is None and also is True:
        fn(0)
    elif npt is None:
        pl.when(also)(lambda: fn(0))
    else:
        i = pl.program_id(0)
        pl.when(also & (i < npt))(lambda: fn(0))
        pl.when(also & (i >= npt))(lambda: fn(1))


def _norm_matmul_body(*refs, npt):
    n_x = 1 if npt is None else 2
    x_refs = refs[:n_x]
    g_ref, w_ref, o_ref, xn_ref = refs[n_x:]

    def normalize(k):
        xn_ref[...] = _rmsnorm_rows(x_refs[k][...], g_ref[...]).astype(BF16)

    _for_row_source(npt, normalize, also=pl.program_id(1) == 0)
    o_ref[...] = jnp.dot(xn_ref[...], w_ref[...], preferred_element_type=F32).astype(o_ref.dtype)


def _weight_spec(w, layer, k_rows, k_block, tn):
    if layer is None:
        return pl.BlockSpec((k_rows, tn), lambda i, j: (k_block, j))
    return pl.BlockSpec((None, k_rows, tn), lambda i, j: (layer, k_block, j))


def norm_matmul(x, g, w, layer=None, *, tm, tn, out_dtype):
    d, n = w.shape[-2:]
    x_specs, m, npt = _row_specs(x, tm, d)
    xs = x if isinstance(x, tuple) else (x,)
    return pl.pallas_call(
        functools.partial(_norm_matmul_body, npt=npt),
        grid=(m // tm, n // tn),
        in_specs=x_specs + [pl.BlockSpec((1, d), lambda i, j: (0, 0)), _weight_spec(w, layer, d, 0, tn)],
        out_specs=pl.BlockSpec((tm, tn), lambda i, j: (i, j)),
        out_shape=jax.ShapeDtypeStruct((m, n), out_dtype),
        scratch_shapes=[pltpu.VMEM((tm, d), BF16)],
        compiler_params=_cparams(2),
        name="norm_matmul",
    )(*xs, g.reshape(1, d), w)


def _matmul_res_body(*refs, arity, npt):
    pos = 0
    terms = []
    for n_a in arity[:-1]:
        terms.append((refs[pos:pos + n_a], refs[pos + n_a]))
        pos += n_a + 1
    r_refs, o_ref = refs[pos:pos + arity[-1]], refs[-1]

    def compute(k):
        acc = r_refs[min(k, len(r_refs) - 1)][...]
        for a_refs, w_ref in terms:
            acc = acc + jnp.dot(a_refs[min(k, len(a_refs) - 1)][...], w_ref[0], preferred_element_type=F32)
        o_ref[...] = acc

    _for_row_source(npt, compute)


def matmul_res(terms, w, layer, res, *, tm, tn):
    n = w.shape[-1]
    el = pl.Element
    in_specs, operands, arity, npts = [], [], [], set()
    for a, a_col, k, w_row in terms:
        specs, m, npt = _row_specs(a, tm, k, col=lambda j, c=a_col: c)
        in_specs += specs + [pl.BlockSpec((el(1), el(k), el(tn)),
                                          lambda i, j, r=w_row: (layer, r, pl.multiple_of(j * tn, tn)))]
        operands += list(a if isinstance(a, tuple) else (a,)) + [w]
        arity.append(len(specs))
        npts.add(npt)
    r_specs, m, npt = _row_specs(res, tm, tn, col=lambda j: pl.multiple_of(j * tn, tn))
    arity.append(len(r_specs))
    npts = (npts | {npt}) - {None}
    assert len(npts) <= 1
    return pl.pallas_call(
        functools.partial(_matmul_res_body, arity=tuple(arity), npt=npts.pop() if npts else None),
        grid=(m // tm, n // tn),
        in_specs=in_specs + r_specs,
        out_specs=pl.BlockSpec((tm, tn), lambda i, j: (i, j)),
        out_shape=jax.ShapeDtypeStruct((m, n), F32),
        compiler_params=_cparams(2),
        name="matmul_res",
    )(*operands, *(res if isinstance(res, tuple) else (res,)))


def _rmsnorm_body(x_ref, g_ref, op_ref, os_ref, *, npt):
    y = _rmsnorm_rows(x_ref[...], g_ref[...])

    @pl.when(pl.program_id(0) < npt)
    def _():
        op_ref[...] = y

    @pl.when(pl.program_id(0) >= npt)
    def _():
        os_ref[...] = y


def rmsnorm_rows(x, g, *, n_prompt_rows, tm):
    m, d = x.shape
    npt = n_prompt_rows // tm
    return pl.pallas_call(
        functools.partial(_rmsnorm_body, npt=npt),
        grid=(m // tm,),
        in_specs=[pl.BlockSpec((tm, d), lambda i: (i, 0)), pl.BlockSpec((1, d), lambda i: (0, 0))],
        out_specs=[pl.BlockSpec((tm, d), lambda i: (jnp.minimum(i, npt - 1), 0)),
                   pl.BlockSpec((tm, d), lambda i: (jnp.maximum(i - npt, 0), 0))],
        out_shape=[jax.ShapeDtypeStruct((n_prompt_rows, d), F32),
                   jax.ShapeDtypeStruct((m - n_prompt_rows, d), F32)],
        compiler_params=_cparams(1),
        name="final_norm",
    )(x, g.reshape(1, d))


def _attn_head(qh, kh, vh):
    s = lax.dot_general(qh, kh, NT_DIMS, preferred_element_type=F32) * qh.shape[-1] ** -0.5
    p = jnp.exp(s - jnp.max(s, axis=-1, keepdims=True))
    p = p / jnp.sum(p, axis=-1, keepdims=True)
    return jnp.dot(p.astype(BF16), vh, preferred_element_type=F32)


def _attn_prompt_body(q_ref, k_ref, v_ref, o_ref, *, heads):
    hd = q_ref.shape[1] // heads
    outs = []
    for h in range(heads):
        sl = slice(h * hd, (h + 1) * hd)
        outs.append(_attn_head(q_ref[:, sl], k_ref[:, sl].astype(BF16), v_ref[:, sl].astype(BF16)))
    o_ref[...] = jnp.concatenate(outs, axis=-1).astype(o_ref.dtype)


def _attn_sample_body(q_ref, k_ref, v_ref, o_ref, *, heads):
    hd = q_ref.shape[1] // heads
    outs = []
    for h in range(heads):
        outs.append(_attn_head(q_ref[:, h * hd:(h + 1) * hd], k_ref[:, h, :].astype(BF16),
                               v_ref[:, h, :].astype(BF16)))
    o_ref[...] = jnp.concatenate(outs, axis=-1).astype(o_ref.dtype)


def mem_attention(q, k_prompt, v_prompt, cache_k, cache_v, layer, *, n_prompt_rows, tm):
    m, d = q.shape
    nm = k_prompt.shape[0]
    ns, heads, hd = cache_k.shape[1], cache_k.shape[3], cache_k.shape[4]
    npc = n_prompt_rows // CHUNK
    att = pl.pallas_call(
        functools.partial(_attn_prompt_body, heads=heads),
        grid=(n_prompt_rows // tm,),
        in_specs=[pl.BlockSpec((tm, d), lambda i: (i, 0)),
                  pl.BlockSpec((nm, d), lambda i: (0, 0)),
                  pl.BlockSpec((nm, d), lambda i: (0, 0))],
        out_specs=pl.BlockSpec((tm, d), lambda i: (i, 0)),
        out_shape=jax.ShapeDtypeStruct((n_prompt_rows, d), BF16),
        compiler_params=_cparams(1),
        name="mem_attention_prompt",
    )(q, k_prompt, v_prompt)
    cache_spec = pl.BlockSpec((None, None, nm, heads, hd), lambda s: (layer, s, 0, 0, 0))
    att_s = pl.pallas_call(
        functools.partial(_attn_sample_body, heads=heads),
        grid=(ns,),
        in_specs=[pl.BlockSpec((CHUNK, d), lambda s: (npc + s, 0)), cache_spec, cache_spec],
        out_specs=pl.BlockSpec((CHUNK, d), lambda s: (s, 0)),
        out_shape=jax.ShapeDtypeStruct((m - n_prompt_rows, d), BF16),
        compiler_params=_cparams(1),
        name="mem_attention_sample",
    )(q, cache_k, cache_v)
    return att, att_s


def _ffn_up_body(x_ref, g_ref, wg_ref, wv_ref, cw_ref, cb_ref, prev_ref, act_ref, stp_ref, sts_ref,
                 xn_ref, *, n_prompt_tiles, nseq, col_splits):
    i = pl.program_id(0)
    j = pl.program_id(1)
    tn = act_ref.shape[1]
    wm1 = stp_ref.shape[2]
    sw = tn // col_splits

    @pl.when((i == 0) & (j == 0))
    def _():
        stp_ref[...] = jnp.zeros(stp_ref.shape, F32)

    @pl.when(j == 0)
    def _():
        xn_ref[...] = _rmsnorm_rows(x_ref[...], g_ref[...]).astype(BF16)

    xn = xn_ref[...]
    sample = jnp.where(i >= n_prompt_tiles, 1, 0)
    chunk = lax.broadcasted_iota(jnp.int32, (nseq, 1, sw), 0)
    from_state = chunk * 0 + sample == 1
    fresh = chunk * (1 - sample) == 0
    hist = []
    for k in range(col_splits):
        cs = slice(k * sw, (k + 1) * sw)
        gate = jnp.dot(xn, wg_ref[0, :, cs], preferred_element_type=F32)
        val = jnp.dot(xn, wv_ref[0, :, cs], preferred_element_type=F32)
        prev = jnp.where(from_state, prev_ref[0, :, :, cs], stp_ref[j, :, :, cs])
        gc = _causal_conv(gate, prev, cw_ref[:, cs], cb_ref[:, cs], nseq, CHUNK, fresh=fresh)
        act_ref[:, cs] = (_silu(gc) * val).astype(act_ref.dtype)
        hist.append(gate.reshape(nseq, CHUNK, sw)[:, CHUNK - wm1:, :])
    hist = jnp.concatenate(hist, axis=-1)

    @pl.when(i < n_prompt_tiles)
    def _():
        stp_ref[j] = hist[nseq - 1:]

    @pl.when(i >= n_prompt_tiles)
    def _():
        sts_ref[j, pl.ds((i - n_prompt_tiles) * nseq, nseq)] = hist


def conv_ffn_up(x, g, w_up, layer, conv_w, conv_b, prev_s, *, n_prompt_rows, tm, tn):
    m, d = x.shape
    ff = w_up.shape[2] // 2
    nj = -(-ff // tn)
    ns, wm1 = prev_s.shape[1], prev_s.shape[2]
    npt = n_prompt_rows // tm
    nseq = tm // CHUNK
    col = lambda j, base=0: pl.multiple_of(base + jnp.minimum(j * tn, ff - tn), LANES)
    assert ff % LANES == 0 and tn % LANES == 0
    el = pl.Element
    return pl.pallas_call(
        functools.partial(_ffn_up_body, n_prompt_tiles=npt, nseq=nseq, col_splits=2),
        grid=(m // tm, nj),
        in_specs=[pl.BlockSpec((tm, d), lambda i, j: (i, 0)),
                  pl.BlockSpec((1, d), lambda i, j: (0, 0)),
                  pl.BlockSpec((el(1), el(d), el(tn)), lambda i, j: (layer, 0, col(j))),
                  pl.BlockSpec((el(1), el(d), el(tn)), lambda i, j: (layer, 0, col(j, ff))),
                  pl.BlockSpec((el(wm1 + 1), el(tn)), lambda i, j: (0, col(j))),
                  pl.BlockSpec((el(1), el(tn)), lambda i, j: (0, col(j))),
                  pl.BlockSpec((1, nseq, wm1, tn), lambda i, j: (j, jnp.maximum(i - npt, 0), 0, 0))],
        out_specs=[pl.BlockSpec((tm, tn), lambda i, j: (i, j)),
                   pl.BlockSpec((nj, 1, wm1, tn), lambda i, j: (0, 0, 0, 0)),
                   pl.BlockSpec((nj, ns, wm1, tn), lambda i, j: (0, 0, 0, 0))],
        out_shape=[jax.ShapeDtypeStruct((m, nj * tn), BF16),
                   jax.ShapeDtypeStruct((nj, 1, wm1, tn), F32),
                   jax.ShapeDtypeStruct((nj, ns, wm1, tn), F32)],
        scratch_shapes=[pltpu.VMEM((tm, d), BF16)],
        compiler_params=_cparams(2),
        name="ffn_up",
    )(x, g.reshape(1, d), w_up, w_up, conv_w, conv_b.reshape(1, ff), prev_s)


def _rglru_body(x_ref, gate_ref, cw_ref, cb_ref, wa_ref, ba_ref, wx_ref, bx_ref, lam_ref,
                cv0_ref, h0_ref, o_ref, cvp_ref, hp_ref, cvs_ref, hs_ref, *, n_prompt_tiles, nseq, heads):
    i = pl.program_id(0)
    tm, ch = x_ref.shape
    bd = ch // heads
    x = x_ref[...].astype(F32)

    def run(prev, h0, n_seq, seqlen, first_row):
        xc = _causal_conv(x, prev, cw_ref[...], cb_ref[...], n_seq, seqlen)
        ra, rx = [], []
        for h in range(heads):
            xh = xc[:, h * bd:(h + 1) * bd].astype(BF16)
            ra.append(jnp.dot(xh, wa_ref[h], preferred_element_type=F32))
            rx.append(jnp.dot(xh, wx_ref[h], preferred_element_type=F32))
        r = _sigmoid(jnp.concatenate(ra, axis=-1) + ba_ref[...])
        ig = _sigmoid(jnp.concatenate(rx, axis=-1) + bx_ref[...])
        log_a = RG_C * r * _log_sigmoid(lam_ref[...])
        a = jnp.exp(log_a)
        m2 = -jnp.tanh(log_a) * (a * a + 1.0)
        mult = jnp.where(m2 > 0.0, m2 * lax.rsqrt(m2), 0.0)
        row = lax.broadcasted_iota(jnp.int32, (n_seq, seqlen, ch), 1)
        if first_row is not None:
            mult = jnp.where(row.reshape(tm, ch) + first_row == 0, 1.0, mult)
        u = xc * ig * mult
        ng = tm // SUBLANES
        ag = a.reshape(ng, SUBLANES, ch)
        ug = u.reshape(ng, SUBLANES, ch)
        grow = lax.broadcasted_iota(jnp.int32, (ng, SUBLANES, ch), 1)
        d = 1
        while d < SUBLANES:
            live = grow >= d
            a_sh = jnp.where(live, pltpu.roll(ag, d, axis=1), 1.0)
            u_sh = jnp.where(live, pltpu.roll(ug, d, axis=1), 0.0)
            ug = ug + ag * u_sh
            ag = ag * a_sh
            d *= 2
        groups_per_seq = seqlen // SUBLANES
        hs, last = [], []
        for gi in range(ng):
            if gi % groups_per_seq == 0:
                carry = h0[gi // groups_per_seq]
            hg = ug[gi] + ag[gi] * carry
            carry = hg[SUBLANES - 1:]
            hs.append(hg)
            if (gi + 1) % groups_per_seq == 0:
                last.append(carry)
        h_all = jnp.concatenate(hs, axis=0)
        o_ref[...] = (h_all * _gelu_tanh(gate_ref[...].astype(F32))).astype(o_ref.dtype)
        x3 = x.reshape(n_seq, seqlen, ch)
        return x3[:, seqlen - prev.shape[1]:, :], jnp.stack(last, axis=0)

    @pl.when(i == 0)
    def _():
        cvp_ref[...] = jnp.zeros(cvp_ref.shape, F32)
        hp_ref[...] = jnp.zeros(hp_ref.shape, F32)

    @pl.when(i < n_prompt_tiles)
    def _():
        cv, hl = run(cvp_ref[...], hp_ref[...], 1, tm, i * tm)
        cvp_ref[...] = cv
        hp_ref[...] = hl

    @pl.when(i >= n_prompt_tiles)
    def _():
        cv, hl = run(cv0_ref[...], h0_ref[...], nseq, CHUNK, None)
        s0 = (i - n_prompt_tiles) * nseq
        cvs_ref[pl.ds(s0, nseq)] = cv
        hs_ref[pl.ds(s0, nseq)] = hl


def rglru(proj, x_col, gate_col, conv_w, conv_b, w_a, b_a, w_x, b_x, lam, cv0, h0, *, n_prompt_rows, tm):
    m = proj.shape[0]
    heads, bd, _ = w_a.shape
    ch = heads * bd
    ns, wm1 = cv0.shape[0], cv0.shape[1]
    npt = n_prompt_rows // tm
    nseq = tm // CHUNK
    samp = lambda i: (jnp.maximum(i - npt, 0), 0, 0)
    const2 = lambda i: (0, 0)
    const3 = lambda i: (0, 0, 0)
    return pl.pallas_call(
        functools.partial(_rglru_body, n_prompt_tiles=npt, nseq=nseq, heads=heads),
        grid=(m // tm,),
        in_specs=[pl.BlockSpec((tm, ch), lambda i: (i, x_col)),
                  pl.BlockSpec((tm, ch), lambda i: (i, gate_col)),
                  pl.BlockSpec((wm1 + 1, ch), const2),
                  pl.BlockSpec((1, ch), const2),
                  pl.BlockSpec((heads, bd, bd), const3),
                  pl.BlockSpec((1, ch), const2),
                  pl.BlockSpec((heads, bd, bd), const3),
                  pl.BlockSpec((1, ch), const2),
                  pl.BlockSpec((1, ch), const2),
                  pl.BlockSpec((nseq, wm1, ch), samp),
                  pl.BlockSpec((nseq, 1, ch), samp)],
        out_specs=[pl.BlockSpec((tm, ch), lambda i: (i, 0)),
                   pl.BlockSpec((1, wm1, ch), const3),
                   pl.BlockSpec((1, 1, ch), const3),
                   pl.BlockSpec((ns, wm1, ch), const3),
                   pl.BlockSpec((ns, 1, ch), const3)],
        out_shape=[jax.ShapeDtypeStruct((m, ch), BF16),
                   jax.ShapeDtypeStruct((1, wm1, ch), F32),
                   jax.ShapeDtypeStruct((1, 1, ch), F32),
                   jax.ShapeDtypeStruct((ns, wm1, ch), F32),
                   jax.ShapeDtypeStruct((ns, 1, ch), F32)],
        compiler_params=_cparams(1),
        name="rglru",
    )(proj, proj, conv_w, conv_b.reshape(1, ch), w_a.astype(BF16), b_a.reshape(1, ch),
      w_x.astype(BF16), b_x.reshape(1, ch), lam.reshape(1, ch), cv0, h0)


def _gated_chunk(q, k, g, v, st_ref, kb_ref, heads):
    c = q.shape[0]
    dk = q.shape[1] // heads
    dv = v.shape[1] // heads
    nsub = c // SUB
    b = _dot01_left(_tril01(c), g) * LOG2_E
    b_end = b[c - 1:c, :]
    qb = (q * jnp.exp2(b)).astype(BF16)
    k_end = (k * jnp.exp2(b_end - b)).astype(BF16)
    dec_end = jnp.exp2(b_end)
    vb = v.astype(BF16)
    kb_ref[0] = k
    kb_ref[1] = b
    lane = lax.broadcasted_iota(jnp.int32, (SUB, c), 1)
    srow = lax.broadcasted_iota(jnp.int32, (SUB, c), 0)
    outs = []
    for h in range(heads):
        sk = slice(h * dk, (h + 1) * dk)
        sv = slice(h * dv, (h + 1) * dv)
        st = st_ref[0, h]
        o = lax.dot_general(qb[:, sk], st.astype(BF16), NT_DIMS, preferred_element_type=F32)
        qh, kh, bh, vh = q[:, sk], k[:, sk], b[:, sk], vb[:, sv]
        blocks = []
        for i in range(nsub):
            r0 = i * SUB
            qi, ki, bi = qh[r0:r0 + SUB], kh[r0:r0 + SUB], bh[r0:r0 + SUB]
            a = jnp.zeros((SUB, c), F32)
            for s in range(SUB):
                e = jnp.exp2(bi - kb_ref[1, r0 + s:r0 + s + 1, sk])
                col = jnp.sum(qi * (kb_ref[0, r0 + s:r0 + s + 1, sk] * e), axis=-1, keepdims=True)
                a = jnp.where(lane == r0 + s, col, a)
            a = jnp.where(srow >= lane - r0, a, 0.0)
            if i > 0:
                ri = bh[r0:r0 + 1]
                qt = (qi * jnp.exp2(bi - ri)).astype(BF16)
                kt = (kh * jnp.exp2(ri - bh)).astype(BF16)
                a = jnp.where(lane < r0, lax.dot_general(qt, kt, NT_DIMS, preferred_element_type=F32), a)
            blocks.append(a)
        a_full = jnp.concatenate(blocks, axis=0).astype(BF16)
        o = o + jnp.dot(a_full, vh, preferred_element_type=F32)
        st_ref[0, h] = st * dec_end[:, sk] + lax.dot_general(
            vh, k_end[:, sk], TN_DIMS, preferred_element_type=F32)
        outs.append(o)
    return outs


def _head_norm_gate(outs, ng_ref, gate_ref):
    res = []
    for h, o in enumerate(outs):
        dv = o.shape[-1]
        sv = slice(h * dv, (h + 1) * dv)
        res.append(_rmsnorm_rows(o, ng_ref[:, sv]) * _silu(gate_ref[:, sv].astype(F32)))
    return jnp.concatenate(res, axis=-1)


def _hgrn_body(q_ref, f_ref, v_ref, gate_ref, lb_ref, ng_ref, s0_ref, o_ref, st_ref, kb_ref,
               *, n_prompt_chunks, heads):
    c = pl.program_id(0)

    @pl.when((c == 0) | (c >= n_prompt_chunks))
    def _():
        st_ref[...] = s0_ref[...]

    lb = lb_ref[...]
    k = (1.0 - lb) / (1.0 + jnp.exp(f_ref[...].astype(F32)))
    g = jnp.log1p(-k)
    outs = _gated_chunk(_silu(q_ref[...].astype(F32)), k, g, v_ref[...], st_ref, kb_ref, heads)
    o_ref[...] = _head_norm_gate(outs, ng_ref, gate_ref).astype(o_ref.dtype)


def _gla_body(q_ref, k_ref, v_ref, gate_ref, gl_ref, gkw_ref, gkb_ref, ng_ref, s0_ref, o_ref, st_ref, kb_ref,
              *, n_prompt_chunks, heads):
    c = pl.program_id(0)

    @pl.when((c == 0) | (c >= n_prompt_chunks))
    def _():
        st_ref[...] = s0_ref[...]

    dk = q_ref.shape[1] // heads
    z = jnp.dot(gl_ref[...].astype(BF16), gkw_ref[...], preferred_element_type=F32) + gkb_ref[...]
    g = _log_sigmoid(z) * (1.0 / D_GATE_NORM)
    outs = _gated_chunk(q_ref[...].astype(F32) * dk ** -0.5, k_ref[...].astype(F32), g, v_ref[...], st_ref, kb_ref, heads)
    o_ref[...] = _head_norm_gate(outs, ng_ref, gate_ref).astype(o_ref.dtype)


def _chunk_state_map(n_prompt_chunks, ndim):
    return lambda c: (jnp.maximum(c - (n_prompt_chunks - 1), 0),) + (0,) * (ndim - 1)


def hgrn2(proj, lb, norm_g, s0, *, n_prompt_chunks):
    m = proj.shape[0]
    _, heads, dv, dk = s0.shape
    kw, vw = heads * dk, heads * dv
    smap = _chunk_state_map(n_prompt_chunks, 4)
    row = lambda c: (0, 0)
    return pl.pallas_call(
        functools.partial(_hgrn_body, n_prompt_chunks=n_prompt_chunks, heads=heads),
        grid=(m // CHUNK,),
        in_specs=[pl.BlockSpec((CHUNK, kw), lambda c: (c, 0)),
                  pl.BlockSpec((CHUNK, kw), lambda c: (c, 1)),
                  pl.BlockSpec((CHUNK, vw), lambda c: (c, 2 * kw // vw)),
                  pl.BlockSpec((CHUNK, vw), lambda c: (c, 2 * kw // vw + 1)),
                  pl.BlockSpec((1, kw), row),
                  pl.BlockSpec((1, vw), row),
                  pl.BlockSpec((1, heads, dv, dk), smap)],
        out_specs=[pl.BlockSpec((CHUNK, vw), lambda c: (c, 0)),
                   pl.BlockSpec((1, heads, dv, dk), smap)],
        out_shape=[jax.ShapeDtypeStruct((m, vw), BF16),
                   jax.ShapeDtypeStruct(s0.shape, F32)],
        scratch_shapes=[pltpu.VMEM((2, CHUNK, kw), F32)],
        compiler_params=_cparams(1),
        name="hgrn2",
    )(proj, proj, proj, proj, lb.reshape(1, kw), norm_g.reshape(1, vw), s0)


def gla(proj, cols, gk_w, gk_b, norm_g, s0, *, n_prompt_chunks):
    m = proj.shape[0]
    _, heads, dv, dk = s0.shape
    kw, vw = heads * dk, heads * dv
    cq, ck, cv, cg, cgl = cols
    smap = _chunk_state_map(n_prompt_chunks, 4)
    row = lambda c: (0, 0)
    return pl.pallas_call(
        functools.partial(_gla_body, n_prompt_chunks=n_prompt_chunks, heads=heads),
        grid=(m // CHUNK,),
        in_specs=[pl.BlockSpec((CHUNK, kw), lambda c: (c, cq)),
                  pl.BlockSpec((CHUNK, kw), lambda c: (c, ck)),
                  pl.BlockSpec((CHUNK, vw), lambda c: (c, cv)),
                  pl.BlockSpec((CHUNK, vw), lambda c: (c, cg)),
                  pl.BlockSpec((CHUNK, LANES), lambda c: (c, cgl)),
                  pl.BlockSpec((LANES, kw), row),
                  pl.BlockSpec((1, kw), row),
                  pl.BlockSpec((1, vw), row),
                  pl.BlockSpec((1, heads, dv, dk), smap)],
        out_specs=[pl.BlockSpec((CHUNK, vw), lambda c: (c, 0)),
                   pl.BlockSpec((1, heads, dv, dk), smap)],
        out_shape=[jax.ShapeDtypeStruct((m, vw), BF16),
                   jax.ShapeDtypeStruct(s0.shape, F32)],
        scratch_shapes=[pltpu.VMEM((2, CHUNK, kw), F32)],
        compiler_params=_cparams(1),
        name="gla",
    )(proj, proj, proj, proj, proj, gk_w, gk_b.reshape(1, kw), norm_g.reshape(1, vw), s0)


def _ssd_body(z_ref, x_ref, bc_ref, dt_ref, cw_ref, cb_ref, dtb_ref, alog_ref, dsk_ref, ng_ref,
              cv0_ref, s0_ref, o_ref, cv_ref, st_ref, *, n_prompt_chunks, heads, groups):
    c = pl.program_id(0)

    @pl.when((c == 0) | (c >= n_prompt_chunks))
    def _():
        st_ref[...] = s0_ref[...]
        cv_ref[...] = cv0_ref[...]

    rows, xw_ = x_ref.shape
    hp = xw_ // heads
    n = bc_ref.shape[1] // (2 * groups)
    gw = xw_ // groups
    x_raw, bc_raw = x_ref[...].astype(F32), bc_ref[...].astype(F32)
    prev = cv_ref[...]
    cw, cb = cw_ref[...], cb_ref[...]
    xs = _silu(_causal_conv(x_raw, prev[:, :, :xw_], cw[:, :xw_], cb[:, :xw_], 1, rows))
    bcs = _silu(_causal_conv(bc_raw, prev[:, :, xw_:], cw[:, xw_:], cb[:, xw_:], 1, rows))
    wm1 = prev.shape[1]
    cv_ref[0, :, :xw_] = x_raw[rows - wm1:, :]
    cv_ref[0, :, xw_:] = bc_raw[rows - wm1:, :]

    erow = lax.broadcasted_iota(jnp.int32, (LANES, xw_), 0)
    ecol = lax.broadcasted_iota(jnp.int32, (LANES, xw_), 1)
    expand = jnp.where(ecol // hp == erow, 1.0, 0.0).astype(BF16)
    dt = _softplus(_dot01_right(dt_ref[...].astype(F32), expand) + dtb_ref[...])
    dta = dt * (-jnp.exp(alog_ref[...]))
    cum = _dot01_left(_tril01(rows), dta)
    cum_end = cum[rows - 1:rows, :]
    xdt = xs * dt
    xdt_b = xdt.astype(BF16)
    x_end = (xdt * jnp.exp(cum_end - cum)).astype(BF16)
    e_cum = jnp.exp(cum)
    dec_end = jnp.exp(cum_end)

    prow = lax.broadcasted_iota(jnp.int32, (rows, LANES), 0)
    plane = lax.broadcasted_iota(jnp.int32, (rows, LANES), 1)
    assert hp == rows and LANES % hp == 0
    pair = LANES // hp
    psrc = plane % hp
    y_parts = []
    for g in range(groups):
        bg = bcs[:, g * n:(g + 1) * n].astype(BF16)
        cg = bcs[:, (groups + g) * n:(groups + g + 1) * n].astype(BF16)
        gl = slice(g * gw, (g + 1) * gw)
        st = st_ref[0, :, gl]
        y_inter = jnp.dot(cg, st.astype(BF16), preferred_element_type=F32) * e_cum[:, gl]
        cb_rep = lax.dot_general(cg, jnp.concatenate([bg] * pair, axis=0), NT_DIMS, preferred_element_type=F32)
        y_intra = []
        for sl in range(gw // LANES):
            lo = g * gw + sl * LANES
            cs = cum[:, lo:lo + LANES]
            cdiag = jnp.sum(jnp.where(prow == psrc, cs, 0.0), axis=0, keepdims=True)
            decay = jnp.exp(jnp.minimum(cs - cdiag, 0.0))
            mt = jnp.where(prow >= psrc, cb_rep * decay, 0.0).astype(BF16)
            xp = xdt_b[:, lo:lo + LANES]
            rhs = jnp.concatenate(
                [jnp.where(plane // hp == p, xp, jnp.zeros_like(xp)) for p in range(pair)], axis=0)
            y_intra.append(jnp.dot(mt, rhs, preferred_element_type=F32))
        y_parts.append(jnp.concatenate(y_intra, axis=-1) + y_inter)
        st_ref[0, :, gl] = st * dec_end[:, gl] + lax.dot_general(
            bg, x_end[:, gl], TN_DIMS, preferred_element_type=F32)
    y = jnp.concatenate(y_parts, axis=-1) + dsk_ref[...] * xs
    yz = y * _silu(z_ref[...].astype(F32))
    res = []
    for g in range(groups):
        gl = slice(g * gw, (g + 1) * gw)
        res.append(_rmsnorm_rows(yz[:, gl], ng_ref[:, gl]))
    o_ref[...] = jnp.concatenate(res, axis=-1).astype(o_ref.dtype)


def ssd(proj, cols, conv_w, conv_b, dt_bias_x, a_log_x, d_skip_x, norm_g, cv0, s0, *, n_prompt_chunks):
    m = proj.shape[0]
    xw_ = s0.shape[2]
    n = s0.shape[1]
    bcw = 2 * C_GROUPS * n
    cz, cx, cbc, cdt = cols
    wm1 = cv0.shape[1]
    smap3 = _chunk_state_map(n_prompt_chunks, 3)
    row = lambda c: (0, 0)
    return pl.pallas_call(
        functools.partial(_ssd_body, n_prompt_chunks=n_prompt_chunks, heads=C_HEADS, groups=C_GROUPS),
        grid=(m // CHUNK,),
        in_specs=[pl.BlockSpec((CHUNK, xw_), lambda c: (c, cz)),
                  pl.BlockSpec((CHUNK, xw_), lambda c: (c, cx)),
                  pl.BlockSpec((CHUNK, bcw), lambda c: (c, cbc)),
                  pl.BlockSpec((CHUNK, LANES), lambda c: (c, cdt)),
                  pl.BlockSpec((wm1 + 1, xw_ + bcw), row),
                  pl.BlockSpec((1, xw_ + bcw), row),
                  pl.BlockSpec((1, xw_), row),
                  pl.BlockSpec((1, xw_), row),
                  pl.BlockSpec((1, xw_), row),
                  pl.BlockSpec((1, xw_), row),
                  pl.BlockSpec((1, wm1, xw_ + bcw), smap3),
                  pl.BlockSpec((1, n, xw_), smap3)],
        out_specs=[pl.BlockSpec((CHUNK, xw_), lambda c: (c, 0)),
                   pl.BlockSpec((1, wm1, xw_ + bcw), smap3),
                   pl.BlockSpec((1, n, xw_), smap3)],
        out_shape=[jax.ShapeDtypeStruct((m, xw_), BF16),
                   jax.ShapeDtypeStruct(cv0.shape, F32),
                   jax.ShapeDtypeStruct(s0.shape, F32)],
        compiler_params=_cparams(1),
        name="ssd",
    )(proj, proj, proj, proj, conv_w, conv_b.reshape(1, xw_ + bcw), dt_bias_x.reshape(1, xw_),
      a_log_x.reshape(1, xw_), d_skip_x.reshape(1, xw_), norm_g.reshape(1, xw_), cv0, s0)


def _with_zero_first(s):
    return jnp.concatenate([jnp.zeros((1,) + s.shape[1:], s.dtype), s], axis=0)


def _col_tile_starts(width, tn):
    return [min(j * tn, width - tn) for j in range(-(-width // tn))]


def _block_cols(a, width, tn):
    return jnp.stack([a[..., o:o + tn] for o in _col_tile_starts(width, tn)], axis=0)


def _unblock_cols(b, width, tn):
    return jnp.concatenate([b[j][..., j * tn - o:] for j, o in enumerate(_col_tile_starts(width, tn))], axis=-1)


def kernel(x_prompt, x_sample, mem_prompt, cache_mem_k, cache_mem_v, state_hgrn, state_rglru, state_rg_conv, state_ssd, state_ssd_conv, state_gla, state_ffn_conv, norm_mix, norm_xattn, norm_mem, norm_ffn, ev_w_in, hgrn_lb_logits, hgrn_norm, rg_conv_w, rg_conv_b, rg_w_a, rg_b_a, rg_w_x, rg_b_x, rg_lambda, ev_w_out, od_w_in, ssd_conv_w, ssd_conv_b, ssd_dt_bias, ssd_A_log, ssd_D, ssd_norm, gla_gk_w, gla_gk_b, gla_norm, od_w_out, xa_wq, xa_wk, xa_wv, xa_wo, ffn_up, ffn_conv_w, ffn_conv_b, ffn_down, final_norm):
    bp, p_rows, d = x_prompt.shape
    ns, s_len, _ = x_sample.shape
    assert bp == 1 and s_len == CHUNK and p_rows % CHUNK == 0
    depth = norm_mix.shape[0]
    s_rows = ns * s_len
    m = p_rows + s_rows
    npc = p_rows // CHUNK
    nm = mem_prompt.shape[1]
    tm = math.gcd(math.gcd(p_rows, s_rows), 1024)
    tm_seq = math.gcd(tm, 256)

    x = (x_prompt.reshape(p_rows, d), x_sample.reshape(s_rows, d))
    lb_all = jnp.cumsum(jax.nn.softmax(hgrn_lb_logits.astype(F32), axis=0), axis=0)
    ev_w_in_b, ev_w_out_b, od_w_out_b = ev_w_in.astype(BF16), ev_w_out.astype(BF16), od_w_out.astype(BF16)
    wq_b, wk_b, wv_b, wo_b = xa_wq.astype(BF16), xa_wk.astype(BF16), xa_wv.astype(BF16), xa_wo.astype(BF16)
    w_up_b, w_down_b = ffn_up.astype(BF16), ffn_down.astype(BF16)

    a_kw = hgrn_lb_logits.shape[1]
    a_vw = hgrn_norm.shape[1]
    b_w = rg_lambda.shape[1]
    c_w = ssd_norm.shape[1]
    c_bc = 2 * C_GROUPS * C_STATE
    c_hd = c_w // C_HEADS
    d_val = gla_norm.shape[1]
    d_key = gla_gk_b.shape[1]
    d_rank = gla_gk_w.shape[1]
    ff = ffn_conv_b.shape[1]
    ffn_tn = 512
    wide_tn = lambda rows: 512 if isinstance(rows, tuple) else 1024

    mem_k, mem_v = [], []
    hg, rl, rc, ss, sc, gs, fc_p, fc_s = [], [], [], [], [], [], [], []
    for l in range(depth):
        j = l // 2
        if l % 2 == 0:
            proj = norm_matmul(x, norm_mix[l], ev_w_in_b, j, tm=tm, tn=wide_tn(x), out_dtype=PROJ_DTYPE)
            s0 = _with_zero_first(jnp.swapaxes(state_hgrn[j], -1, -2))
            o_a, st = hgrn2(proj, lb_all[l], hgrn_norm[j], s0, n_prompt_chunks=npc)
            hg.append(jnp.swapaxes(st, -1, -2))
            xcol = (2 * a_kw + 2 * a_vw) // b_w
            o_b, cvp, hp_, cvs, hs_ = rglru(
                proj, xcol, xcol + 1, rg_conv_w[j], rg_conv_b[j], rg_w_a[j], rg_b_a[j], rg_w_x[j], rg_b_x[j],
                rg_lambda[j], state_rg_conv[j], state_rglru[j].reshape(ns, 1, b_w), n_prompt_rows=p_rows, tm=tm_seq)
            rc.append((cvp, cvs))
            rl.append((hp_.reshape(1, b_w), hs_.reshape(ns, b_w)))
            x = matmul_res([(o_a, 0, a_vw, 0), (o_b, 0, b_w, a_vw)], ev_w_out_b, j, x, tm=tm, tn=wide_tn(x))
        else:
            w = od_w_in[j]
            offs = np_cumsum_offsets((c_w, c_w + c_bc, C_HEADS, d_key, d_key, d_val, d_val, d_rank))
            o_z, o_xbc, o_dt, o_q, o_k, o_v, o_g, o_gl = offs
            pad = lambda a: jnp.pad(a, ((0, 0), (0, LANES - a.shape[1])))
            w_re = jnp.concatenate([
                w[:, o_z:o_z + c_w], w[:, o_xbc:o_xbc + c_w], w[:, o_xbc + c_w:o_xbc + c_w + c_bc],
                w[:, o_q:o_q + d_key], w[:, o_k:o_k + d_key],
                pad(w[:, o_dt:o_dt + C_HEADS]), pad(w[:, o_gl:o_gl + d_rank]),
                jnp.zeros((d, 2 * LANES), w.dtype),
                w[:, o_v:o_v + d_val], w[:, o_g:o_g + d_val]], axis=1).astype(BF16)
            proj = norm_matmul(x, norm_mix[l], w_re, tm=tm, tn=wide_tn(x), out_dtype=PROJ_DTYPE)
            assert c_w == d_val
            base = 2 * c_w + c_bc
            col_dt = (base + 2 * d_key) // LANES
            cv0 = _with_zero_first(state_ssd_conv[j])
            s0c = _with_zero_first(jnp.transpose(state_ssd[j], (0, 3, 1, 2)).reshape(ns, C_STATE, c_w))
            rep = lambda a: jnp.repeat(a, c_hd)
            o_c, cv, stc = ssd(proj, (0, 1, 2 * c_w // c_bc, col_dt), ssd_conv_w[j], ssd_conv_b[j],
                               rep(ssd_dt_bias[j]), rep(ssd_A_log[j]), rep(ssd_D[j]), ssd_norm[j], cv0, s0c,
                               n_prompt_chunks=npc)
            sc.append(cv)
            ss.append(jnp.transpose(stc.reshape(1 + ns, C_STATE, C_HEADS, c_hd), (0, 2, 3, 1)))
            s0d = _with_zero_first(jnp.swapaxes(state_gla[j], -1, -2))
            gkw = jnp.pad(gla_gk_w[j], ((0, LANES - d_rank), (0, 0))).astype(BF16)
            v_start = base + 2 * d_key + 4 * LANES
            o_d, std = gla(proj, (base // d_key, base // d_key + 1, v_start // d_val, v_start // d_val + 1, col_dt + 1),
                           gkw, gla_gk_b[j], gla_norm[j], s0d, n_prompt_chunks=npc)
            gs.append(jnp.swapaxes(std, -1, -2))
            x = matmul_res([(o_c, 0, c_w, 0), (o_d, 0, d_val, c_w)], od_w_out_b, j, x, tm=tm, tn=wide_tn(x))

        mem = mem_prompt.reshape(nm, d)
        k_p = norm_matmul(mem, norm_mem[l], wk_b, l, tm=nm, tn=512, out_dtype=F32)
        v_p = norm_matmul(mem, norm_mem[l], wv_b, l, tm=nm, tn=512, out_dtype=F32)
        mem_k.append(k_p.reshape(1, nm, MEM_HEADS, d // MEM_HEADS))
        mem_v.append(v_p.reshape(1, nm, MEM_HEADS, d // MEM_HEADS))
        q = norm_matmul(x, norm_xattn[l], wq_b, l, tm=tm, tn=wide_tn(x), out_dtype=BF16)
        att = mem_attention(q, k_p, v_p, cache_mem_k, cache_mem_v, l, n_prompt_rows=p_rows, tm=tm_seq)
        x = matmul_res([(att, 0, d, 0)], wo_b, l, x, tm=tm, tn=wide_tn(x))

        prev_s = _block_cols(state_ffn_conv[l], ff, ffn_tn)
        act, stp, sts = conv_ffn_up(x, norm_ffn[l], w_up_b, l, ffn_conv_w[l], ffn_conv_b[l],
                                    prev_s, n_prompt_rows=p_rows, tm=tm, tn=ffn_tn)
        fc_p.append(_unblock_cols(stp, ff, ffn_tn))
        fc_s.append(_unblock_cols(sts, ff, ffn_tn))
        starts = _col_tile_starts(ff, ffn_tn)
        body = (len(starts) - 1) * ffn_tn
        down_terms = [(act, 0, body, 0), (act, body + (body - starts[-1]), ff - body, body)]
        x = matmul_res(down_terms, w_down_b, l, x, tm=tm, tn=512)

    y_prompt, y_sample = rmsnorm_rows(x, final_norm, n_prompt_rows=p_rows, tm=tm)
    y_prompt = y_prompt.reshape(1, p_rows, d)
    y_sample = y_sample.reshape(ns, s_len, d)
    stack_p = lambda lst: jnp.stack([a[:1] for a in lst])
    stack_s = lambda lst: jnp.stack([a[1:] for a in lst])
    return (y_prompt, y_sample, jnp.stack(mem_k), jnp.stack(mem_v),
            stack_p(hg), jnp.stack([a[0] for a in rl]), jnp.stack([a[0] for a in rc]),
            stack_p(ss), stack_p(sc), stack_p(gs), jnp.stack(fc_p),
            stack_s(hg), jnp.stack([a[1] for a in rl]), jnp.stack([a[1] for a in rc]),
            stack_s(ss), stack_s(sc), stack_s(gs), jnp.stack(fc_s))


def np_cumsum_offsets(sizes):
    offs, acc = [], 0
    for s in sizes:
        offs.append(acc)
        acc += s
    return offs
```

```python
import functools
import math

import jax
import jax.numpy as jnp
from jax import lax
from jax.experimental import pallas as pl
from jax.experimental.pallas import tpu as pltpu

F32 = jnp.float32
BF16 = jnp.bfloat16
PROJ_DTYPE = F32

CHUNK = 64
SUB = 16
EPS = 1e-6
LOG2_E = 1.4426950408889634
LANES = 128
SUBLANES = 8
VMEM_LIMIT = 56 * 1024 * 1024

MEM_HEADS = 4
A_HEADS = 8
B_HEADS = 8
C_HEADS = 16
C_GROUPS = 2
C_STATE = 128
D_HEADS = 4
RG_C = 8.0
D_GATE_NORM = 16.0

NT_DIMS = (((1,), (1,)), ((), ()))
TN_DIMS = (((0,), (0,)), ((), ()))


def _cparams(n_axes):
    return pltpu.CompilerParams(dimension_semantics=("arbitrary",) * n_axes,
                                vmem_limit_bytes=VMEM_LIMIT)


def _sigmoid(x):
    return 1.0 / (1.0 + jnp.exp(-x))


def _silu(x):
    return x * _sigmoid(x)


def _log_sigmoid(x):
    return jnp.minimum(x, 0.0) - jnp.log1p(jnp.exp(-jnp.abs(x)))


def _softplus(x):
    return jnp.maximum(x, 0.0) + jnp.log1p(jnp.exp(-jnp.abs(x)))


def _gelu_tanh(x):
    return 0.5 * x * (1.0 + jnp.tanh(math.sqrt(2.0 / math.pi) * (x + 0.044715 * (x * x * x))))


def _split3(x):
    hi = x.astype(BF16)
    r = x - hi.astype(F32)
    mid = r.astype(BF16)
    lo = (r - mid.astype(F32)).astype(BF16)
    return hi, mid, lo


def _dot01_left(t01, x):
    return sum(jnp.dot(t01, p, preferred_element_type=F32) for p in _split3(x))


def _dot01_right(x, e01):
    return sum(jnp.dot(p, e01, preferred_element_type=F32) for p in _split3(x))


def _tril01(n):
    r = lax.broadcasted_iota(jnp.int32, (n, n), 0)
    c = lax.broadcasted_iota(jnp.int32, (n, n), 1)
    return jnp.where(r >= c, 1.0, 0.0).astype(BF16)


def _causal_conv(x, prev, w, b, nseq, seqlen, fresh=None):
    width = w.shape[0]
    ch = x.shape[-1]
    head = 8
    tap = lambda d: w[width - 1 - d:width - d].reshape(1, 1, ch)
    x3 = x.reshape(nseq, seqlen, ch)
    rolled = [pltpu.roll(x, d, axis=0).reshape(nseq, seqlen, ch) for d in range(1, width)]
    y = b + tap(0) * x3
    for d in range(1, width):
        y = y + tap(d) * rolled[d - 1]
    row = lax.broadcasted_iota(jnp.int32, (nseq, head, ch), 1)
    yh = b + tap(0) * x3[:, :head]
    for d in range(1, width):
        xs = rolled[d - 1][:, :head]
        for r in range(d):
            p = width - 1 - d + r
            xs = jnp.where(row == r, prev[:, p:p + 1, :], xs)
        yh = yh + tap(d) * xs
    if fresh is not None:
        yh = jnp.where(fresh, yh, y[:, :head])
    y = jnp.concatenate([yh, y[:, head:]], axis=1)
    return y.reshape(nseq * seqlen, ch)


def _rmsnorm_rows(x, g):
    ms = jnp.mean(x * x, axis=-1, keepdims=True)
    return x * lax.rsqrt(ms + EPS) * g


def _row_specs(rows, tm, width, col=lambda j: 0):
    blk = (pl.Element(tm), pl.Element(width))
    row = lambda t: pl.multiple_of(t * tm, tm)
    if not isinstance(rows, tuple):
        return [pl.BlockSpec(blk, lambda i, j: (row(i), col(j)))], rows.shape[0], None
    npt = rows[0].shape[0] // tm
    specs = [pl.BlockSpec(blk, lambda i, j: (row(jnp.minimum(i, npt - 1)), col(j))),
             pl.BlockSpec(blk, lambda i, j: (row(jnp.maximum(i - npt, 0)), col(j)))]
    return specs, rows[0].shape[0] + rows[1].shape[0], npt


def _for_row_source(npt, fn, also=True):
    if npt is None and also is True:
        fn(0)
    elif npt is None:
        pl.when(also)(lambda: fn(0))
    else:
        i = pl.program_id(0)
        pl.when(also & (i < npt))(lambda: fn(0))
        pl.when(also & (i >= npt))(lambda: fn(1))


def _norm_matmul_body(*refs, npt):
    n_x = 1 if npt is None else 2
    x_refs = refs[:n_x]
    g_ref, w_ref, o_ref, xn_ref = refs[n_x:]

    def normalize(k):
        xn_ref[...] = _rmsnorm_rows(x_refs[k][...], g_ref[...]).astype(BF16)

    _for_row_source(npt, normalize, also=pl.program_id(1) == 0)
    o_ref[...] = jnp.dot(xn_ref[...], w_ref[...], preferred_element_type=F32).astype(o_ref.dtype)


def _weight_spec(w, layer, k_rows, k_block, tn):
    if layer is None:
        return pl.BlockSpec((k_rows, tn), lambda i, j: (k_block, j))
    return pl.BlockSpec((None, k_rows, tn), lambda i, j: (layer, k_block, j))


def norm_matmul(x, g, w, layer=None, *, tm, tn, out_dtype):
    d, n = w.shape[-2:]
    x_specs, m, npt = _row_specs(x, tm, d)
    xs = x if isinstance(x, tuple) else (x,)
    return pl.pallas_call(
        functools.partial(_norm_matmul_body, npt=npt),
        grid=(m // tm, n // tn),
        in_specs=x_specs + [pl.BlockSpec((1, d), lambda i, j: (0, 0)), _weight_spec(w, layer, d, 0, tn)],
        out_specs=pl.BlockSpec((tm, tn), lambda i, j: (i, j)),
        out_shape=jax.ShapeDtypeStruct((m, n), out_dtype),
        scratch_shapes=[pltpu.VMEM((tm, d), BF16)],
        compiler_params=_cparams(2),
        name="norm_matmul",
    )(*xs, g.reshape(1, d), w)


def _matmul_res_body(*refs, arity, npt):
    pos = 0
    terms = []
    for n_a in arity[:-1]:
        terms.append((refs[pos:pos + n_a], refs[pos + n_a]))
        pos += n_a + 1
    r_refs, o_ref = refs[pos:pos + arity[-1]], refs[-1]

    def compute(k):
        acc = r_refs[min(k, len(r_refs) - 1)][...]
        for a_refs, w_ref in terms:
            acc = acc + jnp.dot(a_refs[min(k, len(a_refs) - 1)][...], w_ref[0], preferred_element_type=F32)
        o_ref[...] = acc

    _for_row_source(npt, compute)


def matmul_res(terms, w, layer, res, *, tm, tn):
    n = w.shape[-1]
    el = pl.Element
    in_specs, operands, arity, npts = [], [], [], set()
    for a, a_col, k, w_row in terms:
        specs, m, npt = _row_specs(a, tm, k, col=lambda j, c=a_col: c)
        in_specs += specs + [pl.BlockSpec((el(1), el(k), el(tn)),
                                          lambda i, j, r=w_row: (layer, r, pl.multiple_of(j * tn, tn)))]
        operands += list(a if isinstance(a, tuple) else (a,)) + [w]
        arity.append(len(specs))
        npts.add(npt)
    r_specs, m, npt = _row_specs(res, tm, tn, col=lambda j: pl.multiple_of(j * tn, tn))
    arity.append(len(r_specs))
    npts = (npts | {npt}) - {None}
    assert len(npts) <= 1
    return pl.pallas_call(
        functools.partial(_matmul_res_body, arity=tuple(arity), npt=npts.pop() if npts else None),
        grid=(m // tm, n // tn),
        in_specs=in_specs + r_specs,
        out_specs=pl.BlockSpec((tm, tn), lambda i, j: (i, j)),
        out_shape=jax.ShapeDtypeStruct((m, n), F32),
        compiler_params=_cparams(2),
        name="matmul_res",
    )(*operands, *(res if isinstance(res, tuple) else (res,)))


def _rmsnorm_body(x_ref, g_ref, op_ref, os_ref, *, npt):
    y = _rmsnorm_rows(x_ref[...], g_ref[...])

    @pl.when(pl.program_id(0) < npt)
    def _():
        op_ref[...] = y

    @pl.when(pl.program_id(0) >= npt)
    def _():
        os_ref[...] = y


def rmsnorm_rows(x, g, *, n_prompt_rows, tm):
    m, d = x.shape
    npt = n_prompt_rows // tm
    return pl.pallas_call(
        functools.partial(_rmsnorm_body, npt=npt),
        grid=(m // tm,),
        in_specs=[pl.BlockSpec((tm, d), lambda i: (i, 0)), pl.BlockSpec((1, d), lambda i: (0, 0))],
        out_specs=[pl.BlockSpec((tm, d), lambda i: (jnp.minimum(i, npt - 1), 0)),
                   pl.BlockSpec((tm, d), lambda i: (jnp.maximum(i - npt, 0), 0))],
        out_shape=[jax.ShapeDtypeStruct((n_prompt_rows, d), F32),
                   jax.ShapeDtypeStruct((m - n_prompt_rows, d), F32)],
        compiler_params=_cparams(1),
        name="final_norm",
    )(x, g.reshape(1, d))


def _attn_head(qh, kh, vh):
    s = lax.dot_general(qh, kh, NT_DIMS, preferred_element_type=F32) * qh.shape[-1] ** -0.5
    p = jnp.exp(s - jnp.max(s, axis=-1, keepdims=True))
    p = p / jnp.sum(p, axis=-1, keepdims=True)
    return jnp.dot(p.astype(BF16), vh, preferred_element_type=F32)


def _attn_prompt_body(q_ref, k_ref, v_ref, o_ref, *, heads):
    hd = q_ref.shape[1] // heads
    outs = []
    for h in range(heads):
        sl = slice(h * hd, (h + 1) * hd)
        outs.append(_attn_head(q_ref[:, sl], k_ref[:, sl].astype(BF16), v_ref[:, sl].astype(BF16)))
    o_ref[...] = jnp.concatenate(outs, axis=-1).astype(o_ref.dtype)


def _attn_sample_body(q_ref, k_ref, v_ref, o_ref, *, heads):
    hd = q_ref.shape[1] // heads
    outs = []
    for h in range(heads):
        outs.append(_attn_head(q_ref[:, h * hd:(h + 1) * hd], k_ref[:, h, :].astype(BF16),
                               v_ref[:, h, :].astype(BF16)))
    o_ref[...] = jnp.concatenate(outs, axis=-1).astype(o_ref.dtype)


def mem_attention(q, k_prompt, v_prompt, cache_k, cache_v, layer, *, n_prompt_rows, tm):
    m, d = q.shape
    nm = k_prompt.shape[0]
    ns, heads, hd = cache_k.shape[1], cache_k.shape[3], cache_k.shape[4]
    npc = n_prompt_rows // CHUNK
    att = pl.pallas_call(
        functools.partial(_attn_prompt_body, heads=heads),
        grid=(n_prompt_rows // tm,),
        in_specs=[pl.BlockSpec((tm, d), lambda i: (i, 0)),
                  pl.BlockSpec((nm, d), lambda i: (0, 0)),
                  pl.BlockSpec((nm, d), lambda i: (0, 0))],
        out_specs=pl.BlockSpec((tm, d), lambda i: (i, 0)),
        out_shape=jax.ShapeDtypeStruct((n_prompt_rows, d), BF16),
        compiler_params=_cparams(1),
        name="mem_attention_prompt",
    )(q, k_prompt, v_prompt)
    cache_spec = pl.BlockSpec((None, None, nm, heads, hd), lambda s: (layer, s, 0, 0, 0))
    att_s = pl.pallas_call(
        functools.partial(_attn_sample_body, heads=heads),
        grid=(ns,),
        in_specs=[pl.BlockSpec((CHUNK, d), lambda s: (npc + s, 0)), cache_spec, cache_spec],
        out_specs=pl.BlockSpec((CHUNK, d), lambda s: (s, 0)),
        out_shape=jax.ShapeDtypeStruct((m - n_prompt_rows, d), BF16),
        compiler_params=_cparams(1),
        name="mem_attention_sample",
    )(q, cache_k, cache_v)
    return att, att_s


def _ffn_up_body(x_ref, g_ref, wg_ref, wv_ref, cw_ref, cb_ref, prev_ref, act_ref, stp_ref, sts_ref,
                 xn_ref, *, n_prompt_tiles, nseq, col_splits):
    i = pl.program_id(0)
    j = pl.program_id(1)
    tn = act_ref.shape[1]
    wm1 = stp_ref.shape[2]
    sw = tn // col_splits

    @pl.when((i == 0) & (j == 0))
    def _():
        stp_ref[...] = jnp.zeros(stp_ref.shape, F32)

    @pl.when(j == 0)
    def _():
        xn_ref[...] = _rmsnorm_rows(x_ref[...], g_ref[...]).astype(BF16)

    xn = xn_ref[...]
    sample = jnp.where(i >= n_prompt_tiles, 1, 0)
    chunk = lax.broadcasted_iota(jnp.int32, (nseq, 1, sw), 0)
    from_state = chunk * 0 + sample == 1
    fresh = chunk * (1 - sample) == 0
    hist = []
    for k in range(col_splits):
        cs = slice(k * sw, (k + 1) * sw)
        gate = jnp.dot(xn, wg_ref[0, :, cs], preferred_element_type=F32)
        val = jnp.dot(xn, wv_ref[0, :, cs], preferred_element_type=F32)
        prev = jnp.where(from_state, prev_ref[0, :, :, cs], stp_ref[j, :, :, cs])
        gc = _causal_conv(gate, prev, cw_ref[:, cs], cb_ref[:, cs], nseq, CHUNK, fresh=fresh)
        act_ref[:, cs] = (_silu(gc) * val).astype(act_ref.dtype)
        hist.append(gate.reshape(nseq, CHUNK, sw)[:, CHUNK - wm1:, :])
    hist = jnp.concatenate(hist, axis=-1)

    @pl.when(i < n_prompt_tiles)
    def _():
        stp_ref[j] = hist[nseq - 1:]

    @pl.when(i >= n_prompt_tiles)
    def _():
        sts_ref[j, pl.ds((i - n_prompt_tiles) * nseq, nseq)] = hist


def conv_ffn_up(x, g, w_up, layer, conv_w, conv_b, prev_s, *, n_prompt_rows, tm, tn):
    m, d = x.shape
    ff = w_up.shape[2] // 2
    nj = -(-ff // tn)
    ns, wm1 = prev_s.shape[1], prev_s.shape[2]
    npt = n_prompt_rows // tm
    nseq = tm // CHUNK
    col = lambda j, base=0: pl.multiple_of(base + jnp.minimum(j * tn, ff - tn), LANES)
    assert ff % LANES == 0 and tn % LANES == 0
    el = pl.Element
    return pl.pallas_call(
        functools.partial(_ffn_up_body, n_prompt_tiles=npt, nseq=nseq, col_splits=2),
        grid=(m // tm, nj),
        in_specs=[pl.BlockSpec((tm, d), lambda i, j: (i, 0)),
                  pl.BlockSpec((1, d), lambda i, j: (0, 0)),
                  pl.BlockSpec((el(1), el(d), el(tn)), lambda i, j: (layer, 0, col(j))),
                  pl.BlockSpec((el(1), el(d), el(tn)), lambda i, j: (layer, 0, col(j, ff))),
                  pl.BlockSpec((el(wm1 + 1), el(tn)), lambda i, j: (0, col(j))),
                  pl.BlockSpec((el(1), el(tn)), lambda i, j: (0, col(j))),
                  pl.BlockSpec((1, nseq, wm1, tn), lambda i, j: (j, jnp.maximum(i - npt, 0), 0, 0))],
        out_specs=[pl.BlockSpec((tm, tn), lambda i, j: (i, j)),
                   pl.BlockSpec((nj, 1, wm1, tn), lambda i, j: (0, 0, 0, 0)),
                   pl.BlockSpec((nj, ns, wm1, tn), lambda i, j: (0, 0, 0, 0))],
        out_shape=[jax.ShapeDtypeStruct((m, nj * tn), BF16),
                   jax.ShapeDtypeStruct((nj, 1, wm1, tn), F32),
                   jax.ShapeDtypeStruct((nj, ns, wm1, tn), F32)],
        scratch_shapes=[pltpu.VMEM((tm, d), BF16)],
        compiler_params=_cparams(2),
        name="ffn_up",
    )(x, g.reshape(1, d), w_up, w_up, conv_w, conv_b.reshape(1, ff), prev_s)


def _rglru_body(x_ref, gate_ref, cw_ref, cb_ref, wa_ref, ba_ref, wx_ref, bx_ref, lam_ref,
                cv0_ref, h0_ref, o_ref, cvp_ref, hp_ref, cvs_ref, hs_ref, *, n_prompt_tiles, nseq, heads):
    i = pl.program_id(0)
    tm, ch = x_ref.shape
    bd = ch // heads
    x = x_ref[...].astype(F32)

    def run(prev, h0, n_seq, seqlen, first_row):
        xc = _causal_conv(x, prev, cw_ref[...], cb_ref[...], n_seq, seqlen)
        ra, rx = [], []
        for h in range(heads):
            xh = xc[:, h * bd:(h + 1) * bd].astype(BF16)
            ra.append(jnp.dot(xh, wa_ref[h], preferred_element_type=F32))
            rx.append(jnp.dot(xh, wx_ref[h], preferred_element_type=F32))
        r = _sigmoid(jnp.concatenate(ra, axis=-1) + ba_ref[...])
        ig = _sigmoid(jnp.concatenate(rx, axis=-1) + bx_ref[...])
        log_a = RG_C * r * _log_sigmoid(lam_ref[...])
        a = jnp.exp(log_a)
        m2 = -jnp.tanh(log_a) * (a * a + 1.0)
        mult = jnp.where(m2 > 0.0, m2 * lax.rsqrt(m2), 0.0)
        row = lax.broadcasted_iota(jnp.int32, (n_seq, seqlen, ch), 1)
        if first_row is not None:
            mult = jnp.where(row.reshape(tm, ch) + first_row == 0, 1.0, mult)
        u = xc * ig * mult
        ng = tm // SUBLANES
        ag = a.reshape(ng, SUBLANES, ch)
        ug = u.reshape(ng, SUBLANES, ch)
        grow = lax.broadcasted_iota(jnp.int32, (ng, SUBLANES, ch), 1)
        d = 1
        while d < SUBLANES:
            live = grow >= d
            a_sh = jnp.where(live, pltpu.roll(ag, d, axis=1), 1.0)
            u_sh = jnp.where(live, pltpu.roll(ug, d, axis=1), 0.0)
            ug = ug + ag * u_sh
            ag = ag * a_sh
            d *= 2
        groups_per_seq = seqlen // SUBLANES
        hs, last = [], []
        for gi in range(ng):
            if gi % groups_per_seq == 0:
                carry = h0[gi // groups_per_seq]
            hg = ug[gi] + ag[gi] * carry
            carry = hg[SUBLANES - 1:]
            hs.append(hg)
            if (gi + 1) % groups_per_seq == 0:
                last.append(carry)
        h_all = jnp.concatenate(hs, axis=0)
        o_ref[...] = (h_all * _gelu_tanh(gate_ref[...].astype(F32))).astype(o_ref.dtype)
        x3 = x.reshape(n_seq, seqlen, ch)
        return x3[:, seqlen - prev.shape[1]:, :], jnp.stack(last, axis=0)

    @pl.when(i == 0)
    def _():
        cvp_ref[...] = jnp.zeros(cvp_ref.shape, F32)
        hp_ref[...] = jnp.zeros(hp_ref.shape, F32)

    @pl.when(i < n_prompt_tiles)
    def _():
        cv, hl = run(cvp_ref[...], hp_ref[...], 1, tm, i * tm)
        cvp_ref[...] = cv
        hp_ref[...] = hl

    @pl.when(i >= n_prompt_tiles)
    def _():
        cv, hl = run(cv0_ref[...], h0_ref[...], nseq, CHUNK, None)
        s0 = (i - n_prompt_tiles) * nseq
        cvs_ref[pl.ds(s0, nseq)] = cv
        hs_ref[pl.ds(s0, nseq)] = hl


def rglru(proj, x_col, gate_col, conv_w, conv_b, w_a, b_a, w_x, b_x, lam, cv0, h0, *, n_prompt_rows, tm):
    m = proj.shape[0]
    heads, bd, _ = w_a.shape
    ch = heads * bd
    ns, wm1 = cv0.shape[0], cv0.shape[1]
    npt = n_prompt_rows // tm
    nseq = tm // CHUNK
    samp = lambda i: (jnp.maximum(i - npt, 0), 0, 0)
    const2 = lambda i: (0, 0)
    const3 = lambda i: (0, 0, 0)
    return pl.pallas_call(
        functools.partial(_rglru_body, n_prompt_tiles=npt, nseq=nseq, heads=heads),
        grid=(m // tm,),
        in_specs=[pl.BlockSpec((tm, ch), lambda i: (i, x_col)),
                  pl.BlockSpec((tm, ch), lambda i: (i, gate_col)),
                  pl.BlockSpec((wm1 + 1, ch), const2),
                  pl.BlockSpec((1, ch), const2),
                  pl.BlockSpec((heads, bd, bd), const3),
                  pl.BlockSpec((1, ch), const2),
                  pl.BlockSpec((heads, bd, bd), const3),
                  pl.BlockSpec((1, ch), const2),
                  pl.BlockSpec((1, ch), const2),
                  pl.BlockSpec((nseq, wm1, ch), samp),
                  pl.BlockSpec((nseq, 1, ch), samp)],
        out_specs=[pl.BlockSpec((tm, ch), lambda i: (i, 0)),
                   pl.BlockSpec((1, wm1, ch), const3),
                   pl.BlockSpec((1, 1, ch), const3),
                   pl.BlockSpec((ns, wm1, ch), const3),
                   pl.BlockSpec((ns, 1, ch), const3)],
        out_shape=[jax.ShapeDtypeStruct((m, ch), BF16),
                   jax.ShapeDtypeStruct((1, wm1, ch), F32),
                   jax.ShapeDtypeStruct((1, 1, ch), F32),
                   jax.ShapeDtypeStruct((ns, wm1, ch), F32),
                   jax.ShapeDtypeStruct((ns, 1, ch), F32)],
        compiler_params=_cparams(1),
        name="rglru",
    )(proj, proj, conv_w, conv_b.reshape(1, ch), w_a.astype(BF16), b_a.reshape(1, ch),
      w_x.astype(BF16), b_x.reshape(1, ch), lam.reshape(1, ch), cv0, h0)


def _gated_chunk(q, k, g, v, st_ref, kb_ref, heads):
    c = q.shape[0]
    dk = q.shape[1] // heads
    dv = v.shape[1] // heads
    nsub = c // SUB
    b = _dot01_left(_tril01(c), g) * LOG2_E
    b_end = b[c - 1:c, :]
    qb = (q * jnp.exp2(b)).astype(BF16)
    k_end = (k * jnp.exp2(b_end - b)).astype(BF16)
    dec_end = jnp.exp2(b_end)
    vb = v.astype(BF16)
    kb_ref[0] = k
    kb_ref[1] = b
    lane = lax.broadcasted_iota(jnp.int32, (SUB, c), 1)
    srow = lax.broadcasted_iota(jnp.int32, (SUB, c), 0)
    lane_h = lax.broadcasted_iota(jnp.int32, (SUB // 2, c), 1)
    outs = []
    for h in range(heads):
        sk = slice(h * dk, (h + 1) * dk)
        sv = slice(h * dv, (h + 1) * dv)
        st = st_ref[0, h]
        o = lax.dot_general(qb[:, sk], st.astype(BF16), NT_DIMS, preferred_element_type=F32)
        qh, kh, bh, vh = q[:, sk], k[:, sk], b[:, sk], vb[:, sv]
        blocks = []
        for i in range(nsub):
            r0 = i * SUB
            qi, ki, bi = qh[r0:r0 + SUB], kh[r0:r0 + SUB], bh[r0:r0 + SUB]
            half = SUB // 2
            tops, bots = jnp.zeros((half, c), F32), jnp.zeros((half, c), F32)
            for s in range(SUB):
                ks = kb_ref[0, r0 + s:r0 + s + 1, sk]
                bs = kb_ref[1, r0 + s:r0 + s + 1, sk]
                if s < half:
                    col = jnp.sum(qi[:half] * (ks * jnp.exp2(bi[:half] - bs)), axis=-1, keepdims=True)
                    tops = jnp.where(lane_h == r0 + s, col, tops)
                col = jnp.sum(qi[half:] * (ks * jnp.exp2(bi[half:] - bs)), axis=-1, keepdims=True)
                bots = jnp.where(lane_h == r0 + s, col, bots)
            a = jnp.where(srow >= lane - r0, jnp.concatenate([tops, bots], axis=0), 0.0)
            if i > 0:
                ri = bh[r0:r0 + 1]
                qt = (qi * jnp.exp2(bi - ri)).astype(BF16)
                kt = jnp.concatenate([kh[:r0] * jnp.exp2(ri - bh[:r0]),
                                      jnp.zeros((c - r0, dk), F32)], axis=0).astype(BF16)
                a = a + lax.dot_general(qt, kt, NT_DIMS, preferred_element_type=F32)
            blocks.append(a)
        a_full = jnp.concatenate(blocks, axis=0).astype(BF16)
        o = o + jnp.dot(a_full, vh, preferred_element_type=F32)
        st_ref[0, h] = st * dec_end[:, sk] + lax.dot_general(
            vh, k_end[:, sk], TN_DIMS, preferred_element_type=F32)
        outs.append(o)
    return outs


def _head_norm_gate(outs, ng_ref, gate_ref):
    res = []
    for h, o in enumerate(outs):
        dv = o.shape[-1]
        sv = slice(h * dv, (h + 1) * dv)
        res.append(_rmsnorm_rows(o, ng_ref[:, sv]) * _silu(gate_ref[:, sv].astype(F32)))
    return jnp.concatenate(res, axis=-1)


def _hgrn_body(q_ref, f_ref, v_ref, gate_ref, lb_ref, ng_ref, s0_ref, o_ref, st_ref, kb_ref,
               *, n_prompt_chunks, heads):
    c = pl.program_id(0)

    @pl.when((c == 0) | (c >= n_prompt_chunks))
    def _():
        st_ref[...] = s0_ref[...]

    lb = lb_ref[...]
    k = (1.0 - lb) / (1.0 + jnp.exp(f_ref[...].astype(F32)))
    g = jnp.log1p(-k)
    outs = _gated_chunk(_silu(q_ref[...].astype(F32)), k, g, v_ref[...], st_ref, kb_ref, heads)
    o_ref[...] = _head_norm_gate(outs, ng_ref, gate_ref).astype(o_ref.dtype)


def _gla_body(q_ref, k_ref, v_ref, gate_ref, gl_ref, gkw_ref, gkb_ref, ng_ref, s0_ref, o_ref, st_ref, kb_ref,
              *, n_prompt_chunks, heads):
    c = pl.program_id(0)

    @pl.when((c == 0) | (c >= n_prompt_chunks))
    def _():
        st_ref[...] = s0_ref[...]

    dk = q_ref.shape[1] // heads
    z = jnp.dot(gl_ref[...].astype(BF16), gkw_ref[...], preferred_element_type=F32) + gkb_ref[...]
    g = _log_sigmoid(z) * (1.0 / D_GATE_NORM)
    outs = _gated_chunk(q_ref[...].astype(F32) * dk ** -0.5, k_ref[...].astype(F32), g, v_ref[...], st_ref, kb_ref, heads)
    o_ref[...] = _head_norm_gate(outs, ng_ref, gate_ref).astype(o_ref.dtype)


def _chunk_state_map(n_prompt_chunks, ndim):
    return lambda c: (jnp.maximum(c - (n_prompt_chunks - 1), 0),) + (0,) * (ndim - 1)


def hgrn2(proj, lb, norm_g, s0, *, n_prompt_chunks):
    m = proj.shape[0]
    _, heads, dv, dk = s0.shape
    kw, vw = heads * dk, heads * dv
    smap = _chunk_state_map(n_prompt_chunks, 4)
    row = lambda c: (0, 0)
    return pl.pallas_call(
        functools.partial(_hgrn_body, n_prompt_chunks=n_prompt_chunks, heads=heads),
        grid=(m // CHUNK,),
        in_specs=[pl.BlockSpec((CHUNK, kw), lambda c: (c, 0)),
                  pl.BlockSpec((CHUNK, kw), lambda c: (c, 1)),
                  pl.BlockSpec((CHUNK, vw), lambda c: (c, 2 * kw // vw)),
                  pl.BlockSpec((CHUNK, vw), lambda c: (c, 2 * kw // vw + 1)),
                  pl.BlockSpec((1, kw), row),
                  pl.BlockSpec((1, vw), row),
                  pl.BlockSpec((1, heads, dv, dk), smap)],
        out_specs=[pl.BlockSpec((CHUNK, vw), lambda c: (c, 0)),
                   pl.BlockSpec((1, heads, dv, dk), smap)],
        out_shape=[jax.ShapeDtypeStruct((m, vw), BF16),
                   jax.ShapeDtypeStruct(s0.shape, F32)],
        scratch_shapes=[pltpu.VMEM((2, CHUNK, kw), F32)],
        compiler_params=_cparams(1),
        name="hgrn2",
    )(proj, proj, proj, proj, lb.reshape(1, kw), norm_g.reshape(1, vw), s0)


def gla(proj, cols, gk_w, gk_b, norm_g, s0, *, n_prompt_chunks):
    m = proj.shape[0]
    _, heads, dv, dk = s0.shape
    kw, vw = heads * dk, heads * dv
    cq, ck, cv, cg, cgl = cols
    smap = _chunk_state_map(n_prompt_chunks, 4)
    row = lambda c: (0, 0)
    return pl.pallas_call(
        functools.partial(_gla_body, n_prompt_chunks=n_prompt_chunks, heads=heads),
        grid=(m // CHUNK,),
        in_specs=[pl.BlockSpec((CHUNK, kw), lambda c: (c, cq)),
                  pl.BlockSpec((CHUNK, kw), lambda c: (c, ck)),
                  pl.BlockSpec((CHUNK, vw), lambda c: (c, cv)),
                  pl.BlockSpec((CHUNK, vw), lambda c: (c, cg)),
                  pl.BlockSpec((CHUNK, LANES), lambda c: (c, cgl)),
                  pl.BlockSpec((LANES, kw), row),
                  pl.BlockSpec((1, kw), row),
                  pl.BlockSpec((1, vw), row),
                  pl.BlockSpec((1, heads, dv, dk), smap)],
        out_specs=[pl.BlockSpec((CHUNK, vw), lambda c: (c, 0)),
                   pl.BlockSpec((1, heads, dv, dk), smap)],
        out_shape=[jax.ShapeDtypeStruct((m, vw), BF16),
                   jax.ShapeDtypeStruct(s0.shape, F32)],
        scratch_shapes=[pltpu.VMEM((2, CHUNK, kw), F32)],
        compiler_params=_cparams(1),
        name="gla",
    )(proj, proj, proj, proj, proj, gk_w, gk_b.reshape(1, kw), norm_g.reshape(1, vw), s0)


def _ssd_body(z_ref, x_ref, bc_ref, dt_ref, cw_ref, cb_ref, dtb_ref, alog_ref, dsk_ref, ng_ref,
              cv0_ref, s0_ref, o_ref, cv_ref, st_ref, *, n_prompt_chunks, heads, groups):
    c = pl.program_id(0)

    @pl.when((c == 0) | (c >= n_prompt_chunks))
    def _():
        st_ref[...] = s0_ref[...]
        cv_ref[...] = cv0_ref[...]

    rows, xw_ = x_ref.shape
    hp = xw_ // heads
    n = bc_ref.shape[1] // (2 * groups)
    gw = xw_ // groups
    x_raw, bc_raw = x_ref[...].astype(F32), bc_ref[...].astype(F32)
    prev = cv_ref[...]
    cw, cb = cw_ref[...], cb_ref[...]
    xs = _silu(_causal_conv(x_raw, prev[:, :, :xw_], cw[:, :xw_], cb[:, :xw_], 1, rows))
    bcs = _silu(_causal_conv(bc_raw, prev[:, :, xw_:], cw[:, xw_:], cb[:, xw_:], 1, rows))
    wm1 = prev.shape[1]
    cv_ref[0, :, :xw_] = x_raw[rows - wm1:, :]
    cv_ref[0, :, xw_:] = bc_raw[rows - wm1:, :]

    erow = lax.broadcasted_iota(jnp.int32, (LANES, xw_), 0)
    ecol = lax.broadcasted_iota(jnp.int32, (LANES, xw_), 1)
    expand = jnp.where(ecol // hp == erow, 1.0, 0.0).astype(BF16)
    dt = _softplus(_dot01_right(dt_ref[...].astype(F32), expand) + dtb_ref[...])
    dta = dt * (-jnp.exp(alog_ref[...]))
    cum = _dot01_left(_tril01(rows), dta)
    cum_end = cum[rows - 1:rows, :]
    xdt = xs * dt
    xdt_b = xdt.astype(BF16)
    x_end = (xdt * jnp.exp(cum_end - cum)).astype(BF16)
    e_cum = jnp.exp(cum)
    dec_end = jnp.exp(cum_end)

    prow = lax.broadcasted_iota(jnp.int32, (rows, LANES), 0)
    plane = lax.broadcasted_iota(jnp.int32, (rows, LANES), 1)
    assert hp == rows and LANES % hp == 0
    pair = LANES // hp
    psrc = plane % hp
    y_parts = []
    for g in range(groups):
        bg = bcs[:, g * n:(g + 1) * n].astype(BF16)
        cg = bcs[:, (groups + g) * n:(groups + g + 1) * n].astype(BF16)
        gl = slice(g * gw, (g + 1) * gw)
        st = st_ref[0, :, gl]
        y_inter = jnp.dot(cg, st.astype(BF16), preferred_element_type=F32) * e_cum[:, gl]
        cb_rep = lax.dot_general(cg, jnp.concatenate([bg] * pair, axis=0), NT_DIMS, preferred_element_type=F32)
        y_intra = []
        for sl in range(gw // LANES):
            lo = g * gw + sl * LANES
            cs = cum[:, lo:lo + LANES]
            cdiag = jnp.sum(jnp.where(prow == psrc, cs, 0.0), axis=0, keepdims=True)
            decay = jnp.exp(jnp.minimum(cs - cdiag, 0.0))
            mt = jnp.where(prow >= psrc, cb_rep * decay, 0.0).astype(BF16)
            xp = xdt_b[:, lo:lo + LANES]
            rhs = jnp.concatenate(
                [jnp.where(plane // hp == p, xp, jnp.zeros_like(xp)) for p in range(pair)], axis=0)
            y_intra.append(jnp.dot(mt, rhs, preferred_element_type=F32))
        y_parts.append(jnp.concatenate(y_intra, axis=-1) + y_inter)
        st_ref[0, :, gl] = st * dec_end[:, gl] + lax.dot_general(
            bg, x_end[:, gl], TN_DIMS, preferred_element_type=F32)
    y = jnp.concatenate(y_parts, axis=-1) + dsk_ref[...] * xs
    yz = y * _silu(z_ref[...].astype(F32))
    res = []
    for g in range(groups):
        gl = slice(g * gw, (g + 1) * gw)
        res.append(_rmsnorm_rows(yz[:, gl], ng_ref[:, gl]))
    o_ref[...] = jnp.concatenate(res, axis=-1).astype(o_ref.dtype)


def ssd(proj, cols, conv_w, conv_b, dt_bias_x, a_log_x, d_skip_x, norm_g, cv0, s0, *, n_prompt_chunks):
    m = proj.shape[0]
    xw_ = s0.shape[2]
    n = s0.shape[1]
    bcw = 2 * C_GROUPS * n
    cz, cx, cbc, cdt = cols
    wm1 = cv0.shape[1]
    smap3 = _chunk_state_map(n_prompt_chunks, 3)
    row = lambda c: (0, 0)
    return pl.pallas_call(
        functools.partial(_ssd_body, n_prompt_chunks=n_prompt_chunks, heads=C_HEADS, groups=C_GROUPS),
        grid=(m // CHUNK,),
        in_specs=[pl.BlockSpec((CHUNK, xw_), lambda c: (c, cz)),
                  pl.BlockSpec((CHUNK, xw_), lambda c: (c, cx)),
                  pl.BlockSpec((CHUNK, bcw), lambda c: (c, cbc)),
                  pl.BlockSpec((CHUNK, LANES), lambda c: (c, cdt)),
                  pl.BlockSpec((wm1 + 1, xw_ + bcw), row),
                  pl.BlockSpec((1, xw_ + bcw), row),
                  pl.BlockSpec((1, xw_), row),
                  pl.BlockSpec((1, xw_), row),
                  pl.BlockSpec((1, xw_), row),
                  pl.BlockSpec((1, xw_), row),
                  pl.BlockSpec((1, wm1, xw_ + bcw), smap3),
                  pl.BlockSpec((1, n, xw_), smap3)],
        out_specs=[pl.BlockSpec((CHUNK, xw_), lambda c: (c, 0)),
                   pl.BlockSpec((1, wm1, xw_ + bcw), smap3),
                   pl.BlockSpec((1, n, xw_), smap3)],
        out_shape=[jax.ShapeDtypeStruct((m, xw_), BF16),
                   jax.ShapeDtypeStruct(cv0.shape, F32),
                   jax.ShapeDtypeStruct(s0.shape, F32)],
        compiler_params=_cparams(1),
        name="ssd",
    )(proj, proj, proj, proj, conv_w, conv_b.reshape(1, xw_ + bcw), dt_bias_x.reshape(1, xw_),
      a_log_x.reshape(1, xw_), d_skip_x.reshape(1, xw_), norm_g.reshape(1, xw_), cv0, s0)


def _with_zero_first(s):
    return jnp.concatenate([jnp.zeros((1,) + s.shape[1:], s.dtype), s], axis=0)


def _col_tile_starts(width, tn):
    return [min(j * tn, width - tn) for j in range(-(-width // tn))]


def _block_cols(a, width, tn):
    return jnp.stack([a[..., o:o + tn] for o in _col_tile_starts(width, tn)], axis=0)


def _unblock_cols(b, width, tn):
    return jnp.concatenate([b[j][..., j * tn - o:] for j, o in enumerate(_col_tile_starts(width, tn))], axis=-1)


def kernel(x_prompt, x_sample, mem_prompt, cache_mem_k, cache_mem_v, state_hgrn, state_rglru, state_rg_conv, state_ssd, state_ssd_conv, state_gla, state_ffn_conv, norm_mix, norm_xattn, norm_mem, norm_ffn, ev_w_in, hgrn_lb_logits, hgrn_norm, rg_conv_w, rg_conv_b, rg_w_a, rg_b_a, rg_w_x, rg_b_x, rg_lambda, ev_w_out, od_w_in, ssd_conv_w, ssd_conv_b, ssd_dt_bias, ssd_A_log, ssd_D, ssd_norm, gla_gk_w, gla_gk_b, gla_norm, od_w_out, xa_wq, xa_wk, xa_wv, xa_wo, ffn_up, ffn_conv_w, ffn_conv_b, ffn_down, final_norm):
    bp, p_rows, d = x_prompt.shape
    ns, s_len, _ = x_sample.shape
    assert bp == 1 and s_len == CHUNK and p_rows % CHUNK == 0
    depth = norm_mix.shape[0]
    s_rows = ns * s_len
    m = p_rows + s_rows
    npc = p_rows // CHUNK
    nm = mem_prompt.shape[1]
    tm = math.gcd(math.gcd(p_rows, s_rows), 1024)
    tm_seq = math.gcd(tm, 256)

    x = (x_prompt.reshape(p_rows, d), x_sample.reshape(s_rows, d))
    lb_all = jnp.cumsum(jax.nn.softmax(hgrn_lb_logits.astype(F32), axis=0), axis=0)
    ev_w_in_b, ev_w_out_b, od_w_out_b = ev_w_in.astype(BF16), ev_w_out.astype(BF16), od_w_out.astype(BF16)
    wq_b, wk_b, wv_b, wo_b = xa_wq.astype(BF16), xa_wk.astype(BF16), xa_wv.astype(BF16), xa_wo.astype(BF16)
    w_up_b, w_down_b = ffn_up.astype(BF16), ffn_down.astype(BF16)

    a_kw = hgrn_lb_logits.shape[1]
    a_vw = hgrn_norm.shape[1]
    b_w = rg_lambda.shape[1]
    c_w = ssd_norm.shape[1]
    c_bc = 2 * C_GROUPS * C_STATE
    c_hd = c_w // C_HEADS
    d_val = gla_norm.shape[1]
    d_key = gla_gk_b.shape[1]
    d_rank = gla_gk_w.shape[1]
    ff = ffn_conv_b.shape[1]
    ffn_tn = 512
    wide_tn = lambda rows: 512 if isinstance(rows, tuple) else 1536
    tm_full = tm // 2

    mem_k, mem_v = [], []
    hg, rl, rc, ss, sc, gs, fc_p, fc_s = [], [], [], [], [], [], [], []
    for l in range(depth):
        j = l // 2
        if l % 2 == 0:
            proj = norm_matmul(x, norm_mix[l], ev_w_in_b, j, tm=tm, tn=wide_tn(x), out_dtype=PROJ_DTYPE)
            s0 = _with_zero_first(jnp.swapaxes(state_hgrn[j], -1, -2))
            o_a, st = hgrn2(proj, lb_all[l], hgrn_norm[j], s0, n_prompt_chunks=npc)
            hg.append(jnp.swapaxes(st, -1, -2))
            xcol = (2 * a_kw + 2 * a_vw) // b_w
            o_b, cvp, hp_, cvs, hs_ = rglru(
                proj, xcol, xcol + 1, rg_conv_w[j], rg_conv_b[j], rg_w_a[j], rg_b_a[j], rg_w_x[j], rg_b_x[j],
                rg_lambda[j], state_rg_conv[j], state_rglru[j].reshape(ns, 1, b_w), n_prompt_rows=p_rows, tm=tm_seq)
            rc.append((cvp, cvs))
            rl.append((hp_.reshape(1, b_w), hs_.reshape(ns, b_w)))
            x = matmul_res([(o_a, 0, a_vw, 0), (o_b, 0, b_w, a_vw)], ev_w_out_b, j, x, tm=tm_full, tn=d)
        else:
            w = od_w_in[j]
            offs = np_cumsum_offsets((c_w, c_w + c_bc, C_HEADS, d_key, d_key, d_val, d_val, d_rank))
            o_z, o_xbc, o_dt, o_q, o_k, o_v, o_g, o_gl = offs
            pad = lambda a: jnp.pad(a, ((0, 0), (0, LANES - a.shape[1])))
            w_re = jnp.concatenate([
                w[:, o_z:o_z + c_w], w[:, o_xbc:o_xbc + c_w], w[:, o_xbc + c_w:o_xbc + c_w + c_bc],
                w[:, o_q:o_q + d_key], w[:, o_k:o_k + d_key],
                pad(w[:, o_dt:o_dt + C_HEADS]), pad(w[:, o_gl:o_gl + d_rank]),
                jnp.zeros((d, 2 * LANES), w.dtype),
                w[:, o_v:o_v + d_val], w[:, o_g:o_g + d_val]], axis=1).astype(BF16)
            proj = norm_matmul(x, norm_mix[l], w_re, tm=tm, tn=wide_tn(x), out_dtype=PROJ_DTYPE)
            assert c_w == d_val
            base = 2 * c_w + c_bc
            col_dt = (base + 2 * d_key) // LANES
            cv0 = _with_zero_first(state_ssd_conv[j])
            s0c = _with_zero_first(jnp.transpose(state_ssd[j], (0, 3, 1, 2)).reshape(ns, C_STATE, c_w))
            rep = lambda a: jnp.repeat(a, c_hd)
            o_c, cv, stc = ssd(proj, (0, 1, 2 * c_w // c_bc, col_dt), ssd_conv_w[j], ssd_conv_b[j],
                               rep(ssd_dt_bias[j]), rep(ssd_A_log[j]), rep(ssd_D[j]), ssd_norm[j], cv0, s0c,
                               n_prompt_chunks=npc)
            sc.append(cv)
            ss.append(jnp.transpose(stc.reshape(1 + ns, C_STATE, C_HEADS, c_hd), (0, 2, 3, 1)))
            s0d = _with_zero_first(jnp.swapaxes(state_gla[j], -1, -2))
            gkw = jnp.pad(gla_gk_w[j], ((0, LANES - d_rank), (0, 0))).astype(BF16)
            v_start = base + 2 * d_key + 4 * LANES
            o_d, std = gla(proj, (base // d_key, base // d_key + 1, v_start // d_val, v_start // d_val + 1, col_dt + 1),
                           gkw, gla_gk_b[j], gla_norm[j], s0d, n_prompt_chunks=npc)
            gs.append(jnp.swapaxes(std, -1, -2))
            x = matmul_res([(o_c, 0, c_w, 0), (o_d, 0, d_val, c_w)], od_w_out_b, j, x, tm=tm_full, tn=d)

        mem = mem_prompt.reshape(nm, d)
        k_p = norm_matmul(mem, norm_mem[l], wk_b, l, tm=nm, tn=512, out_dtype=F32)
        v_p = norm_matmul(mem, norm_mem[l], wv_b, l, tm=nm, tn=512, out_dtype=F32)
        mem_k.append(k_p.reshape(1, nm, MEM_HEADS, d // MEM_HEADS))
        mem_v.append(v_p.reshape(1, nm, MEM_HEADS, d // MEM_HEADS))
        q = norm_matmul(x, norm_xattn[l], wq_b, l, tm=tm, tn=d, out_dtype=BF16)
        att = mem_attention(q, k_p, v_p, cache_mem_k, cache_mem_v, l, n_prompt_rows=p_rows, tm=tm_seq)
        x = matmul_res([(att, 0, d, 0)], wo_b, l, x, tm=tm_full, tn=d)

        prev_s = _block_cols(state_ffn_conv[l], ff, ffn_tn)
        act, stp, sts = conv_ffn_up(x, norm_ffn[l], w_up_b, l, ffn_conv_w[l], ffn_conv_b[l],
                                    prev_s, n_prompt_rows=p_rows, tm=tm, tn=ffn_tn)
        fc_p.append(_unblock_cols(stp, ff, ffn_tn))
        fc_s.append(_unblock_cols(sts, ff, ffn_tn))
        starts = _col_tile_starts(ff, ffn_tn)
        body = (len(starts) - 1) * ffn_tn
        down_terms = [(act, 0, body, 0), (act, body + (body - starts[-1]), ff - body, body)]
        x = matmul_res(down_terms, w_down_b, l, x, tm=tm, tn=512)

    y_prompt, y_sample = rmsnorm_rows(x, final_norm, n_prompt_rows=p_rows, tm=tm)
    y_prompt = y_prompt.reshape(1, p_rows, d)
    y_sample = y_sample.reshape(ns, s_len, d)
    stack_p = lambda lst: jnp.stack([a[:1] for a in lst])
    stack_s = lambda lst: jnp.stack([a[1:] for a in lst])
    return (y_prompt, y_sample, jnp.stack(mem_k), jnp.stack(mem_v),
            stack_p(hg), jnp.stack([a[0] for a in rl]), jnp.stack([a[0] for a in rc]),
            stack_p(ss), stack_p(sc), stack_p(gs), jnp.stack(fc_p),
            stack_s(hg), jnp.stack([a[1] for a in rl]), jnp.stack([a[1] for a in rc]),
            stack_s(ss), stack_s(sc), stack_s(gs), jnp.stack(fc_s))


def np_cumsum_offsets(sizes):
    offs, acc = [], 0
    for s in sizes:
        offs.append(acc)
        acc += s
    return offs
```

```python
import functools
import math

import jax
import jax.numpy as jnp
from jax import lax
from jax.experimental import pallas as pl
from jax.experimental.pallas import tpu as pltpu

F32 = jnp.float32
BF16 = jnp.bfloat16
PROJ_DTYPE = F32

CHUNK = 64
SUB = 16
EPS = 1e-6
LOG2_E = 1.4426950408889634
LANES = 128
SUBLANES = 8
VMEM_LIMIT = 56 * 1024 * 1024

MEM_HEADS = 4
A_HEADS = 8
B_HEADS = 8
C_HEADS = 16
C_GROUPS = 2
C_STATE = 128
D_HEADS = 4
RG_C = 8.0
D_GATE_NORM = 16.0

NT_DIMS = (((1,), (1,)), ((), ()))
TN_DIMS = (((0,), (0,)), ((), ()))


def _cparams(n_axes):
    return pltpu.CompilerParams(dimension_semantics=("arbitrary",) * n_axes,
                                vmem_limit_bytes=VMEM_LIMIT)


def _sigmoid(x):
    return 1.0 / (1.0 + jnp.exp(-x))


def _silu(x):
    return x * _sigmoid(x)


def _log_sigmoid(x):
    return jnp.minimum(x, 0.0) - jnp.log1p(jnp.exp(-jnp.abs(x)))


def _softplus(x):
    return jnp.maximum(x, 0.0) + jnp.log1p(jnp.exp(-jnp.abs(x)))


def _gelu_tanh(x):
    return 0.5 * x * (1.0 + jnp.tanh(math.sqrt(2.0 / math.pi) * (x + 0.044715 * (x * x * x))))


def _split3(x):
    hi = x.astype(BF16)
    r = x - hi.astype(F32)
    mid = r.astype(BF16)
    lo = (r - mid.astype(F32)).astype(BF16)
    return hi, mid, lo


def _dot01_left(t01, x):
    return sum(jnp.dot(t01, p, preferred_element_type=F32) for p in _split3(x))


def _dot01_right(x, e01):
    return sum(jnp.dot(p, e01, preferred_element_type=F32) for p in _split3(x))


def _tril01(n):
    r = lax.broadcasted_iota(jnp.int32, (n, n), 0)
    c = lax.broadcasted_iota(jnp.int32, (n, n), 1)
    return jnp.where(r >= c, 1.0, 0.0).astype(BF16)


def _causal_conv(x, prev, w, b, nseq, seqlen, fresh=None):
    width = w.shape[0]
    ch = x.shape[-1]
    head = 8
    tap = lambda d: w[width - 1 - d:width - d].reshape(1, 1, ch)
    x3 = x.reshape(nseq, seqlen, ch)
    rolled = [pltpu.roll(x, d, axis=0).reshape(nseq, seqlen, ch) for d in range(1, width)]
    y = b + tap(0) * x3
    for d in range(1, width):
        y = y + tap(d) * rolled[d - 1]
    row = lax.broadcasted_iota(jnp.int32, (nseq, head, ch), 1)
    yh = b + tap(0) * x3[:, :head]
    for d in range(1, width):
        xs = rolled[d - 1][:, :head]
        for r in range(d):
            p = width - 1 - d + r
            xs = jnp.where(row == r, prev[:, p:p + 1, :], xs)
        yh = yh + tap(d) * xs
    if fresh is not None:
        yh = jnp.where(fresh, yh, y[:, :head])
    y = jnp.concatenate([yh, y[:, head:]], axis=1)
    return y.reshape(nseq * seqlen, ch)


def _rmsnorm_rows(x, g):
    ms = jnp.mean(x * x, axis=-1, keepdims=True)
    return x * lax.rsqrt(ms + EPS) * g


def _row_specs(rows, tm, width, col=lambda j: 0, single_buffer_pair=False):
    blk = (pl.Element(tm), pl.Element(width))
    row = lambda t: pl.multiple_of(t * tm, tm)
    if not isinstance(rows, tuple):
        return [pl.BlockSpec(blk, lambda i, j: (row(i), col(j)))], rows.shape[0], None
    npt = rows[0].shape[0] // tm
    mode = dict(pipeline_mode=pl.Buffered(1)) if single_buffer_pair else {}
    specs = [pl.BlockSpec(blk, lambda i, j: (row(jnp.minimum(i, npt - 1)), col(j)), **mode),
             pl.BlockSpec(blk, lambda i, j: (row(jnp.maximum(i - npt, 0)), col(j)), **mode)]
    return specs, rows[0].shape[0] + rows[1].shape[0], npt


def _for_row_source(npt, fn, also=True):
    if npt is None and also is True:
        fn(0)
    elif npt is None:
        pl.when(also)(lambda: fn(0))
    else:
        i = pl.program_id(0)
        pl.when(also & (i < npt))(lambda: fn(0))
        pl.when(also & (i >= npt))(lambda: fn(1))


def _norm_matmul_body(*refs, npt):
    n_x = 1 if npt is None else 2
    x_refs = refs[:n_x]
    g_ref, w_ref, o_ref, xn_ref = refs[n_x:]

    def normalize(k):
        xn_ref[...] = _rmsnorm_rows(x_refs[k][...], g_ref[...]).astype(BF16)

    _for_row_source(npt, normalize, also=pl.program_id(1) == 0)
    o_ref[...] = jnp.dot(xn_ref[...], w_ref[...], preferred_element_type=F32).astype(o_ref.dtype)


def _weight_spec(w, layer, k_rows, k_block, tn):
    if layer is None:
        return pl.BlockSpec((k_rows, tn), lambda i, j: (k_block, j))
    return pl.BlockSpec((None, k_rows, tn), lambda i, j: (layer, k_block, j))


def norm_matmul(x, g, w, layer=None, *, tm, tn, out_dtype):
    d, n = w.shape[-2:]
    x_specs, m, npt = _row_specs(x, tm, d, single_buffer_pair=True)
    xs = x if isinstance(x, tuple) else (x,)
    return pl.pallas_call(
        functools.partial(_norm_matmul_body, npt=npt),
        grid=(m // tm, n // tn),
        in_specs=x_specs + [pl.BlockSpec((1, d), lambda i, j: (0, 0)), _weight_spec(w, layer, d, 0, tn)],
        out_specs=pl.BlockSpec((tm, tn), lambda i, j: (i, j)),
        out_shape=jax.ShapeDtypeStruct((m, n), out_dtype),
        scratch_shapes=[pltpu.VMEM((tm, d), BF16)],
        compiler_params=_cparams(2),
        name="norm_matmul",
    )(*xs, g.reshape(1, d), w)


def _matmul_res_body(*refs, arity, npt):
    pos = 0
    terms = []
    for n_a in arity[:-1]:
        terms.append((refs[pos:pos + n_a], refs[pos + n_a]))
        pos += n_a + 1
    r_refs, o_ref = refs[pos:pos + arity[-1]], refs[-1]

    def compute(k):
        acc = r_refs[min(k, len(r_refs) - 1)][...]
        for a_refs, w_ref in terms:
            acc = acc + jnp.dot(a_refs[min(k, len(a_refs) - 1)][...], w_ref[0], preferred_element_type=F32)
        o_ref[...] = acc

    _for_row_source(npt, compute)


def matmul_res(terms, w, layer, res, *, tm, tn):
    n = w.shape[-1]
    el = pl.Element
    in_specs, operands, arity, npts = [], [], [], set()
    for a, a_col, k, w_row in terms:
        specs, m, npt = _row_specs(a, tm, k, col=lambda j, c=a_col: c)
        in_specs += specs + [pl.BlockSpec((el(1), el(k), el(tn)),
                                          lambda i, j, r=w_row: (layer, r, pl.multiple_of(j * tn, tn)))]
        operands += list(a if isinstance(a, tuple) else (a,)) + [w]
        arity.append(len(specs))
        npts.add(npt)
    r_specs, m, npt = _row_specs(res, tm, tn, col=lambda j: pl.multiple_of(j * tn, tn))
    arity.append(len(r_specs))
    npts = (npts | {npt}) - {None}
    assert len(npts) <= 1
    return pl.pallas_call(
        functools.partial(_matmul_res_body, arity=tuple(arity), npt=npts.pop() if npts else None),
        grid=(m // tm, n // tn),
        in_specs=in_specs + r_specs,
        out_specs=pl.BlockSpec((tm, tn), lambda i, j: (i, j)),
        out_shape=jax.ShapeDtypeStruct((m, n), F32),
        compiler_params=_cparams(2),
        name="matmul_res",
    )(*operands, *(res if isinstance(res, tuple) else (res,)))


def _rmsnorm_body(x_ref, g_ref, op_ref, os_ref, *, npt):
    y = _rmsnorm_rows(x_ref[...], g_ref[...])

    @pl.when(pl.program_id(0) < npt)
    def _():
        op_ref[...] = y

    @pl.when(pl.program_id(0) >= npt)
    def _():
        os_ref[...] = y


def rmsnorm_rows(x, g, *, n_prompt_rows, tm):
    m, d = x.shape
    npt = n_prompt_rows // tm
    return pl.pallas_call(
        functools.partial(_rmsnorm_body, npt=npt),
        grid=(m // tm,),
        in_specs=[pl.BlockSpec((tm, d), lambda i: (i, 0)), pl.BlockSpec((1, d), lambda i: (0, 0))],
        out_specs=[pl.BlockSpec((tm, d), lambda i: (jnp.minimum(i, npt - 1), 0)),
                   pl.BlockSpec((tm, d), lambda i: (jnp.maximum(i - npt, 0), 0))],
        out_shape=[jax.ShapeDtypeStruct((n_prompt_rows, d), F32),
                   jax.ShapeDtypeStruct((m - n_prompt_rows, d), F32)],
        compiler_params=_cparams(1),
        name="final_norm",
    )(x, g.reshape(1, d))


def _attn_head(qh, kh, vh):
    s = lax.dot_general(qh, kh, NT_DIMS, preferred_element_type=F32) * qh.shape[-1] ** -0.5
    p = jnp.exp(s - jnp.max(s, axis=-1, keepdims=True))
    p = p / jnp.sum(p, axis=-1, keepdims=True)
    return jnp.dot(p.astype(BF16), vh, preferred_element_type=F32)


def _attn_prompt_body(q_ref, k_ref, v_ref, o_ref, *, heads):
    hd = q_ref.shape[1] // heads
    outs = []
    for h in range(heads):
        sl = slice(h * hd, (h + 1) * hd)
        outs.append(_attn_head(q_ref[:, sl], k_ref[:, sl].astype(BF16), v_ref[:, sl].astype(BF16)))
    o_ref[...] = jnp.concatenate(outs, axis=-1).astype(o_ref.dtype)


def _attn_sample_body(q_ref, k_ref, v_ref, o_ref, *, heads):
    rows = q_ref.shape[0]
    nm, _, hd = k_ref.shape
    k2 = k_ref[...].reshape(nm * heads, hd).astype(BF16)
    v2 = v_ref[...].reshape(nm * heads, hd).astype(BF16)
    qs = jnp.concatenate([q_ref[:, h * hd:(h + 1) * hd] for h in range(heads)], axis=0)
    s = lax.dot_general(qs, k2, NT_DIMS, preferred_element_type=F32) * hd ** -0.5
    q_head = lax.broadcasted_iota(jnp.int32, s.shape, 0) // rows
    k_head = lax.broadcasted_iota(jnp.int32, s.shape, 1) % heads
    own = q_head == k_head
    m = jnp.max(jnp.where(own, s, -jnp.inf), axis=-1, keepdims=True)
    p = jnp.where(own, jnp.exp(s - m), 0.0)
    p = p / jnp.sum(p, axis=-1, keepdims=True)
    o = jnp.dot(p.astype(BF16), v2, preferred_element_type=F32)
    o_ref[...] = jnp.concatenate([o[h * rows:(h + 1) * rows] for h in range(heads)], axis=-1).astype(o_ref.dtype)


def mem_attention(q, k_prompt, v_prompt, cache_k, cache_v, layer, *, n_prompt_rows, tm):
    m, d = q.shape
    nm = k_prompt.shape[0]
    ns, heads, hd = cache_k.shape[1], cache_k.shape[3], cache_k.shape[4]
    npc = n_prompt_rows // CHUNK
    att = pl.pallas_call(
        functools.partial(_attn_prompt_body, heads=heads),
        grid=(n_prompt_rows // tm,),
        in_specs=[pl.BlockSpec((tm, d), lambda i: (i, 0)),
                  pl.BlockSpec((nm, d), lambda i: (0, 0)),
                  pl.BlockSpec((nm, d), lambda i: (0, 0))],
        out_specs=pl.BlockSpec((tm, d), lambda i: (i, 0)),
        out_shape=jax.ShapeDtypeStruct((n_prompt_rows, d), BF16),
        compiler_params=_cparams(1),
        name="mem_attention_prompt",
    )(q, k_prompt, v_prompt)
    cache_spec = pl.BlockSpec((None, None, nm, heads, hd), lambda s: (layer, s, 0, 0, 0))
    att_s = pl.pallas_call(
        functools.partial(_attn_sample_body, heads=heads),
        grid=(ns,),
        in_specs=[pl.BlockSpec((CHUNK, d), lambda s: (npc + s, 0)), cache_spec, cache_spec],
        out_specs=pl.BlockSpec((CHUNK, d), lambda s: (s, 0)),
        out_shape=jax.ShapeDtypeStruct((m - n_prompt_rows, d), BF16),
        compiler_params=_cparams(1),
        name="mem_attention_sample",
    )(q, cache_k, cache_v)
    return att, att_s


def _ffn_up_body(x_ref, g_ref, wg_ref, wv_ref, cw_ref, cb_ref, prev_ref, act_ref, stp_ref, sts_ref,
                 xn_ref, *, n_prompt_tiles, nseq, col_splits):
    i = pl.program_id(0)
    j = pl.program_id(1)
    tn = act_ref.shape[1]
    wm1 = stp_ref.shape[2]
    sw = tn // col_splits

    @pl.when((i == 0) & (j == 0))
    def _():
        stp_ref[...] = jnp.zeros(stp_ref.shape, F32)

    @pl.when(j == 0)
    def _():
        xn_ref[...] = _rmsnorm_rows(x_ref[...], g_ref[...]).astype(BF16)

    xn = xn_ref[...]
    sample = jnp.where(i >= n_prompt_tiles, 1, 0)
    chunk = lax.broadcasted_iota(jnp.int32, (nseq, 1, sw), 0)
    from_state = chunk * 0 + sample == 1
    fresh = chunk * (1 - sample) == 0
    hist = []
    for k in range(col_splits):
        cs = slice(k * sw, (k + 1) * sw)
        gate = jnp.dot(xn, wg_ref[0, :, cs], preferred_element_type=F32)
        val = jnp.dot(xn, wv_ref[0, :, cs], preferred_element_type=F32)
        prev = jnp.where(from_state, prev_ref[0, :, :, cs], stp_ref[j, :, :, cs])
        gc = _causal_conv(gate, prev, cw_ref[:, cs], cb_ref[:, cs], nseq, CHUNK, fresh=fresh)
        act_ref[:, cs] = (_silu(gc) * val).astype(act_ref.dtype)
        hist.append(gate.reshape(nseq, CHUNK, sw)[:, CHUNK - wm1:, :])
    hist = jnp.concatenate(hist, axis=-1)

    @pl.when(i < n_prompt_tiles)
    def _():
        stp_ref[j] = hist[nseq - 1:]

    @pl.when(i >= n_prompt_tiles)
    def _():
        sts_ref[j, pl.ds((i - n_prompt_tiles) * nseq, nseq)] = hist


def conv_ffn_up(x, g, w_up, layer, conv_w, conv_b, prev_s, *, n_prompt_rows, tm, tn):
    m, d = x.shape
    ff = w_up.shape[2] // 2
    nj = -(-ff // tn)
    ns, wm1 = prev_s.shape[1], prev_s.shape[2]
    npt = n_prompt_rows // tm
    nseq = tm // CHUNK
    col = lambda j, base=0: pl.multiple_of(base + jnp.minimum(j * tn, ff - tn), LANES)
    assert ff % LANES == 0 and tn % LANES == 0
    el = pl.Element
    return pl.pallas_call(
        functools.partial(_ffn_up_body, n_prompt_tiles=npt, nseq=nseq, col_splits=2),
        grid=(m // tm, nj),
        in_specs=[pl.BlockSpec((tm, d), lambda i, j: (i, 0)),
                  pl.BlockSpec((1, d), lambda i, j: (0, 0)),
                  pl.BlockSpec((el(1), el(d), el(tn)), lambda i, j: (layer, 0, col(j))),
                  pl.BlockSpec((el(1), el(d), el(tn)), lambda i, j: (layer, 0, col(j, ff))),
                  pl.BlockSpec((el(wm1 + 1), el(tn)), lambda i, j: (0, col(j))),
                  pl.BlockSpec((el(1), el(tn)), lambda i, j: (0, col(j))),
                  pl.BlockSpec((1, nseq, wm1, tn), lambda i, j: (j, jnp.maximum(i - npt, 0), 0, 0))],
        out_specs=[pl.BlockSpec((tm, tn), lambda i, j: (i, j)),
                   pl.BlockSpec((nj, 1, wm1, tn), lambda i, j: (0, 0, 0, 0)),
                   pl.BlockSpec((nj, ns, wm1, tn), lambda i, j: (0, 0, 0, 0))],
        out_shape=[jax.ShapeDtypeStruct((m, nj * tn), BF16),
                   jax.ShapeDtypeStruct((nj, 1, wm1, tn), F32),
                   jax.ShapeDtypeStruct((nj, ns, wm1, tn), F32)],
        scratch_shapes=[pltpu.VMEM((tm, d), BF16)],
        compiler_params=_cparams(2),
        name="ffn_up",
    )(x, g.reshape(1, d), w_up, w_up, conv_w, conv_b.reshape(1, ff), prev_s)


def _rglru_body(x_ref, gate_ref, cw_ref, cb_ref, wa_ref, ba_ref, wx_ref, bx_ref, lam_ref,
                cv0_ref, h0_ref, o_ref, cvp_ref, hp_ref, cvs_ref, hs_ref, *, n_prompt_tiles, nseq, heads):
    i = pl.program_id(0)
    tm, ch = x_ref.shape
    bd = ch // heads
    x = x_ref[...].astype(F32)

    def run(prev, h0, n_seq, seqlen, first_row):
        xc = _causal_conv(x, prev, cw_ref[...], cb_ref[...], n_seq, seqlen)
        ra, rx = [], []
        for h in range(heads):
            xh = xc[:, h * bd:(h + 1) * bd].astype(BF16)
            ra.append(jnp.dot(xh, wa_ref[h], preferred_element_type=F32))
            rx.append(jnp.dot(xh, wx_ref[h], preferred_element_type=F32))
        r = _sigmoid(jnp.concatenate(ra, axis=-1) + ba_ref[...])
        ig = _sigmoid(jnp.concatenate(rx, axis=-1) + bx_ref[...])
        log_a = RG_C * r * _log_sigmoid(lam_ref[...])
        a = jnp.exp(log_a)
        m2 = -jnp.tanh(log_a) * (a * a + 1.0)
        mult = jnp.where(m2 > 0.0, m2 * lax.rsqrt(m2), 0.0)
        row = lax.broadcasted_iota(jnp.int32, (n_seq, seqlen, ch), 1)
        if first_row is not None:
            mult = jnp.where(row.reshape(tm, ch) + first_row == 0, 1.0, mult)
        u = xc * ig * mult
        ng = tm // SUBLANES
        ag = a.reshape(ng, SUBLANES, ch)
        ug = u.reshape(ng, SUBLANES, ch)
        grow = lax.broadcasted_iota(jnp.int32, (ng, SUBLANES, ch), 1)
        d = 1
        while d < SUBLANES:
            live = grow >= d
            a_sh = jnp.where(live, pltpu.roll(ag, d, axis=1), 1.0)
            u_sh = jnp.where(live, pltpu.roll(ug, d, axis=1), 0.0)
            ug = ug + ag * u_sh
            ag = ag * a_sh
            d *= 2
        groups_per_seq = seqlen // SUBLANES
        hs, last = [], []
        for gi in range(ng):
            if gi % groups_per_seq == 0:
                carry = h0[gi // groups_per_seq]
            hg = ug[gi] + ag[gi] * carry
            carry = hg[SUBLANES - 1:]
            hs.append(hg)
            if (gi + 1) % groups_per_seq == 0:
                last.append(carry)
        h_all = jnp.concatenate(hs, axis=0)
        o_ref[...] = (h_all * _gelu_tanh(gate_ref[...].astype(F32))).astype(o_ref.dtype)
        x3 = x.reshape(n_seq, seqlen, ch)
        return x3[:, seqlen - prev.shape[1]:, :], jnp.stack(last, axis=0)

    @pl.when(i == 0)
    def _():
        cvp_ref[...] = jnp.zeros(cvp_ref.shape, F32)
        hp_ref[...] = jnp.zeros(hp_ref.shape, F32)

    @pl.when(i < n_prompt_tiles)
    def _():
        cv, hl = run(cvp_ref[...], hp_ref[...], 1, tm, i * tm)
        cvp_ref[...] = cv
        hp_ref[...] = hl

    @pl.when(i >= n_prompt_tiles)
    def _():
        cv, hl = run(cv0_ref[...], h0_ref[...], nseq, CHUNK, None)
        s0 = (i - n_prompt_tiles) * nseq
        cvs_ref[pl.ds(s0, nseq)] = cv
        hs_ref[pl.ds(s0, nseq)] = hl


def rglru(proj, x_col, gate_col, conv_w, conv_b, w_a, b_a, w_x, b_x, lam, cv0, h0, *, n_prompt_rows, tm):
    m = proj.shape[0]
    heads, bd, _ = w_a.shape
    ch = heads * bd
    ns, wm1 = cv0.shape[0], cv0.shape[1]
    npt = n_prompt_rows // tm
    nseq = tm // CHUNK
    samp = lambda i: (jnp.maximum(i - npt, 0), 0, 0)
    const2 = lambda i: (0, 0)
    const3 = lambda i: (0, 0, 0)
    return pl.pallas_call(
        functools.partial(_rglru_body, n_prompt_tiles=npt, nseq=nseq, heads=heads),
        grid=(m // tm,),
        in_specs=[pl.BlockSpec((tm, ch), lambda i: (i, x_col)),
                  pl.BlockSpec((tm, ch), lambda i: (i, gate_col)),
                  pl.BlockSpec((wm1 + 1, ch), const2),
                  pl.BlockSpec((1, ch), const2),
                  pl.BlockSpec((heads, bd, bd), const3),
                  pl.BlockSpec((1, ch), const2),
                  pl.BlockSpec((heads, bd, bd), const3),
                  pl.BlockSpec((1, ch), const2),
                  pl.BlockSpec((1, ch), const2),
                  pl.BlockSpec((nseq, wm1, ch), samp),
                  pl.BlockSpec((nseq, 1, ch), samp)],
        out_specs=[pl.BlockSpec((tm, ch), lambda i: (i, 0)),
                   pl.BlockSpec((1, wm1, ch), const3),
                   pl.BlockSpec((1, 1, ch), const3),
                   pl.BlockSpec((ns, wm1, ch), const3),
                   pl.BlockSpec((ns, 1, ch), const3)],
        out_shape=[jax.ShapeDtypeStruct((m, ch), BF16),
                   jax.ShapeDtypeStruct((1, wm1, ch), F32),
                   jax.ShapeDtypeStruct((1, 1, ch), F32),
                   jax.ShapeDtypeStruct((ns, wm1, ch), F32),
                   jax.ShapeDtypeStruct((ns, 1, ch), F32)],
        compiler_params=_cparams(1),
        name="rglru",
    )(proj, proj, conv_w, conv_b.reshape(1, ch), w_a.astype(BF16), b_a.reshape(1, ch),
      w_x.astype(BF16), b_x.reshape(1, ch), lam.reshape(1, ch), cv0, h0)


def _gated_chunk(q, k, g, v, st_ref, kb_ref, heads):
    c = q.shape[0]
    dk = q.shape[1] // heads
    dv = v.shape[1] // heads
    nsub = c // SUB
    b = _dot01_left(_tril01(c), g) * LOG2_E
    b_end = b[c - 1:c, :]
    qb = (q * jnp.exp2(b)).astype(BF16)
    k_end = (k * jnp.exp2(b_end - b)).astype(BF16)
    dec_end = jnp.exp2(b_end)
    vb = v.astype(BF16)
    kb_ref[0] = k
    kb_ref[1] = b
    lane = lax.broadcasted_iota(jnp.int32, (SUB, c), 1)
    srow = lax.broadcasted_iota(jnp.int32, (SUB, c), 0)
    lane_h = lax.broadcasted_iota(jnp.int32, (SUB // 2, c), 1)
    outs = []
    for h in range(heads):
        sk = slice(h * dk, (h + 1) * dk)
        sv = slice(h * dv, (h + 1) * dv)
        st = st_ref[0, h]
        o = lax.dot_general(qb[:, sk], st.astype(BF16), NT_DIMS, preferred_element_type=F32)
        qh, kh, bh, vh = q[:, sk], k[:, sk], b[:, sk], vb[:, sv]
        blocks = []
        for i in range(nsub):
            r0 = i * SUB
            qi, ki, bi = qh[r0:r0 + SUB], kh[r0:r0 + SUB], bh[r0:r0 + SUB]
            half = SUB // 2
            tops, bots = jnp.zeros((half, c), F32), jnp.zeros((half, c), F32)
            for s in range(SUB):
                ks = kb_ref[0, r0 + s:r0 + s + 1, sk]
                bs = kb_ref[1, r0 + s:r0 + s + 1, sk]
                if s < half:
                    col = jnp.sum(qi[:half] * (ks * jnp.exp2(bi[:half] - bs)), axis=-1, keepdims=True)
                    tops = jnp.where(lane_h == r0 + s, col, tops)
                col = jnp.sum(qi[half:] * (ks * jnp.exp2(bi[half:] - bs)), axis=-1, keepdims=True)
                bots = jnp.where(lane_h == r0 + s, col, bots)
            a = jnp.where(srow >= lane - r0, jnp.concatenate([tops, bots], axis=0), 0.0)
            if i > 0:
                ri = bh[r0:r0 + 1]
                qt = (qi * jnp.exp2(bi - ri)).astype(BF16)
                kt = jnp.concatenate([kh[:r0] * jnp.exp2(ri - bh[:r0]),
                                      jnp.zeros((c - r0, dk), F32)], axis=0).astype(BF16)
                a = a + lax.dot_general(qt, kt, NT_DIMS, preferred_element_type=F32)
            blocks.append(a)
        a_full = jnp.concatenate(blocks, axis=0).astype(BF16)
        o = o + jnp.dot(a_full, vh, preferred_element_type=F32)
        st_ref[0, h] = st * dec_end[:, sk] + lax.dot_general(
            vh, k_end[:, sk], TN_DIMS, preferred_element_type=F32)
        outs.append(o)
    return outs


def _head_norm_gate(outs, ng_ref, gate_ref):
    res = []
    for h, o in enumerate(outs):
        dv = o.shape[-1]
        sv = slice(h * dv, (h + 1) * dv)
        res.append(_rmsnorm_rows(o, ng_ref[:, sv]) * _silu(gate_ref[:, sv].astype(F32)))
    return jnp.concatenate(res, axis=-1)


def _hgrn_body(q_ref, f_ref, v_ref, gate_ref, lb_ref, ng_ref, s0_ref, o_ref, st_ref, kb_ref,
               *, n_prompt_chunks, heads):
    c = pl.program_id(0)

    @pl.when((c == 0) | (c >= n_prompt_chunks))
    def _():
        st_ref[...] = s0_ref[...]

    lb = lb_ref[...]
    k = (1.0 - lb) / (1.0 + jnp.exp(f_ref[...].astype(F32)))
    g = jnp.log1p(-k)
    outs = _gated_chunk(_silu(q_ref[...].astype(F32)), k, g, v_ref[...], st_ref, kb_ref, heads)
    o_ref[...] = _head_norm_gate(outs, ng_ref, gate_ref).astype(o_ref.dtype)


def _gla_body(q_ref, k_ref, v_ref, gate_ref, gl_ref, gkw_ref, gkb_ref, ng_ref, s0_ref, o_ref, st_ref, kb_ref,
              *, n_prompt_chunks, heads):
    c = pl.program_id(0)

    @pl.when((c == 0) | (c >= n_prompt_chunks))
    def _():
        st_ref[...] = s0_ref[...]

    dk = q_ref.shape[1] // heads
    z = jnp.dot(gl_ref[...].astype(BF16), gkw_ref[...], preferred_element_type=F32) + gkb_ref[...]
    g = _log_sigmoid(z) * (1.0 / D_GATE_NORM)
    outs = _gated_chunk(q_ref[...].astype(F32) * dk ** -0.5, k_ref[...].astype(F32), g, v_ref[...], st_ref, kb_ref, heads)
    o_ref[...] = _head_norm_gate(outs, ng_ref, gate_ref).astype(o_ref.dtype)


def _chunk_state_map(n_prompt_chunks, ndim):
    return lambda c: (jnp.maximum(c - (n_prompt_chunks - 1), 0),) + (0,) * (ndim - 1)


def hgrn2(proj, lb, norm_g, s0, *, n_prompt_chunks):
    m = proj.shape[0]
    _, heads, dv, dk = s0.shape
    kw, vw = heads * dk, heads * dv
    smap = _chunk_state_map(n_prompt_chunks, 4)
    row = lambda c: (0, 0)
    return pl.pallas_call(
        functools.partial(_hgrn_body, n_prompt_chunks=n_prompt_chunks, heads=heads),
        grid=(m // CHUNK,),
        in_specs=[pl.BlockSpec((CHUNK, kw), lambda c: (c, 0)),
                  pl.BlockSpec((CHUNK, kw), lambda c: (c, 1)),
                  pl.BlockSpec((CHUNK, vw), lambda c: (c, 2 * kw // vw)),
                  pl.BlockSpec((CHUNK, vw), lambda c: (c, 2 * kw // vw + 1)),
                  pl.BlockSpec((1, kw), row),
                  pl.BlockSpec((1, vw), row),
                  pl.BlockSpec((1, heads, dv, dk), smap)],
        out_specs=[pl.BlockSpec((CHUNK, vw), lambda c: (c, 0)),
                   pl.BlockSpec((1, heads, dv, dk), smap)],
        out_shape=[jax.ShapeDtypeStruct((m, vw), BF16),
                   jax.ShapeDtypeStruct(s0.shape, F32)],
        scratch_shapes=[pltpu.VMEM((2, CHUNK, kw), F32)],
        compiler_params=_cparams(1),
        name="hgrn2",
    )(proj, proj, proj, proj, lb.reshape(1, kw), norm_g.reshape(1, vw), s0)


def gla(proj, cols, gk_w, gk_b, norm_g, s0, *, n_prompt_chunks):
    m = proj.shape[0]
    _, heads, dv, dk = s0.shape
    kw, vw = heads * dk, heads * dv
    cq, ck, cv, cg, cgl = cols
    smap = _chunk_state_map(n_prompt_chunks, 4)
    row = lambda c: (0, 0)
    return pl.pallas_call(
        functools.partial(_gla_body, n_prompt_chunks=n_prompt_chunks, heads=heads),
        grid=(m // CHUNK,),
        in_specs=[pl.BlockSpec((CHUNK, kw), lambda c: (c, cq)),
                  pl.BlockSpec((CHUNK, kw), lambda c: (c, ck)),
                  pl.BlockSpec((CHUNK, vw), lambda c: (c, cv)),
                  pl.BlockSpec((CHUNK, vw), lambda c: (c, cg)),
                  pl.BlockSpec((CHUNK, LANES), lambda c: (c, cgl)),
                  pl.BlockSpec((LANES, kw), row),
                  pl.BlockSpec((1, kw), row),
                  pl.BlockSpec((1, vw), row),
                  pl.BlockSpec((1, heads, dv, dk), smap)],
        out_specs=[pl.BlockSpec((CHUNK, vw), lambda c: (c, 0)),
                   pl.BlockSpec((1, heads, dv, dk), smap)],
        out_shape=[jax.ShapeDtypeStruct((m, vw), BF16),
                   jax.ShapeDtypeStruct(s0.shape, F32)],
        scratch_shapes=[pltpu.VMEM((2, CHUNK, kw), F32)],
        compiler_params=_cparams(1),
        name="gla",
    )(proj, proj, proj, proj, proj, gk_w, gk_b.reshape(1, kw), norm_g.reshape(1, vw), s0)


def _ssd_body(z_ref, x_ref, bc_ref, dt_ref, cw_ref, cb_ref, dtb_ref, alog_ref, dsk_ref, ng_ref,
              cv0_ref, s0_ref, o_ref, cv_ref, st_ref, *, n_prompt_chunks, heads, groups):
    c = pl.program_id(0)

    @pl.when((c == 0) | (c >= n_prompt_chunks))
    def _():
        st_ref[...] = s0_ref[...]
        cv_ref[...] = cv0_ref[...]

    rows, xw_ = x_ref.shape
    hp = xw_ // heads
    n = bc_ref.shape[1] // (2 * groups)
    gw = xw_ // groups
    x_raw, bc_raw = x_ref[...].astype(F32), bc_ref[...].astype(F32)
    prev = cv_ref[...]
    cw, cb = cw_ref[...], cb_ref[...]
    xs = _silu(_causal_conv(x_raw, prev[:, :, :xw_], cw[:, :xw_], cb[:, :xw_], 1, rows))
    bcs = _silu(_causal_conv(bc_raw, prev[:, :, xw_:], cw[:, xw_:], cb[:, xw_:], 1, rows))
    wm1 = prev.shape[1]
    cv_ref[0, :, :xw_] = x_raw[rows - wm1:, :]
    cv_ref[0, :, xw_:] = bc_raw[rows - wm1:, :]

    erow = lax.broadcasted_iota(jnp.int32, (LANES, xw_), 0)
    ecol = lax.broadcasted_iota(jnp.int32, (LANES, xw_), 1)
    expand = jnp.where(ecol // hp == erow, 1.0, 0.0).astype(BF16)
    dt = _softplus(_dot01_right(dt_ref[...].astype(F32), expand) + dtb_ref[...])
    dta = dt * (-jnp.exp(alog_ref[...]))
    cum = _dot01_left(_tril01(rows), dta)
    cum_end = cum[rows - 1:rows, :]
    xdt = xs * dt
    xdt_b = xdt.astype(BF16)
    x_end = (xdt * jnp.exp(cum_end - cum)).astype(BF16)
    e_cum = jnp.exp(cum)
    dec_end = jnp.exp(cum_end)

    prow = lax.broadcasted_iota(jnp.int32, (rows, LANES), 0)
    plane = lax.broadcasted_iota(jnp.int32, (rows, LANES), 1)
    assert hp == rows and LANES % hp == 0
    pair = LANES // hp
    psrc = plane % hp
    y_parts = []
    for g in range(groups):
        bg = bcs[:, g * n:(g + 1) * n].astype(BF16)
        cg = bcs[:, (groups + g) * n:(groups + g + 1) * n].astype(BF16)
        gl = slice(g * gw, (g + 1) * gw)
        st = st_ref[0, :, gl]
        y_inter = jnp.dot(cg, st.astype(BF16), preferred_element_type=F32) * e_cum[:, gl]
        cb_rep = lax.dot_general(cg, jnp.concatenate([bg] * pair, axis=0), NT_DIMS, preferred_element_type=F32)
        y_intra = []
        for sl in range(gw // LANES):
            lo = g * gw + sl * LANES
            cs = cum[:, lo:lo + LANES]
            cdiag = jnp.sum(jnp.where(prow == psrc, cs, 0.0), axis=0, keepdims=True)
            decay = jnp.exp(jnp.minimum(cs - cdiag, 0.0))
            mt = jnp.where(prow >= psrc, cb_rep * decay, 0.0).astype(BF16)
            xp = xdt_b[:, lo:lo + LANES]
            rhs = jnp.concatenate(
                [jnp.where(plane // hp == p, xp, jnp.zeros_like(xp)) for p in range(pair)], axis=0)
            y_intra.append(jnp.dot(mt, rhs, preferred_element_type=F32))
        y_parts.append(jnp.concatenate(y_intra, axis=-1) + y_inter)
        st_ref[0, :, gl] = st * dec_end[:, gl] + lax.dot_general(
            bg, x_end[:, gl], TN_DIMS, preferred_element_type=F32)
    y = jnp.concatenate(y_parts, axis=-1) + dsk_ref[...] * xs
    yz = y * _silu(z_ref[...].astype(F32))
    res = []
    for g in range(groups):
        gl = slice(g * gw, (g + 1) * gw)
        res.append(_rmsnorm_rows(yz[:, gl], ng_ref[:, gl]))
    o_ref[...] = jnp.concatenate(res, axis=-1).astype(o_ref.dtype)


def ssd(proj, cols, conv_w, conv_b, dt_bias_x, a_log_x, d_skip_x, norm_g, cv0, s0, *, n_prompt_chunks):
    m = proj.shape[0]
    xw_ = s0.shape[2]
    n = s0.shape[1]
    bcw = 2 * C_GROUPS * n
    cz, cx, cbc, cdt = cols
    wm1 = cv0.shape[1]
    smap3 = _chunk_state_map(n_prompt_chunks, 3)
    row = lambda c: (0, 0)
    return pl.pallas_call(
        functools.partial(_ssd_body, n_prompt_chunks=n_prompt_chunks, heads=C_HEADS, groups=C_GROUPS),
        grid=(m // CHUNK,),
        in_specs=[pl.BlockSpec((CHUNK, xw_), lambda c: (c, cz)),
                  pl.BlockSpec((CHUNK, xw_), lambda c: (c, cx)),
                  pl.BlockSpec((CHUNK, bcw), lambda c: (c, cbc)),
                  pl.BlockSpec((CHUNK, LANES), lambda c: (c, cdt)),
                  pl.BlockSpec((wm1 + 1, xw_ + bcw), row),
                  pl.BlockSpec((1, xw_ + bcw), row),
                  pl.BlockSpec((1, xw_), row),
                  pl.BlockSpec((1, xw_), row),
                  pl.BlockSpec((1, xw_), row),
                  pl.BlockSpec((1, xw_), row),
                  pl.BlockSpec((1, wm1, xw_ + bcw), smap3),
                  pl.BlockSpec((1, n, xw_), smap3)],
        out_specs=[pl.BlockSpec((CHUNK, xw_), lambda c: (c, 0)),
                   pl.BlockSpec((1, wm1, xw_ + bcw), smap3),
                   pl.BlockSpec((1, n, xw_), smap3)],
        out_shape=[jax.ShapeDtypeStruct((m, xw_), BF16),
                   jax.ShapeDtypeStruct(cv0.shape, F32),
                   jax.ShapeDtypeStruct(s0.shape, F32)],
        compiler_params=_cparams(1),
        name="ssd",
    )(proj, proj, proj, proj, conv_w, conv_b.reshape(1, xw_ + bcw), dt_bias_x.reshape(1, xw_),
      a_log_x.reshape(1, xw_), d_skip_x.reshape(1, xw_), norm_g.reshape(1, xw_), cv0, s0)


def _with_zero_first(s):
    return jnp.concatenate([jnp.zeros((1,) + s.shape[1:], s.dtype), s], axis=0)


def _col_tile_starts(width, tn):
    return [min(j * tn, width - tn) for j in range(-(-width // tn))]


def _block_cols(a, width, tn):
    return jnp.stack([a[..., o:o + tn] for o in _col_tile_starts(width, tn)], axis=0)


def _unblock_cols(b, width, tn):
    return jnp.concatenate([b[j][..., j * tn - o:] for j, o in enumerate(_col_tile_starts(width, tn))], axis=-1)


def kernel(x_prompt, x_sample, mem_prompt, cache_mem_k, cache_mem_v, state_hgrn, state_rglru, state_rg_conv, state_ssd, state_ssd_conv, state_gla, state_ffn_conv, norm_mix, norm_xattn, norm_mem, norm_ffn, ev_w_in, hgrn_lb_logits, hgrn_norm, rg_conv_w, rg_conv_b, rg_w_a, rg_b_a, rg_w_x, rg_b_x, rg_lambda, ev_w_out, od_w_in, ssd_conv_w, ssd_conv_b, ssd_dt_bias, ssd_A_log, ssd_D, ssd_norm, gla_gk_w, gla_gk_b, gla_norm, od_w_out, xa_wq, xa_wk, xa_wv, xa_wo, ffn_up, ffn_conv_w, ffn_conv_b, ffn_down, final_norm):
    bp, p_rows, d = x_prompt.shape
    ns, s_len, _ = x_sample.shape
    assert bp == 1 and s_len == CHUNK and p_rows % CHUNK == 0
    depth = norm_mix.shape[0]
    s_rows = ns * s_len
    m = p_rows + s_rows
    npc = p_rows // CHUNK
    nm = mem_prompt.shape[1]
    tm = math.gcd(math.gcd(p_rows, s_rows), 1024)
    tm_seq = math.gcd(tm, 256)

    x = (x_prompt.reshape(p_rows, d), x_sample.reshape(s_rows, d))
    lb_all = jnp.cumsum(jax.nn.softmax(hgrn_lb_logits.astype(F32), axis=0), axis=0)
    ev_w_in_b, ev_w_out_b, od_w_out_b = ev_w_in.astype(BF16), ev_w_out.astype(BF16), od_w_out.astype(BF16)
    wq_b, wk_b, wv_b, wo_b = xa_wq.astype(BF16), xa_wk.astype(BF16), xa_wv.astype(BF16), xa_wo.astype(BF16)
    w_up_b, w_down_b = ffn_up.astype(BF16), ffn_down.astype(BF16)

    a_kw = hgrn_lb_logits.shape[1]
    a_vw = hgrn_norm.shape[1]
    b_w = rg_lambda.shape[1]
    c_w = ssd_norm.shape[1]
    c_bc = 2 * C_GROUPS * C_STATE
    c_hd = c_w // C_HEADS
    d_val = gla_norm.shape[1]
    d_key = gla_gk_b.shape[1]
    d_rank = gla_gk_w.shape[1]
    ff = ffn_conv_b.shape[1]
    ffn_tn = 512
    in_tn = 1536
    tm_full = tm // 2

    mem_k, mem_v = [], []
    hg, rl, rc, ss, sc, gs, fc_p, fc_s = [], [], [], [], [], [], [], []
    for l in range(depth):
        j = l // 2
        if l % 2 == 0:
            proj = norm_matmul(x, norm_mix[l], ev_w_in_b, j, tm=tm, tn=in_tn, out_dtype=PROJ_DTYPE)
            s0 = _with_zero_first(jnp.swapaxes(state_hgrn[j], -1, -2))
            o_a, st = hgrn2(proj, lb_all[l], hgrn_norm[j], s0, n_prompt_chunks=npc)
            hg.append(jnp.swapaxes(st, -1, -2))
            xcol = (2 * a_kw + 2 * a_vw) // b_w
            o_b, cvp, hp_, cvs, hs_ = rglru(
                proj, xcol, xcol + 1, rg_conv_w[j], rg_conv_b[j], rg_w_a[j], rg_b_a[j], rg_w_x[j], rg_b_x[j],
                rg_lambda[j], state_rg_conv[j], state_rglru[j].reshape(ns, 1, b_w), n_prompt_rows=p_rows, tm=tm_seq)
            rc.append((cvp, cvs))
            rl.append((hp_.reshape(1, b_w), hs_.reshape(ns, b_w)))
            x = matmul_res([(o_a, 0, a_vw, 0), (o_b, 0, b_w, a_vw)], ev_w_out_b, j, x, tm=tm_full, tn=d)
        else:
            w = od_w_in[j]
            offs = np_cumsum_offsets((c_w, c_w + c_bc, C_HEADS, d_key, d_key, d_val, d_val, d_rank))
            o_z, o_xbc, o_dt, o_q, o_k, o_v, o_g, o_gl = offs
            pad = lambda a: jnp.pad(a, ((0, 0), (0, LANES - a.shape[1])))
            w_re = jnp.concatenate([
                w[:, o_z:o_z + c_w], w[:, o_xbc:o_xbc + c_w], w[:, o_xbc + c_w:o_xbc + c_w + c_bc],
                w[:, o_q:o_q + d_key], w[:, o_k:o_k + d_key],
                pad(w[:, o_dt:o_dt + C_HEADS]), pad(w[:, o_gl:o_gl + d_rank]),
                jnp.zeros((d, 2 * LANES), w.dtype),
                w[:, o_v:o_v + d_val], w[:, o_g:o_g + d_val]], axis=1).astype(BF16)
            proj = norm_matmul(x, norm_mix[l], w_re, tm=tm, tn=in_tn, out_dtype=PROJ_DTYPE)
            assert c_w == d_val
            base = 2 * c_w + c_bc
            col_dt = (base + 2 * d_key) // LANES
            cv0 = _with_zero_first(state_ssd_conv[j])
            s0c = _with_zero_first(jnp.transpose(state_ssd[j], (0, 3, 1, 2)).reshape(ns, C_STATE, c_w))
            rep = lambda a: jnp.repeat(a, c_hd)
            o_c, cv, stc = ssd(proj, (0, 1, 2 * c_w // c_bc, col_dt), ssd_conv_w[j], ssd_conv_b[j],
                               rep(ssd_dt_bias[j]), rep(ssd_A_log[j]), rep(ssd_D[j]), ssd_norm[j], cv0, s0c,
                               n_prompt_chunks=npc)
            sc.append(cv)
            ss.append(jnp.transpose(stc.reshape(1 + ns, C_STATE, C_HEADS, c_hd), (0, 2, 3, 1)))
            s0d = _with_zero_first(jnp.swapaxes(state_gla[j], -1, -2))
            gkw = jnp.pad(gla_gk_w[j], ((0, LANES - d_rank), (0, 0))).astype(BF16)
            v_start = base + 2 * d_key + 4 * LANES
            o_d, std = gla(proj, (base // d_key, base // d_key + 1, v_start // d_val, v_start // d_val + 1, col_dt + 1),
                           gkw, gla_gk_b[j], gla_norm[j], s0d, n_prompt_chunks=npc)
            gs.append(jnp.swapaxes(std, -1, -2))
            x = matmul_res([(o_c, 0, c_w, 0), (o_d, 0, d_val, c_w)], od_w_out_b, j, x, tm=tm_full, tn=d)

        mem = mem_prompt.reshape(nm, d)
        k_p = norm_matmul(mem, norm_mem[l], wk_b, l, tm=nm, tn=512, out_dtype=F32)
        v_p = norm_matmul(mem, norm_mem[l], wv_b, l, tm=nm, tn=512, out_dtype=F32)
        mem_k.append(k_p.reshape(1, nm, MEM_HEADS, d // MEM_HEADS))
        mem_v.append(v_p.reshape(1, nm, MEM_HEADS, d // MEM_HEADS))
        q = norm_matmul(x, norm_xattn[l], wq_b, l, tm=tm, tn=d, out_dtype=BF16)
        att = mem_attention(q, k_p, v_p, cache_mem_k, cache_mem_v, l, n_prompt_rows=p_rows, tm=tm_seq)
        x = matmul_res([(att, 0, d, 0)], wo_b, l, x, tm=tm_full, tn=d)

        prev_s = _block_cols(state_ffn_conv[l], ff, ffn_tn)
        act, stp, sts = conv_ffn_up(x, norm_ffn[l], w_up_b, l, ffn_conv_w[l], ffn_conv_b[l],
                                    prev_s, n_prompt_rows=p_rows, tm=tm, tn=ffn_tn)
        fc_p.append(_unblock_cols(stp, ff, ffn_tn))
        fc_s.append(_unblock_cols(sts, ff, ffn_tn))
        starts = _col_tile_starts(ff, ffn_tn)
        body = (len(starts) - 1) * ffn_tn
        down_terms = [(act, 0, body, 0), (act, body + (body - starts[-1]), ff - body, body)]
        x = matmul_res(down_terms, w_down_b, l, x, tm=tm, tn=512)

    y_prompt, y_sample = rmsnorm_rows(x, final_norm, n_prompt_rows=p_rows, tm=tm)
    y_prompt = y_prompt.reshape(1, p_rows, d)
    y_sample = y_sample.reshape(ns, s_len, d)
    stack_p = lambda lst: jnp.stack([a[:1] for a in lst])
    stack_s = lambda lst: jnp.stack([a[1:] for a in lst])
    return (y_prompt, y_sample, jnp.stack(mem_k), jnp.stack(mem_v),
            stack_p(hg), jnp.stack([a[0] for a in rl]), jnp.stack([a[0] for a in rc]),
            stack_p(ss), stack_p(sc), stack_p(gs), jnp.stack(fc_p),
            stack_s(hg), jnp.stack([a[1] for a in rl]), jnp.stack([a[1] for a in rc]),
            stack_s(ss), stack_s(sc), stack_s(gs), jnp.stack(fc_s))


def np_cumsum_offsets(sizes):
    offs, acc = [], 0
    for s in sizes:
        offs.append(acc)
        acc += s
    return offs
```

```python
import functools
import math

import jax
import jax.numpy as jnp
from jax import lax
from jax.experimental import pallas as pl
from jax.experimental.pallas import tpu as pltpu

F32 = jnp.float32
BF16 = jnp.bfloat16
PROJ_DTYPE = F32

CHUNK = 64
CHUNKS_PER_STEP = 4
STEP_ROWS = CHUNK * CHUNKS_PER_STEP
SUB = 16
EPS = 1e-6
LOG2_E = 1.4426950408889634
LANES = 128
SUBLANES = 8
VMEM_LIMIT = 56 * 1024 * 1024

MEM_HEADS = 4
A_HEADS = 8
B_HEADS = 8
C_HEADS = 16
C_GROUPS = 2
C_STATE = 128
D_HEADS = 4
RG_C = 8.0
D_GATE_NORM = 16.0

NT_DIMS = (((1,), (1,)), ((), ()))
TN_DIMS = (((0,), (0,)), ((), ()))


def _cparams(n_axes):
    return pltpu.CompilerParams(dimension_semantics=("arbitrary",) * n_axes,
                                vmem_limit_bytes=VMEM_LIMIT)


def _sigmoid(x):
    return 1.0 / (1.0 + jnp.exp(-x))


def _silu(x):
    return x * _sigmoid(x)


def _log_sigmoid(x):
    return jnp.minimum(x, 0.0) - jnp.log1p(jnp.exp(-jnp.abs(x)))


def _softplus(x):
    return jnp.maximum(x, 0.0) + jnp.log1p(jnp.exp(-jnp.abs(x)))


def _gelu_tanh(x):
    return 0.5 * x * (1.0 + jnp.tanh(math.sqrt(2.0 / math.pi) * (x + 0.044715 * (x * x * x))))


def _split3(x):
    hi = x.astype(BF16)
    r = x - hi.astype(F32)
    mid = r.astype(BF16)
    lo = (r - mid.astype(F32)).astype(BF16)
    return hi, mid, lo


def _dot01_left(t01, x):
    return sum(jnp.dot(t01, p, preferred_element_type=F32) for p in _split3(x))


def _dot01_right(x, e01):
    return sum(jnp.dot(p, e01, preferred_element_type=F32) for p in _split3(x))


def _tril01(n):
    r = lax.broadcasted_iota(jnp.int32, (n, n), 0)
    c = lax.broadcasted_iota(jnp.int32, (n, n), 1)
    return jnp.where(r >= c, 1.0, 0.0).astype(BF16)


def _causal_conv(x, prev, w, b, nseq, seqlen, fresh=None):
    width = w.shape[0]
    ch = x.shape[-1]
    head = 8
    tap = lambda d: w[width - 1 - d:width - d].reshape(1, 1, ch)
    x3 = x.reshape(nseq, seqlen, ch)
    rolled = [pltpu.roll(x, d, axis=0).reshape(nseq, seqlen, ch) for d in range(1, width)]
    y = b + tap(0) * x3
    for d in range(1, width):
        y = y + tap(d) * rolled[d - 1]
    row = lax.broadcasted_iota(jnp.int32, (nseq, head, ch), 1)
    yh = b + tap(0) * x3[:, :head]
    for d in range(1, width):
        xs = rolled[d - 1][:, :head]
        for r in range(d):
            p = width - 1 - d + r
            xs = jnp.where(row == r, prev[:, p:p + 1, :], xs)
        yh = yh + tap(d) * xs
    if fresh is not None:
        yh = jnp.where(fresh, yh, y[:, :head])
    y = jnp.concatenate([yh, y[:, head:]], axis=1)
    return y.reshape(nseq * seqlen, ch)


def _rmsnorm_rows(x, g):
    ms = jnp.mean(x * x, axis=-1, keepdims=True)
    return x * lax.rsqrt(ms + EPS) * g


def _row_specs(rows, tm, width, col=lambda j: 0, single_buffer_pair=False):
    blk = (pl.Element(tm), pl.Element(width))
    row = lambda t: pl.multiple_of(t * tm, tm)
    if not isinstance(rows, tuple):
        return [pl.BlockSpec(blk, lambda i, j: (row(i), col(j)))], rows.shape[0], None
    npt = rows[0].shape[0] // tm
    mode = dict(pipeline_mode=pl.Buffered(1)) if single_buffer_pair else {}
    specs = [pl.BlockSpec(blk, lambda i, j: (row(jnp.minimum(i, npt - 1)), col(j)), **mode),
             pl.BlockSpec(blk, lambda i, j: (row(jnp.maximum(i - npt, 0)), col(j)), **mode)]
    return specs, rows[0].shape[0] + rows[1].shape[0], npt


def _for_row_source(npt, fn, also=True):
    if npt is None and also is True:
        fn(0)
    elif npt is None:
        pl.when(also)(lambda: fn(0))
    else:
        i = pl.program_id(0)
        pl.when(also & (i < npt))(lambda: fn(0))
        pl.when(also & (i >= npt))(lambda: fn(1))


def _norm_matmul_body(*refs, npt):
    n_x = 1 if npt is None else 2
    x_refs = refs[:n_x]
    g_ref, w_ref, o_ref, xn_ref = refs[n_x:]

    def normalize(k):
        xn_ref[...] = _rmsnorm_rows(x_refs[k][...], g_ref[...]).astype(BF16)

    _for_row_source(npt, normalize, also=pl.program_id(1) == 0)
    o_ref[...] = jnp.dot(xn_ref[...], w_ref[...], preferred_element_type=F32).astype(o_ref.dtype)


def _weight_spec(w, layer, k_rows, k_block, tn):
    if layer is None:
        return pl.BlockSpec((k_rows, tn), lambda i, j: (k_block, j))
    return pl.BlockSpec((None, k_rows, tn), lambda i, j: (layer, k_block, j))


def norm_matmul(x, g, w, layer=None, *, tm, tn, out_dtype):
    d, n = w.shape[-2:]
    x_specs, m, npt = _row_specs(x, tm, d, single_buffer_pair=True)
    xs = x if isinstance(x, tuple) else (x,)
    return pl.pallas_call(
        functools.partial(_norm_matmul_body, npt=npt),
        grid=(m // tm, n // tn),
        in_specs=x_specs + [pl.BlockSpec((1, d), lambda i, j: (0, 0)), _weight_spec(w, layer, d, 0, tn)],
        out_specs=pl.BlockSpec((tm, tn), lambda i, j: (i, j)),
        out_shape=jax.ShapeDtypeStruct((m, n), out_dtype),
        scratch_shapes=[pltpu.VMEM((tm, d), BF16)],
        compiler_params=_cparams(2),
        name="norm_matmul",
    )(*xs, g.reshape(1, d), w)


def _matmul_res_body(*refs, arity, npt):
    pos = 0
    terms = []
    for n_a in arity[:-1]:
        terms.append((refs[pos:pos + n_a], refs[pos + n_a]))
        pos += n_a + 1
    r_refs, o_ref = refs[pos:pos + arity[-1]], refs[-1]

    def compute(k):
        acc = r_refs[min(k, len(r_refs) - 1)][...]
        for a_refs, w_ref in terms:
            acc = acc + jnp.dot(a_refs[min(k, len(a_refs) - 1)][...], w_ref[0], preferred_element_type=F32)
        o_ref[...] = acc

    _for_row_source(npt, compute)


def matmul_res(terms, w, layer, res, *, tm, tn):
    n = w.shape[-1]
    el = pl.Element
    in_specs, operands, arity, npts = [], [], [], set()
    for a, a_col, k, w_row in terms:
        specs, m, npt = _row_specs(a, tm, k, col=lambda j, c=a_col: c)
        in_specs += specs + [pl.BlockSpec((el(1), el(k), el(tn)),
                                          lambda i, j, r=w_row: (layer, r, pl.multiple_of(j * tn, tn)))]
        operands += list(a if isinstance(a, tuple) else (a,)) + [w]
        arity.append(len(specs))
        npts.add(npt)
    r_specs, m, npt = _row_specs(res, tm, tn, col=lambda j: pl.multiple_of(j * tn, tn))
    arity.append(len(r_specs))
    npts = (npts | {npt}) - {None}
    assert len(npts) <= 1
    return pl.pallas_call(
        functools.partial(_matmul_res_body, arity=tuple(arity), npt=npts.pop() if npts else None),
        grid=(m // tm, n // tn),
        in_specs=in_specs + r_specs,
        out_specs=pl.BlockSpec((tm, tn), lambda i, j: (i, j)),
        out_shape=jax.ShapeDtypeStruct((m, n), F32),
        compiler_params=_cparams(2),
        name="matmul_res",
    )(*operands, *(res if isinstance(res, tuple) else (res,)))


def _rmsnorm_body(x_ref, g_ref, op_ref, os_ref, *, npt):
    y = _rmsnorm_rows(x_ref[...], g_ref[...])

    @pl.when(pl.program_id(0) < npt)
    def _():
        op_ref[...] = y

    @pl.when(pl.program_id(0) >= npt)
    def _():
        os_ref[...] = y


def rmsnorm_rows(x, g, *, n_prompt_rows, tm):
    m, d = x.shape
    npt = n_prompt_rows // tm
    return pl.pallas_call(
        functools.partial(_rmsnorm_body, npt=npt),
        grid=(m // tm,),
        in_specs=[pl.BlockSpec((tm, d), lambda i: (i, 0)), pl.BlockSpec((1, d), lambda i: (0, 0))],
        out_specs=[pl.BlockSpec((tm, d), lambda i: (jnp.minimum(i, npt - 1), 0)),
                   pl.BlockSpec((tm, d), lambda i: (jnp.maximum(i - npt, 0), 0))],
        out_shape=[jax.ShapeDtypeStruct((n_prompt_rows, d), F32),
                   jax.ShapeDtypeStruct((m - n_prompt_rows, d), F32)],
        compiler_params=_cparams(1),
        name="final_norm",
    )(x, g.reshape(1, d))


def _attn_head(qh, kh, vh):
    s = lax.dot_general(qh, kh, NT_DIMS, preferred_element_type=F32) * qh.shape[-1] ** -0.5
    p = jnp.exp(s - jnp.max(s, axis=-1, keepdims=True))
    p = p / jnp.sum(p, axis=-1, keepdims=True)
    return jnp.dot(p.astype(BF16), vh, preferred_element_type=F32)


def _attn_prompt_body(q_ref, k_ref, v_ref, o_ref, *, heads):
    hd = q_ref.shape[1] // heads
    outs = []
    for h in range(heads):
        sl = slice(h * hd, (h + 1) * hd)
        outs.append(_attn_head(q_ref[:, sl], k_ref[:, sl].astype(BF16), v_ref[:, sl].astype(BF16)))
    o_ref[...] = jnp.concatenate(outs, axis=-1).astype(o_ref.dtype)


def _attn_sample_body(q_ref, k_ref, v_ref, o_ref, *, heads):
    rows = q_ref.shape[0]
    nm, _, hd = k_ref.shape
    k2 = k_ref[...].reshape(nm * heads, hd).astype(BF16)
    v2 = v_ref[...].reshape(nm * heads, hd).astype(BF16)
    qs = jnp.concatenate([q_ref[:, h * hd:(h + 1) * hd] for h in range(heads)], axis=0)
    s = lax.dot_general(qs, k2, NT_DIMS, preferred_element_type=F32) * hd ** -0.5
    q_head = lax.broadcasted_iota(jnp.int32, s.shape, 0) // rows
    k_head = lax.broadcasted_iota(jnp.int32, s.shape, 1) % heads
    own = q_head == k_head
    m = jnp.max(jnp.where(own, s, -jnp.inf), axis=-1, keepdims=True)
    p = jnp.where(own, jnp.exp(s - m), 0.0)
    p = p / jnp.sum(p, axis=-1, keepdims=True)
    o = jnp.dot(p.astype(BF16), v2, preferred_element_type=F32)
    o_ref[...] = jnp.concatenate([o[h * rows:(h + 1) * rows] for h in range(heads)], axis=-1).astype(o_ref.dtype)


def mem_attention(q, k_prompt, v_prompt, cache_k, cache_v, layer, *, n_prompt_rows, tm):
    m, d = q.shape
    nm = k_prompt.shape[0]
    ns, heads, hd = cache_k.shape[1], cache_k.shape[3], cache_k.shape[4]
    npc = n_prompt_rows // CHUNK
    att = pl.pallas_call(
        functools.partial(_attn_prompt_body, heads=heads),
        grid=(n_prompt_rows // tm,),
        in_specs=[pl.BlockSpec((tm, d), lambda i: (i, 0)),
                  pl.BlockSpec((nm, d), lambda i: (0, 0)),
                  pl.BlockSpec((nm, d), lambda i: (0, 0))],
        out_specs=pl.BlockSpec((tm, d), lambda i: (i, 0)),
        out_shape=jax.ShapeDtypeStruct((n_prompt_rows, d), BF16),
        compiler_params=_cparams(1),
        name="mem_attention_prompt",
    )(q, k_prompt, v_prompt)
    cache_spec = pl.BlockSpec((None, None, nm, heads, hd), lambda s: (layer, s, 0, 0, 0))
    att_s = pl.pallas_call(
        functools.partial(_attn_sample_body, heads=heads),
        grid=(ns,),
        in_specs=[pl.BlockSpec((CHUNK, d), lambda s: (npc + s, 0)), cache_spec, cache_spec],
        out_specs=pl.BlockSpec((CHUNK, d), lambda s: (s, 0)),
        out_shape=jax.ShapeDtypeStruct((m - n_prompt_rows, d), BF16),
        compiler_params=_cparams(1),
        name="mem_attention_sample",
    )(q, cache_k, cache_v)
    return att, att_s


def _ffn_up_body(x_ref, g_ref, wg_ref, wv_ref, cw_ref, cb_ref, prev_ref, act_ref, stp_ref, sts_ref,
                 xn_ref, *, n_prompt_tiles, nseq, col_splits):
    i = pl.program_id(0)
    j = pl.program_id(1)
    tn = act_ref.shape[1]
    wm1 = stp_ref.shape[2]
    sw = tn // col_splits

    @pl.when((i == 0) & (j == 0))
    def _():
        stp_ref[...] = jnp.zeros(stp_ref.shape, F32)

    @pl.when(j == 0)
    def _():
        xn_ref[...] = _rmsnorm_rows(x_ref[...], g_ref[...]).astype(BF16)

    xn = xn_ref[...]
    sample = jnp.where(i >= n_prompt_tiles, 1, 0)
    chunk = lax.broadcasted_iota(jnp.int32, (nseq, 1, sw), 0)
    from_state = chunk * 0 + sample == 1
    fresh = chunk * (1 - sample) == 0
    hist = []
    for k in range(col_splits):
        cs = slice(k * sw, (k + 1) * sw)
        gate = jnp.dot(xn, wg_ref[0, :, cs], preferred_element_type=F32)
        val = jnp.dot(xn, wv_ref[0, :, cs], preferred_element_type=F32)
        prev = jnp.where(from_state, prev_ref[0, :, :, cs], stp_ref[j, :, :, cs])
        gc = _causal_conv(gate, prev, cw_ref[:, cs], cb_ref[:, cs], nseq, CHUNK, fresh=fresh)
        act_ref[:, cs] = (_silu(gc) * val).astype(act_ref.dtype)
        hist.append(gate.reshape(nseq, CHUNK, sw)[:, CHUNK - wm1:, :])
    hist = jnp.concatenate(hist, axis=-1)

    @pl.when(i < n_prompt_tiles)
    def _():
        stp_ref[j] = hist[nseq - 1:]

    @pl.when(i >= n_prompt_tiles)
    def _():
        sts_ref[j, pl.ds((i - n_prompt_tiles) * nseq, nseq)] = hist


def conv_ffn_up(x, g, w_up, layer, conv_w, conv_b, prev_s, *, n_prompt_rows, tm, tn):
    m, d = x.shape
    ff = w_up.shape[2] // 2
    nj = -(-ff // tn)
    ns, wm1 = prev_s.shape[1], prev_s.shape[2]
    npt = n_prompt_rows // tm
    nseq = tm // CHUNK
    col = lambda j, base=0: pl.multiple_of(base + jnp.minimum(j * tn, ff - tn), LANES)
    assert ff % LANES == 0 and tn % LANES == 0
    el = pl.Element
    return pl.pallas_call(
        functools.partial(_ffn_up_body, n_prompt_tiles=npt, nseq=nseq, col_splits=2),
        grid=(m // tm, nj),
        in_specs=[pl.BlockSpec((tm, d), lambda i, j: (i, 0)),
                  pl.BlockSpec((1, d), lambda i, j: (0, 0)),
                  pl.BlockSpec((el(1), el(d), el(tn)), lambda i, j: (layer, 0, col(j))),
                  pl.BlockSpec((el(1), el(d), el(tn)), lambda i, j: (layer, 0, col(j, ff))),
                  pl.BlockSpec((el(wm1 + 1), el(tn)), lambda i, j: (0, col(j))),
                  pl.BlockSpec((el(1), el(tn)), lambda i, j: (0, col(j))),
                  pl.BlockSpec((1, nseq, wm1, tn), lambda i, j: (j, jnp.maximum(i - npt, 0), 0, 0))],
        out_specs=[pl.BlockSpec((tm, tn), lambda i, j: (i, j)),
                   pl.BlockSpec((nj, 1, wm1, tn), lambda i, j: (0, 0, 0, 0)),
                   pl.BlockSpec((nj, ns, wm1, tn), lambda i, j: (0, 0, 0, 0))],
        out_shape=[jax.ShapeDtypeStruct((m, nj * tn), BF16),
                   jax.ShapeDtypeStruct((nj, 1, wm1, tn), F32),
                   jax.ShapeDtypeStruct((nj, ns, wm1, tn), F32)],
        scratch_shapes=[pltpu.VMEM((tm, d), BF16)],
        compiler_params=_cparams(2),
        name="ffn_up",
    )(x, g.reshape(1, d), w_up, w_up, conv_w, conv_b.reshape(1, ff), prev_s)


def _rglru_body(x_ref, gate_ref, cw_ref, cb_ref, wa_ref, ba_ref, wx_ref, bx_ref, lam_ref,
                cv0_ref, h0_ref, o_ref, cvp_ref, hp_ref, cvs_ref, hs_ref, *, n_prompt_tiles, nseq, heads):
    i = pl.program_id(0)
    tm, ch = x_ref.shape
    bd = ch // heads
    x = x_ref[...].astype(F32)

    def run(prev, h0, n_seq, seqlen, first_row):
        xc = _causal_conv(x, prev, cw_ref[...], cb_ref[...], n_seq, seqlen)
        ra, rx = [], []
        for h in range(heads):
            xh = xc[:, h * bd:(h + 1) * bd].astype(BF16)
            ra.append(jnp.dot(xh, wa_ref[h], preferred_element_type=F32))
            rx.append(jnp.dot(xh, wx_ref[h], preferred_element_type=F32))
        r = _sigmoid(jnp.concatenate(ra, axis=-1) + ba_ref[...])
        ig = _sigmoid(jnp.concatenate(rx, axis=-1) + bx_ref[...])
        log_a = RG_C * r * _log_sigmoid(lam_ref[...])
        a = jnp.exp(log_a)
        m2 = -jnp.tanh(log_a) * (a * a + 1.0)
        mult = jnp.where(m2 > 0.0, m2 * lax.rsqrt(m2), 0.0)
        row = lax.broadcasted_iota(jnp.int32, (n_seq, seqlen, ch), 1)
        if first_row is not None:
            mult = jnp.where(row.reshape(tm, ch) + first_row == 0, 1.0, mult)
        u = xc * ig * mult
        ng = tm // SUBLANES
        ag = a.reshape(ng, SUBLANES, ch)
        ug = u.reshape(ng, SUBLANES, ch)
        grow = lax.broadcasted_iota(jnp.int32, (ng, SUBLANES, ch), 1)
        d = 1
        while d < SUBLANES:
            live = grow >= d
            a_sh = jnp.where(live, pltpu.roll(ag, d, axis=1), 1.0)
            u_sh = jnp.where(live, pltpu.roll(ug, d, axis=1), 0.0)
            ug = ug + ag * u_sh
            ag = ag * a_sh
            d *= 2
        groups_per_seq = seqlen // SUBLANES
        hs, last = [], []
        for gi in range(ng):
            if gi % groups_per_seq == 0:
                carry = h0[gi // groups_per_seq]
            hg = ug[gi] + ag[gi] * carry
            carry = hg[SUBLANES - 1:]
            hs.append(hg)
            if (gi + 1) % groups_per_seq == 0:
                last.append(carry)
        h_all = jnp.concatenate(hs, axis=0)
        o_ref[...] = (h_all * _gelu_tanh(gate_ref[...].astype(F32))).astype(o_ref.dtype)
        x3 = x.reshape(n_seq, seqlen, ch)
        return x3[:, seqlen - prev.shape[1]:, :], jnp.stack(last, axis=0)

    @pl.when(i == 0)
    def _():
        cvp_ref[...] = jnp.zeros(cvp_ref.shape, F32)
        hp_ref[...] = jnp.zeros(hp_ref.shape, F32)

    @pl.when(i < n_prompt_tiles)
    def _():
        cv, hl = run(cvp_ref[...], hp_ref[...], 1, tm, i * tm)
        cvp_ref[...] = cv
        hp_ref[...] = hl

    @pl.when(i >= n_prompt_tiles)
    def _():
        cv, hl = run(cv0_ref[...], h0_ref[...], nseq, CHUNK, None)
        s0 = (i - n_prompt_tiles) * nseq
        cvs_ref[pl.ds(s0, nseq)] = cv
        hs_ref[pl.ds(s0, nseq)] = hl


def rglru(proj, x_col, gate_col, conv_w, conv_b, w_a, b_a, w_x, b_x, lam, cv0, h0, *, n_prompt_rows, tm):
    m = proj.shape[0]
    heads, bd, _ = w_a.shape
    ch = heads * bd
    ns, wm1 = cv0.shape[0], cv0.shape[1]
    npt = n_prompt_rows // tm
    nseq = tm // CHUNK
    samp = lambda i: (jnp.maximum(i - npt, 0), 0, 0)
    const2 = lambda i: (0, 0)
    const3 = lambda i: (0, 0, 0)
    return pl.pallas_call(
        functools.partial(_rglru_body, n_prompt_tiles=npt, nseq=nseq, heads=heads),
        grid=(m // tm,),
        in_specs=[pl.BlockSpec((tm, ch), lambda i: (i, x_col)),
                  pl.BlockSpec((tm, ch), lambda i: (i, gate_col)),
                  pl.BlockSpec((wm1 + 1, ch), const2),
                  pl.BlockSpec((1, ch), const2),
                  pl.BlockSpec((heads, bd, bd), const3),
                  pl.BlockSpec((1, ch), const2),
                  pl.BlockSpec((heads, bd, bd), const3),
                  pl.BlockSpec((1, ch), const2),
                  pl.BlockSpec((1, ch), const2),
                  pl.BlockSpec((nseq, wm1, ch), samp),
                  pl.BlockSpec((nseq, 1, ch), samp)],
        out_specs=[pl.BlockSpec((tm, ch), lambda i: (i, 0)),
                   pl.BlockSpec((1, wm1, ch), const3),
                   pl.BlockSpec((1, 1, ch), const3),
                   pl.BlockSpec((ns, wm1, ch), const3),
                   pl.BlockSpec((ns, 1, ch), const3)],
        out_shape=[jax.ShapeDtypeStruct((m, ch), BF16),
                   jax.ShapeDtypeStruct((1, wm1, ch), F32),
                   jax.ShapeDtypeStruct((1, 1, ch), F32),
                   jax.ShapeDtypeStruct((ns, wm1, ch), F32),
                   jax.ShapeDtypeStruct((ns, 1, ch), F32)],
        compiler_params=_cparams(1),
        name="rglru",
    )(proj, proj, conv_w, conv_b.reshape(1, ch), w_a.astype(BF16), b_a.reshape(1, ch),
      w_x.astype(BF16), b_x.reshape(1, ch), lam.reshape(1, ch), cv0, h0)


def _gated_chunk(q, k, g, v, st_ref, slot, kb_ref, kb0, heads):
    c = q.shape[0]
    dk = q.shape[1] // heads
    dv = v.shape[1] // heads
    nsub = c // SUB
    b = _dot01_left(_tril01(c), g) * LOG2_E
    b_end = b[c - 1:c, :]
    qb = (q * jnp.exp2(b)).astype(BF16)
    k_end = (k * jnp.exp2(b_end - b)).astype(BF16)
    dec_end = jnp.exp2(b_end)
    vb = v.astype(BF16)
    kb_ref[kb0] = k
    kb_ref[kb0 + 1] = b
    lane = lax.broadcasted_iota(jnp.int32, (SUB, c), 1)
    srow = lax.broadcasted_iota(jnp.int32, (SUB, c), 0)
    lane_h = lax.broadcasted_iota(jnp.int32, (SUB // 2, c), 1)
    outs = []
    for h in range(heads):
        sk = slice(h * dk, (h + 1) * dk)
        sv = slice(h * dv, (h + 1) * dv)
        st = st_ref[slot, h]
        o = lax.dot_general(qb[:, sk], st.astype(BF16), NT_DIMS, preferred_element_type=F32)
        qh, kh, bh, vh = q[:, sk], k[:, sk], b[:, sk], vb[:, sv]
        blocks = []
        for i in range(nsub):
            r0 = i * SUB
            qi, ki, bi = qh[r0:r0 + SUB], kh[r0:r0 + SUB], bh[r0:r0 + SUB]
            half = SUB // 2
            tops, bots = jnp.zeros((half, c), F32), jnp.zeros((half, c), F32)
            for s in range(SUB):
                ks = kb_ref[kb0, r0 + s:r0 + s + 1, sk]
                bs = kb_ref[kb0 + 1, r0 + s:r0 + s + 1, sk]
                if s < half:
                    col = jnp.sum(qi[:half] * (ks * jnp.exp2(bi[:half] - bs)), axis=-1, keepdims=True)
                    tops = jnp.where(lane_h == r0 + s, col, tops)
                col = jnp.sum(qi[half:] * (ks * jnp.exp2(bi[half:] - bs)), axis=-1, keepdims=True)
                bots = jnp.where(lane_h == r0 + s, col, bots)
            a = jnp.where(srow >= lane - r0, jnp.concatenate([tops, bots], axis=0), 0.0)
            if i > 0:
                ri = bh[r0:r0 + 1]
                qt = (qi * jnp.exp2(bi - ri)).astype(BF16)
                kt = jnp.concatenate([kh[:r0] * jnp.exp2(ri - bh[:r0]),
                                      jnp.zeros((c - r0, dk), F32)], axis=0).astype(BF16)
                a = a + lax.dot_general(qt, kt, NT_DIMS, preferred_element_type=F32)
            blocks.append(a)
        a_full = jnp.concatenate(blocks, axis=0).astype(BF16)
        o = o + jnp.dot(a_full, vh, preferred_element_type=F32)
        st_ref[slot, h] = st * dec_end[:, sk] + lax.dot_general(
            vh, k_end[:, sk], TN_DIMS, preferred_element_type=F32)
        outs.append(o)
    return outs


def _head_norm_gate(outs, ng_ref, gate):
    res = []
    for h, o in enumerate(outs):
        dv = o.shape[-1]
        sv = slice(h * dv, (h + 1) * dv)
        res.append(_rmsnorm_rows(o, ng_ref[:, sv]) * _silu(gate[:, sv].astype(F32)))
    return jnp.concatenate(res, axis=-1)


def _chunk_steps(st_ref, s0_ref, n_prompt_steps, one_chunk):
    c = pl.program_id(0)

    @pl.when((c == 0) | (c >= n_prompt_steps))
    def _():
        st_ref[...] = s0_ref[...]

    sample = jnp.where(c >= n_prompt_steps, 1, 0)
    for k in range(CHUNKS_PER_STEP):
        one_chunk(slice(k * CHUNK, (k + 1) * CHUNK), sample * k, k)


def _hgrn_body(q_ref, f_ref, v_ref, gate_ref, lb_ref, ng_ref, s0_ref, o_ref, st_ref, kb_ref,
               *, n_prompt_steps, heads):
    def one_chunk(rows, slot, k_in_step):
        kk = (1.0 - lb_ref[...]) / (1.0 + jnp.exp(f_ref[rows, :].astype(F32)))
        g = jnp.log1p(-kk)
        outs = _gated_chunk(_silu(q_ref[rows, :].astype(F32)), kk, g, v_ref[rows, :], st_ref, slot,
                            kb_ref, 2 * k_in_step, heads)
        o_ref[rows, :] = _head_norm_gate(outs, ng_ref, gate_ref[rows, :]).astype(o_ref.dtype)

    _chunk_steps(st_ref, s0_ref, n_prompt_steps, one_chunk)


def _gla_body(q_ref, k_ref, v_ref, gate_ref, gl_ref, gkw_ref, gkb_ref, ng_ref, s0_ref, o_ref, st_ref, kb_ref,
              *, n_prompt_steps, heads):
    dk = q_ref.shape[1] // heads

    def one_chunk(rows, slot, k_in_step):
        z = jnp.dot(gl_ref[rows, :].astype(BF16), gkw_ref[...], preferred_element_type=F32) + gkb_ref[...]
        g = _log_sigmoid(z) * (1.0 / D_GATE_NORM)
        outs = _gated_chunk(q_ref[rows, :].astype(F32) * dk ** -0.5, k_ref[rows, :].astype(F32), g, v_ref[rows, :],
                            st_ref, slot, kb_ref, 2 * k_in_step, heads)
        o_ref[rows, :] = _head_norm_gate(outs, ng_ref, gate_ref[rows, :]).astype(o_ref.dtype)

    _chunk_steps(st_ref, s0_ref, n_prompt_steps, one_chunk)


def _chunk_state_map(n_prompt_chunks, ndim):
    return lambda c: (jnp.maximum(c - (n_prompt_chunks - 1), 0),) + (0,) * (ndim - 1)


def hgrn2(proj, lb, norm_g, s0, *, n_prompt_chunks):
    m = proj.shape[0]
    _, heads, dv, dk = s0.shape
    kw, vw = heads * dk, heads * dv
    n_prompt_steps = n_prompt_chunks // CHUNKS_PER_STEP
    smap = _chunk_state_map(n_prompt_steps, 4)
    row = lambda c: (0, 0)
    return pl.pallas_call(
        functools.partial(_hgrn_body, n_prompt_steps=n_prompt_steps, heads=heads),
        grid=(m // STEP_ROWS,),
        in_specs=[pl.BlockSpec((STEP_ROWS, kw), lambda c: (c, 0)),
                  pl.BlockSpec((STEP_ROWS, kw), lambda c: (c, 1)),
                  pl.BlockSpec((STEP_ROWS, vw), lambda c: (c, 2 * kw // vw)),
                  pl.BlockSpec((STEP_ROWS, vw), lambda c: (c, 2 * kw // vw + 1)),
                  pl.BlockSpec((1, kw), row),
                  pl.BlockSpec((1, vw), row),
                  pl.BlockSpec((CHUNKS_PER_STEP, heads, dv, dk), smap)],
        out_specs=[pl.BlockSpec((STEP_ROWS, vw), lambda c: (c, 0)),
                   pl.BlockSpec((CHUNKS_PER_STEP, heads, dv, dk), smap)],
        out_shape=[jax.ShapeDtypeStruct((m, vw), BF16),
                   jax.ShapeDtypeStruct(s0.shape, F32)],
        scratch_shapes=[pltpu.VMEM((2 * CHUNKS_PER_STEP, CHUNK, kw), F32)],
        compiler_params=_cparams(1),
        name="hgrn2",
    )(proj, proj, proj, proj, lb.reshape(1, kw), norm_g.reshape(1, vw), s0)


def gla(proj, cols, gk_w, gk_b, norm_g, s0, *, n_prompt_chunks):
    m = proj.shape[0]
    _, heads, dv, dk = s0.shape
    kw, vw = heads * dk, heads * dv
    cq, ck, cv, cg, cgl = cols
    n_prompt_steps = n_prompt_chunks // CHUNKS_PER_STEP
    smap = _chunk_state_map(n_prompt_steps, 4)
    row = lambda c: (0, 0)
    return pl.pallas_call(
        functools.partial(_gla_body, n_prompt_steps=n_prompt_steps, heads=heads),
        grid=(m // STEP_ROWS,),
        in_specs=[pl.BlockSpec((STEP_ROWS, kw), lambda c: (c, cq)),
                  pl.BlockSpec((STEP_ROWS, kw), lambda c: (c, ck)),
                  pl.BlockSpec((STEP_ROWS, vw), lambda c: (c, cv)),
                  pl.BlockSpec((STEP_ROWS, vw), lambda c: (c, cg)),
                  pl.BlockSpec((STEP_ROWS, LANES), lambda c: (c, cgl)),
                  pl.BlockSpec((LANES, kw), row),
                  pl.BlockSpec((1, kw), row),
                  pl.BlockSpec((1, vw), row),
                  pl.BlockSpec((CHUNKS_PER_STEP, heads, dv, dk), smap)],
        out_specs=[pl.BlockSpec((STEP_ROWS, vw), lambda c: (c, 0)),
                   pl.BlockSpec((CHUNKS_PER_STEP, heads, dv, dk), smap)],
        out_shape=[jax.ShapeDtypeStruct((m, vw), BF16),
                   jax.ShapeDtypeStruct(s0.shape, F32)],
        scratch_shapes=[pltpu.VMEM((2 * CHUNKS_PER_STEP, CHUNK, kw), F32)],
        compiler_params=_cparams(1),
        name="gla",
    )(proj, proj, proj, proj, proj, gk_w, gk_b.reshape(1, kw), norm_g.reshape(1, vw), s0)


def _ssd_body(z_ref, x_ref, bc_ref, dt_ref, cw_ref, cb_ref, dtb_ref, alog_ref, dsk_ref, ng_ref,
              cv0_ref, s0_ref, o_ref, cv_ref, st_ref, *, n_prompt_chunks, heads, groups):
    c = pl.program_id(0)

    @pl.when((c == 0) | (c >= n_prompt_chunks))
    def _():
        st_ref[...] = s0_ref[...]
        cv_ref[...] = cv0_ref[...]

    rows, xw_ = x_ref.shape
    hp = xw_ // heads
    n = bc_ref.shape[1] // (2 * groups)
    gw = xw_ // groups
    x_raw, bc_raw = x_ref[...].astype(F32), bc_ref[...].astype(F32)
    prev = cv_ref[...]
    cw, cb = cw_ref[...], cb_ref[...]
    xs = _silu(_causal_conv(x_raw, prev[:, :, :xw_], cw[:, :xw_], cb[:, :xw_], 1, rows))
    bcs = _silu(_causal_conv(bc_raw, prev[:, :, xw_:], cw[:, xw_:], cb[:, xw_:], 1, rows))
    wm1 = prev.shape[1]
    cv_ref[0, :, :xw_] = x_raw[rows - wm1:, :]
    cv_ref[0, :, xw_:] = bc_raw[rows - wm1:, :]

    erow = lax.broadcasted_iota(jnp.int32, (LANES, xw_), 0)
    ecol = lax.broadcasted_iota(jnp.int32, (LANES, xw_), 1)
    expand = jnp.where(ecol // hp == erow, 1.0, 0.0).astype(BF16)
    dt = _softplus(_dot01_right(dt_ref[...].astype(F32), expand) + dtb_ref[...])
    dta = dt * (-jnp.exp(alog_ref[...]))
    cum = _dot01_left(_tril01(rows), dta)
    cum_end = cum[rows - 1:rows, :]
    xdt = xs * dt
    xdt_b = xdt.astype(BF16)
    x_end = (xdt * jnp.exp(cum_end - cum)).astype(BF16)
    e_cum = jnp.exp(cum)
    dec_end = jnp.exp(cum_end)

    prow = lax.broadcasted_iota(jnp.int32, (rows, LANES), 0)
    plane = lax.broadcasted_iota(jnp.int32, (rows, LANES), 1)
    assert hp == rows and LANES % hp == 0
    pair = LANES // hp
    psrc = plane % hp
    y_parts = []
    for g in range(groups):
        bg = bcs[:, g * n:(g + 1) * n].astype(BF16)
        cg = bcs[:, (groups + g) * n:(groups + g + 1) * n].astype(BF16)
        gl = slice(g * gw, (g + 1) * gw)
        st = st_ref[0, :, gl]
        y_inter = jnp.dot(cg, st.astype(BF16), preferred_element_type=F32) * e_cum[:, gl]
        cb_rep = lax.dot_general(cg, jnp.concatenate([bg] * pair, axis=0), NT_DIMS, preferred_element_type=F32)
        y_intra = []
        for sl in range(gw // LANES):
            lo = g * gw + sl * LANES
            cs = cum[:, lo:lo + LANES]
            cdiag = jnp.sum(jnp.where(prow == psrc, cs, 0.0), axis=0, keepdims=True)
            decay = jnp.exp(jnp.minimum(cs - cdiag, 0.0))
            mt = jnp.where(prow >= psrc, cb_rep * decay, 0.0).astype(BF16)
            xp = xdt_b[:, lo:lo + LANES]
            rhs = jnp.concatenate(
                [jnp.where(plane // hp == p, xp, jnp.zeros_like(xp)) for p in range(pair)], axis=0)
            y_intra.append(jnp.dot(mt, rhs, preferred_element_type=F32))
        y_parts.append(jnp.concatenate(y_intra, axis=-1) + y_inter)
        st_ref[0, :, gl] = st * dec_end[:, gl] + lax.dot_general(
            bg, x_end[:, gl], TN_DIMS, preferred_element_type=F32)
    y = jnp.concatenate(y_parts, axis=-1) + dsk_ref[...] * xs
    yz = y * _silu(z_ref[...].astype(F32))
    res = []
    for g in range(groups):
        gl = slice(g * gw, (g + 1) * gw)
        res.append(_rmsnorm_rows(yz[:, gl], ng_ref[:, gl]))
    o_ref[...] = jnp.concatenate(res, axis=-1).astype(o_ref.dtype)


def ssd(proj, cols, conv_w, conv_b, dt_bias_x, a_log_x, d_skip_x, norm_g, cv0, s0, *, n_prompt_chunks):
    m = proj.shape[0]
    xw_ = s0.shape[2]
    n = s0.shape[1]
    bcw = 2 * C_GROUPS * n
    cz, cx, cbc, cdt = cols
    wm1 = cv0.shape[1]
    smap3 = _chunk_state_map(n_prompt_chunks, 3)
    row = lambda c: (0, 0)
    return pl.pallas_call(
        functools.partial(_ssd_body, n_prompt_chunks=n_prompt_chunks, heads=C_HEADS, groups=C_GROUPS),
        grid=(m // CHUNK,),
        in_specs=[pl.BlockSpec((CHUNK, xw_), lambda c: (c, cz)),
                  pl.BlockSpec((CHUNK, xw_), lambda c: (c, cx)),
                  pl.BlockSpec((CHUNK, bcw), lambda c: (c, cbc)),
                  pl.BlockSpec((CHUNK, LANES), lambda c: (c, cdt)),
                  pl.BlockSpec((wm1 + 1, xw_ + bcw), row),
                  pl.BlockSpec((1, xw_ + bcw), row),
                  pl.BlockSpec((1, xw_), row),
                  pl.BlockSpec((1, xw_), row),
                  pl.BlockSpec((1, xw_), row),
                  pl.BlockSpec((1, xw_), row),
                  pl.BlockSpec((1, wm1, xw_ + bcw), smap3),
                  pl.BlockSpec((1, n, xw_), smap3)],
        out_specs=[pl.BlockSpec((CHUNK, xw_), lambda c: (c, 0)),
                   pl.BlockSpec((1, wm1, xw_ + bcw), smap3),
                   pl.BlockSpec((1, n, xw_), smap3)],
        out_shape=[jax.ShapeDtypeStruct((m, xw_), BF16),
                   jax.ShapeDtypeStruct(cv0.shape, F32),
                   jax.ShapeDtypeStruct(s0.shape, F32)],
        compiler_params=_cparams(1),
        name="ssd",
    )(proj, proj, proj, proj, conv_w, conv_b.reshape(1, xw_ + bcw), dt_bias_x.reshape(1, xw_),
      a_log_x.reshape(1, xw_), d_skip_x.reshape(1, xw_), norm_g.reshape(1, xw_), cv0, s0)


def _with_zero_first(s, n=1):
    return jnp.concatenate([jnp.zeros((n,) + s.shape[1:], s.dtype), s], axis=0)


def _drop_unused_prompt_slots(st):
    return jnp.concatenate([st[:1], st[CHUNKS_PER_STEP:]], axis=0)


def _col_tile_starts(width, tn):
    return [min(j * tn, width - tn) for j in range(-(-width // tn))]


def _block_cols(a, width, tn):
    return jnp.stack([a[..., o:o + tn] for o in _col_tile_starts(width, tn)], axis=0)


def _unblock_cols(b, width, tn):
    return jnp.concatenate([b[j][..., j * tn - o:] for j, o in enumerate(_col_tile_starts(width, tn))], axis=-1)


def kernel(x_prompt, x_sample, mem_prompt, cache_mem_k, cache_mem_v, state_hgrn, state_rglru, state_rg_conv, state_ssd, state_ssd_conv, state_gla, state_ffn_conv, norm_mix, norm_xattn, norm_mem, norm_ffn, ev_w_in, hgrn_lb_logits, hgrn_norm, rg_conv_w, rg_conv_b, rg_w_a, rg_b_a, rg_w_x, rg_b_x, rg_lambda, ev_w_out, od_w_in, ssd_conv_w, ssd_conv_b, ssd_dt_bias, ssd_A_log, ssd_D, ssd_norm, gla_gk_w, gla_gk_b, gla_norm, od_w_out, xa_wq, xa_wk, xa_wv, xa_wo, ffn_up, ffn_conv_w, ffn_conv_b, ffn_down, final_norm):
    bp, p_rows, d = x_prompt.shape
    ns, s_len, _ = x_sample.shape
    assert bp == 1 and s_len == CHUNK and p_rows % STEP_ROWS == 0 and ns % CHUNKS_PER_STEP == 0
    depth = norm_mix.shape[0]
    s_rows = ns * s_len
    m = p_rows + s_rows
    npc = p_rows // CHUNK
    nm = mem_prompt.shape[1]
    tm = math.gcd(math.gcd(p_rows, s_rows), 1024)
    tm_seq = math.gcd(tm, 256)

    x = (x_prompt.reshape(p_rows, d), x_sample.reshape(s_rows, d))
    lb_all = jnp.cumsum(jax.nn.softmax(hgrn_lb_logits.astype(F32), axis=0), axis=0)
    ev_w_in_b, ev_w_out_b, od_w_out_b = ev_w_in.astype(BF16), ev_w_out.astype(BF16), od_w_out.astype(BF16)
    wq_b, wk_b, wv_b, wo_b = xa_wq.astype(BF16), xa_wk.astype(BF16), xa_wv.astype(BF16), xa_wo.astype(BF16)
    w_up_b, w_down_b = ffn_up.astype(BF16), ffn_down.astype(BF16)

    a_kw = hgrn_lb_logits.shape[1]
    a_vw = hgrn_norm.shape[1]
    b_w = rg_lambda.shape[1]
    c_w = ssd_norm.shape[1]
    c_bc = 2 * C_GROUPS * C_STATE
    c_hd = c_w // C_HEADS
    d_val = gla_norm.shape[1]
    d_key = gla_gk_b.shape[1]
    d_rank = gla_gk_w.shape[1]
    ff = ffn_conv_b.shape[1]
    ffn_tn = 512
    in_tn = 1536
    tm_full = tm // 2

    mem_k, mem_v = [], []
    hg, rl, rc, ss, sc, gs, fc_p, fc_s = [], [], [], [], [], [], [], []
    for l in range(depth):
        j = l // 2
        if l % 2 == 0:
            proj = norm_matmul(x, norm_mix[l], ev_w_in_b, j, tm=tm, tn=in_tn, out_dtype=PROJ_DTYPE)
            s0 = _with_zero_first(jnp.swapaxes(state_hgrn[j], -1, -2), CHUNKS_PER_STEP)
            o_a, st = hgrn2(proj, lb_all[l], hgrn_norm[j], s0, n_prompt_chunks=npc)
            hg.append(jnp.swapaxes(_drop_unused_prompt_slots(st), -1, -2))
            xcol = (2 * a_kw + 2 * a_vw) // b_w
            o_b, cvp, hp_, cvs, hs_ = rglru(
                proj, xcol, xcol + 1, rg_conv_w[j], rg_conv_b[j], rg_w_a[j], rg_b_a[j], rg_w_x[j], rg_b_x[j],
                rg_lambda[j], state_rg_conv[j], state_rglru[j].reshape(ns, 1, b_w), n_prompt_rows=p_rows, tm=tm_seq)
            rc.append((cvp, cvs))
            rl.append((hp_.reshape(1, b_w), hs_.reshape(ns, b_w)))
            x = matmul_res([(o_a, 0, a_vw, 0), (o_b, 0, b_w, a_vw)], ev_w_out_b, j, x, tm=tm_full, tn=d)
        else:
            w = od_w_in[j]
            offs = np_cumsum_offsets((c_w, c_w + c_bc, C_HEADS, d_key, d_key, d_val, d_val, d_rank))
            o_z, o_xbc, o_dt, o_q, o_k, o_v, o_g, o_gl = offs
            pad = lambda a: jnp.pad(a, ((0, 0), (0, LANES - a.shape[1])))
            w_re = jnp.concatenate([
                w[:, o_z:o_z + c_w], w[:, o_xbc:o_xbc + c_w], w[:, o_xbc + c_w:o_xbc + c_w + c_bc],
                w[:, o_q:o_q + d_key], w[:, o_k:o_k + d_key],
                pad(w[:, o_dt:o_dt + C_HEADS]), pad(w[:, o_gl:o_gl + d_rank]),
                jnp.zeros((d, 2 * LANES), w.dtype),
                w[:, o_v:o_v + d_val], w[:, o_g:o_g + d_val]], axis=1).astype(BF16)
            proj = norm_matmul(x, norm_mix[l], w_re, tm=tm, tn=in_tn, out_dtype=PROJ_DTYPE)
            assert c_w == d_val
            base = 2 * c_w + c_bc
            col_dt = (base + 2 * d_key) // LANES
            cv0 = _with_zero_first(state_ssd_conv[j])
            s0c = _with_zero_first(jnp.transpose(state_ssd[j], (0, 3, 1, 2)).reshape(ns, C_STATE, c_w))
            rep = lambda a: jnp.repeat(a, c_hd)
            o_c, cv, stc = ssd(proj, (0, 1, 2 * c_w // c_bc, col_dt), ssd_conv_w[j], ssd_conv_b[j],
                               rep(ssd_dt_bias[j]), rep(ssd_A_log[j]), rep(ssd_D[j]), ssd_norm[j], cv0, s0c,
                               n_prompt_chunks=npc)
            sc.append(cv)
            ss.append(jnp.transpose(stc.reshape(1 + ns, C_STATE, C_HEADS, c_hd), (0, 2, 3, 1)))
            s0d = _with_zero_first(jnp.swapaxes(state_gla[j], -1, -2), CHUNKS_PER_STEP)
            gkw = jnp.pad(gla_gk_w[j], ((0, LANES - d_rank), (0, 0))).astype(BF16)
            v_start = base + 2 * d_key + 4 * LANES
            o_d, std = gla(proj, (base // d_key, base // d_key + 1, v_start // d_val, v_start // d_val + 1, col_dt + 1),
                           gkw, gla_gk_b[j], gla_norm[j], s0d, n_prompt_chunks=npc)
            gs.append(jnp.swapaxes(_drop_unused_prompt_slots(std), -1, -2))
            x = matmul_res([(o_c, 0, c_w, 0), (o_d, 0, d_val, c_w)], od_w_out_b, j, x, tm=tm_full, tn=d)

        mem = mem_prompt.reshape(nm, d)
        k_p = norm_matmul(mem, norm_mem[l], wk_b, l, tm=nm, tn=512, out_dtype=F32)
        v_p = norm_matmul(mem, norm_mem[l], wv_b, l, tm=nm, tn=512, out_dtype=F32)
        mem_k.append(k_p.reshape(1, nm, MEM_HEADS, d // MEM_HEADS))
        mem_v.append(v_p.reshape(1, nm, MEM_HEADS, d // MEM_HEADS))
        q = norm_matmul(x, norm_xattn[l], wq_b, l, tm=tm, tn=d, out_dtype=BF16)
        att = mem_attention(q, k_p, v_p, cache_mem_k, cache_mem_v, l, n_prompt_rows=p_rows, tm=tm)
        x = matmul_res([(att, 0, d, 0)], wo_b, l, x, tm=tm_full, tn=d)

        prev_s = _block_cols(state_ffn_conv[l], ff, ffn_tn)
        act, stp, sts = conv_ffn_up(x, norm_ffn[l], w_up_b, l, ffn_conv_w[l], ffn_conv_b[l],
                                    prev_s, n_prompt_rows=p_rows, tm=tm, tn=ffn_tn)
        fc_p.append(_unblock_cols(stp, ff, ffn_tn))
        fc_s.append(_unblock_cols(sts, ff, ffn_tn))
        starts = _col_tile_starts(ff, ffn_tn)
        body = (len(starts) - 1) * ffn_tn
        down_terms = [(act, 0, body, 0), (act, body + (body - starts[-1]), ff - body, body)]
        x = matmul_res(down_terms, w_down_b, l, x, tm=tm, tn=512)

    y_prompt, y_sample = rmsnorm_rows(x, final_norm, n_prompt_rows=p_rows, tm=tm)
    y_prompt = y_prompt.reshape(1, p_rows, d)
    y_sample = y_sample.reshape(ns, s_len, d)
    stack_p = lambda lst: jnp.stack([a[:1] for a in lst])
    stack_s = lambda lst: jnp.stack([a[1:] for a in lst])
    return (y_prompt, y_sample, jnp.stack(mem_k), jnp.stack(mem_v),
            stack_p(hg), jnp.stack([a[0] for a in rl]), jnp.stack([a[0] for a in rc]),
            stack_p(ss), stack_p(sc), stack_p(gs), jnp.stack(fc_p),
            stack_s(hg), jnp.stack([a[1] for a in rl]), jnp.stack([a[1] for a in rc]),
            stack_s(ss), stack_s(sc), stack_s(gs), jnp.stack(fc_s))


def np_cumsum_offsets(sizes):
    offs, acc = [], 0
    for s in sizes:
        offs.append(acc)
        acc += s
    return offs
```

```python
import functools
import math

import jax
import jax.numpy as jnp
from jax import lax
from jax.experimental import pallas as pl
from jax.experimental.pallas import tpu as pltpu

F32 = jnp.float32
BF16 = jnp.bfloat16
PROJ_DTYPE = F32

CHUNK = 64
CHUNKS_PER_STEP = 4
STEP_ROWS = CHUNK * CHUNKS_PER_STEP
SUB = 16
EPS = 1e-6
LOG2_E = 1.4426950408889634
LANES = 128
SUBLANES = 8
VMEM_LIMIT = 56 * 1024 * 1024

MEM_HEADS = 4
A_HEADS = 8
B_HEADS = 8
C_HEADS = 16
C_GROUPS = 2
C_STATE = 128
D_HEADS = 4
RG_C = 8.0
D_GATE_NORM = 16.0

NT_DIMS = (((1,), (1,)), ((), ()))
TN_DIMS = (((0,), (0,)), ((), ()))


def _cparams(n_axes):
    return pltpu.CompilerParams(dimension_semantics=("arbitrary",) * n_axes,
                                vmem_limit_bytes=VMEM_LIMIT)


def _sigmoid(x):
    return 1.0 / (1.0 + jnp.exp(-x))


def _silu(x):
    return x * _sigmoid(x)


def _log_sigmoid(x):
    return jnp.minimum(x, 0.0) - jnp.log1p(jnp.exp(-jnp.abs(x)))


def _softplus(x):
    return jnp.maximum(x, 0.0) + jnp.log1p(jnp.exp(-jnp.abs(x)))


def _gelu_tanh(x):
    return 0.5 * x * (1.0 + jnp.tanh(math.sqrt(2.0 / math.pi) * (x + 0.044715 * (x * x * x))))


def _split3(x):
    hi = x.astype(BF16)
    r = x - hi.astype(F32)
    mid = r.astype(BF16)
    lo = (r - mid.astype(F32)).astype(BF16)
    return hi, mid, lo


def _dot01_left(t01, x):
    return sum(jnp.dot(t01, p, preferred_element_type=F32) for p in _split3(x))


def _dot01_right(x, e01):
    return sum(jnp.dot(p, e01, preferred_element_type=F32) for p in _split3(x))


def _tril01(n):
    r = lax.broadcasted_iota(jnp.int32, (n, n), 0)
    c = lax.broadcasted_iota(jnp.int32, (n, n), 1)
    return jnp.where(r >= c, 1.0, 0.0).astype(BF16)


def _causal_conv(x, prev, w, b, nseq, seqlen, fresh=None):
    width = w.shape[0]
    ch = x.shape[-1]
    head = 8
    tap = lambda d: w[width - 1 - d:width - d].reshape(1, 1, ch)
    x3 = x.reshape(nseq, seqlen, ch)
    rolled = [pltpu.roll(x, d, axis=0).reshape(nseq, seqlen, ch) for d in range(1, width)]
    y = b + tap(0) * x3
    for d in range(1, width):
        y = y + tap(d) * rolled[d - 1]
    row = lax.broadcasted_iota(jnp.int32, (nseq, head, ch), 1)
    yh = b + tap(0) * x3[:, :head]
    for d in range(1, width):
        xs = rolled[d - 1][:, :head]
        for r in range(d):
            p = width - 1 - d + r
            xs = jnp.where(row == r, prev[:, p:p + 1, :], xs)
        yh = yh + tap(d) * xs
    if fresh is not None:
        yh = jnp.where(fresh, yh, y[:, :head])
    y = jnp.concatenate([yh, y[:, head:]], axis=1)
    return y.reshape(nseq * seqlen, ch)


def _rmsnorm_rows(x, g):
    ms = jnp.mean(x * x, axis=-1, keepdims=True)
    return x * lax.rsqrt(ms + EPS) * g


def _row_specs(rows, tm, width, col=lambda j: 0, single_buffer_pair=False):
    blk = (pl.Element(tm), pl.Element(width))
    row = lambda t: pl.multiple_of(t * tm, tm)
    if not isinstance(rows, tuple):
        return [pl.BlockSpec(blk, lambda i, j: (row(i), col(j)))], rows.shape[0], None
    npt = rows[0].shape[0] // tm
    mode = dict(pipeline_mode=pl.Buffered(1)) if single_buffer_pair else {}
    specs = [pl.BlockSpec(blk, lambda i, j: (row(jnp.minimum(i, npt - 1)), col(j)), **mode),
             pl.BlockSpec(blk, lambda i, j: (row(jnp.maximum(i - npt, 0)), col(j)), **mode)]
    return specs, rows[0].shape[0] + rows[1].shape[0], npt


def _for_row_source(npt, fn, also=True):
    if npt is None and also is True:
        fn(0)
    elif npt is None:
        pl.when(also)(lambda: fn(0))
    else:
        i = pl.program_id(0)
        pl.when(also & (i < npt))(lambda: fn(0))
        pl.when(also & (i >= npt))(lambda: fn(1))


def _norm_matmul_body(*refs, npt):
    n_x = 1 if npt is None else 2
    x_refs = refs[:n_x]
    g_ref, w_ref, o_ref, xn_ref = refs[n_x:]

    def normalize(k):
        xn_ref[...] = _rmsnorm_rows(x_refs[k][...], g_ref[...]).astype(BF16)

    _for_row_source(npt, normalize, also=pl.program_id(1) == 0)
    o_ref[...] = jnp.dot(xn_ref[...], w_ref[...], preferred_element_type=F32).astype(o_ref.dtype)


def _weight_spec(w, layer, k_rows, k_block, tn):
    if layer is None:
        return pl.BlockSpec((k_rows, tn), lambda i, j: (k_block, j))
    return pl.BlockSpec((None, k_rows, tn), lambda i, j: (layer, k_block, j))


def norm_matmul(x, g, w, layer=None, *, tm, tn, out_dtype):
    d, n = w.shape[-2:]
    x_specs, m, npt = _row_specs(x, tm, d, single_buffer_pair=True)
    xs = x if isinstance(x, tuple) else (x,)
    return pl.pallas_call(
        functools.partial(_norm_matmul_body, npt=npt),
        grid=(m // tm, n // tn),
        in_specs=x_specs + [pl.BlockSpec((1, d), lambda i, j: (0, 0)), _weight_spec(w, layer, d, 0, tn)],
        out_specs=pl.BlockSpec((tm, tn), lambda i, j: (i, j)),
        out_shape=jax.ShapeDtypeStruct((m, n), out_dtype),
        scratch_shapes=[pltpu.VMEM((tm, d), BF16)],
        compiler_params=_cparams(2),
        name="norm_matmul",
    )(*xs, g.reshape(1, d), w)


def _matmul_res_body(*refs, arity, npt):
    pos = 0
    terms = []
    for n_a in arity[:-1]:
        terms.append((refs[pos:pos + n_a], refs[pos + n_a]))
        pos += n_a + 1
    r_refs, o_ref = refs[pos:pos + arity[-1]], refs[-1]

    def compute(k):
        acc = r_refs[min(k, len(r_refs) - 1)][...]
        for a_refs, w_ref in terms:
            acc = acc + jnp.dot(a_refs[min(k, len(a_refs) - 1)][...], w_ref[0], preferred_element_type=F32)
        o_ref[...] = acc

    _for_row_source(npt, compute)


def matmul_res(terms, w, layer, res, *, tm, tn):
    n = w.shape[-1]
    el = pl.Element
    in_specs, operands, arity, npts = [], [], [], set()
    for a, a_col, k, w_row in terms:
        specs, m, npt = _row_specs(a, tm, k, col=lambda j, c=a_col: c)
        in_specs += specs + [pl.BlockSpec((el(1), el(k), el(tn)),
                                          lambda i, j, r=w_row: (layer, r, pl.multiple_of(j * tn, tn)))]
        operands += list(a if isinstance(a, tuple) else (a,)) + [w]
        arity.append(len(specs))
        npts.add(npt)
    r_specs, m, npt = _row_specs(res, tm, tn, col=lambda j: pl.multiple_of(j * tn, tn))
    arity.append(len(r_specs))
    npts = (npts | {npt}) - {None}
    assert len(npts) <= 1
    return pl.pallas_call(
        functools.partial(_matmul_res_body, arity=tuple(arity), npt=npts.pop() if npts else None),
        grid=(m // tm, n // tn),
        in_specs=in_specs + r_specs,
        out_specs=pl.BlockSpec((tm, tn), lambda i, j: (i, j)),
        out_shape=jax.ShapeDtypeStruct((m, n), F32),
        compiler_params=_cparams(2),
        name="matmul_res",
    )(*operands, *(res if isinstance(res, tuple) else (res,)))


def _rmsnorm_body(x_ref, g_ref, op_ref, os_ref, *, npt):
    y = _rmsnorm_rows(x_ref[...], g_ref[...])

    @pl.when(pl.program_id(0) < npt)
    def _():
        op_ref[...] = y

    @pl.when(pl.program_id(0) >= npt)
    def _():
        os_ref[...] = y


def rmsnorm_rows(x, g, *, n_prompt_rows, tm):
    m, d = x.shape
    npt = n_prompt_rows // tm
    return pl.pallas_call(
        functools.partial(_rmsnorm_body, npt=npt),
        grid=(m // tm,),
        in_specs=[pl.BlockSpec((tm, d), lambda i: (i, 0)), pl.BlockSpec((1, d), lambda i: (0, 0))],
        out_specs=[pl.BlockSpec((tm, d), lambda i: (jnp.minimum(i, npt - 1), 0)),
                   pl.BlockSpec((tm, d), lambda i: (jnp.maximum(i - npt, 0), 0))],
        out_shape=[jax.ShapeDtypeStruct((n_prompt_rows, d), F32),
                   jax.ShapeDtypeStruct((m - n_prompt_rows, d), F32)],
        compiler_params=_cparams(1),
        name="final_norm",
    )(x, g.reshape(1, d))


def _attn_head(qh, kh, vh):
    s = lax.dot_general(qh, kh, NT_DIMS, preferred_element_type=F32) * qh.shape[-1] ** -0.5
    p = jnp.exp(s - jnp.max(s, axis=-1, keepdims=True))
    p = p / jnp.sum(p, axis=-1, keepdims=True)
    return jnp.dot(p.astype(BF16), vh, preferred_element_type=F32)


def _attn_prompt_body(q_ref, k_ref, v_ref, o_ref, *, heads):
    hd = q_ref.shape[1] // heads
    outs = []
    for h in range(heads):
        sl = slice(h * hd, (h + 1) * hd)
        outs.append(_attn_head(q_ref[:, sl], k_ref[:, sl].astype(BF16), v_ref[:, sl].astype(BF16)))
    o_ref[...] = jnp.concatenate(outs, axis=-1).astype(o_ref.dtype)


def _attn_sample_body(q_ref, k_ref, v_ref, o_ref, *, heads):
    rows = q_ref.shape[0]
    nm, _, hd = k_ref.shape
    k2 = k_ref[...].reshape(nm * heads, hd).astype(BF16)
    v2 = v_ref[...].reshape(nm * heads, hd).astype(BF16)
    qs = jnp.concatenate([q_ref[:, h * hd:(h + 1) * hd] for h in range(heads)], axis=0)
    s = lax.dot_general(qs, k2, NT_DIMS, preferred_element_type=F32) * hd ** -0.5
    q_head = lax.broadcasted_iota(jnp.int32, s.shape, 0) // rows
    k_head = lax.broadcasted_iota(jnp.int32, s.shape, 1) % heads
    own = q_head == k_head
    m = jnp.max(jnp.where(own, s, -jnp.inf), axis=-1, keepdims=True)
    p = jnp.where(own, jnp.exp(s - m), 0.0)
    p = p / jnp.sum(p, axis=-1, keepdims=True)
    o = jnp.dot(p.astype(BF16), v2, preferred_element_type=F32)
    o_ref[...] = jnp.concatenate([o[h * rows:(h + 1) * rows] for h in range(heads)], axis=-1).astype(o_ref.dtype)


def mem_attention(q, k_prompt, v_prompt, cache_k, cache_v, layer, *, n_prompt_rows, tm):
    m, d = q.shape
    nm = k_prompt.shape[0]
    ns, heads, hd = cache_k.shape[1], cache_k.shape[3], cache_k.shape[4]
    npc = n_prompt_rows // CHUNK
    att = pl.pallas_call(
        functools.partial(_attn_prompt_body, heads=heads),
        grid=(n_prompt_rows // tm,),
        in_specs=[pl.BlockSpec((tm, d), lambda i: (i, 0)),
                  pl.BlockSpec((nm, d), lambda i: (0, 0)),
                  pl.BlockSpec((nm, d), lambda i: (0, 0))],
        out_specs=pl.BlockSpec((tm, d), lambda i: (i, 0)),
        out_shape=jax.ShapeDtypeStruct((n_prompt_rows, d), BF16),
        compiler_params=_cparams(1),
        name="mem_attention_prompt",
    )(q, k_prompt, v_prompt)
    cache_spec = pl.BlockSpec((None, None, nm, heads, hd), lambda s: (layer, s, 0, 0, 0))
    att_s = pl.pallas_call(
        functools.partial(_attn_sample_body, heads=heads),
        grid=(ns,),
        in_specs=[pl.BlockSpec((CHUNK, d), lambda s: (npc + s, 0)), cache_spec, cache_spec],
        out_specs=pl.BlockSpec((CHUNK, d), lambda s: (s, 0)),
        out_shape=jax.ShapeDtypeStruct((m - n_prompt_rows, d), BF16),
        compiler_params=_cparams(1),
        name="mem_attention_sample",
    )(q, cache_k, cache_v)
    return att, att_s


def _ffn_up_body(x_ref, g_ref, wg_ref, wv_ref, cw_ref, cb_ref, prev_ref, act_ref, stp_ref, sts_ref,
                 xn_ref, *, n_prompt_tiles, nseq, col_splits):
    i = pl.program_id(0)
    j = pl.program_id(1)
    tn = act_ref.shape[1]
    wm1 = stp_ref.shape[2]
    sw = tn // col_splits

    @pl.when((i == 0) & (j == 0))
    def _():
        stp_ref[...] = jnp.zeros(stp_ref.shape, F32)

    @pl.when(j == 0)
    def _():
        xn_ref[...] = _rmsnorm_rows(x_ref[...], g_ref[...]).astype(BF16)

    xn = xn_ref[...]
    sample = jnp.where(i >= n_prompt_tiles, 1, 0)
    chunk = lax.broadcasted_iota(jnp.int32, (nseq, 1, sw), 0)
    from_state = chunk * 0 + sample == 1
    fresh = chunk * (1 - sample) == 0
    hist = []
    for k in range(col_splits):
        cs = slice(k * sw, (k + 1) * sw)
        gate = jnp.dot(xn, wg_ref[0, :, cs], preferred_element_type=F32)
        val = jnp.dot(xn, wv_ref[0, :, cs], preferred_element_type=F32)
        prev = jnp.where(from_state, prev_ref[0, :, :, cs], stp_ref[j, :, :, cs])
        gc = _causal_conv(gate, prev, cw_ref[:, cs], cb_ref[:, cs], nseq, CHUNK, fresh=fresh)
        act_ref[:, cs] = (_silu(gc) * val).astype(act_ref.dtype)
        hist.append(gate.reshape(nseq, CHUNK, sw)[:, CHUNK - wm1:, :])
    hist = jnp.concatenate(hist, axis=-1)

    @pl.when(i < n_prompt_tiles)
    def _():
        stp_ref[j] = hist[nseq - 1:]

    @pl.when(i >= n_prompt_tiles)
    def _():
        sts_ref[j, pl.ds((i - n_prompt_tiles) * nseq, nseq)] = hist


def conv_ffn_up(x, g, w_up, layer, conv_w, conv_b, prev_s, *, n_prompt_rows, tm, tn):
    m, d = x.shape
    ff = w_up.shape[2] // 2
    nj = -(-ff // tn)
    ns, wm1 = prev_s.shape[1], prev_s.shape[2]
    npt = n_prompt_rows // tm
    nseq = tm // CHUNK
    col = lambda j, base=0: pl.multiple_of(base + jnp.minimum(j * tn, ff - tn), LANES)
    assert ff % LANES == 0 and tn % LANES == 0
    el = pl.Element
    return pl.pallas_call(
        functools.partial(_ffn_up_body, n_prompt_tiles=npt, nseq=nseq, col_splits=2),
        grid=(m // tm, nj),
        in_specs=[pl.BlockSpec((tm, d), lambda i, j: (i, 0)),
                  pl.BlockSpec((1, d), lambda i, j: (0, 0)),
                  pl.BlockSpec((el(1), el(d), el(tn)), lambda i, j: (layer, 0, col(j))),
                  pl.BlockSpec((el(1), el(d), el(tn)), lambda i, j: (layer, 0, col(j, ff))),
                  pl.BlockSpec((el(wm1 + 1), el(tn)), lambda i, j: (0, col(j))),
                  pl.BlockSpec((el(1), el(tn)), lambda i, j: (0, col(j))),
                  pl.BlockSpec((1, nseq, wm1, tn), lambda i, j: (j, jnp.maximum(i - npt, 0), 0, 0))],
        out_specs=[pl.BlockSpec((tm, tn), lambda i, j: (i, j)),
                   pl.BlockSpec((nj, 1, wm1, tn), lambda i, j: (0, 0, 0, 0)),
                   pl.BlockSpec((nj, ns, wm1, tn), lambda i, j: (0, 0, 0, 0))],
        out_shape=[jax.ShapeDtypeStruct((m, nj * tn), BF16),
                   jax.ShapeDtypeStruct((nj, 1, wm1, tn), F32),
                   jax.ShapeDtypeStruct((nj, ns, wm1, tn), F32)],
        scratch_shapes=[pltpu.VMEM((tm, d), BF16)],
        compiler_params=_cparams(2),
        name="ffn_up",
    )(x, g.reshape(1, d), w_up, w_up, conv_w, conv_b.reshape(1, ff), prev_s)


def _rglru_body(x_ref, gate_ref, cw_ref, cb_ref, wa_ref, ba_ref, wx_ref, bx_ref, lam_ref,
                cv0_ref, h0_ref, o_ref, cvp_ref, hp_ref, cvs_ref, hs_ref, *, n_prompt_tiles, nseq, heads):
    i = pl.program_id(0)
    tm, ch = x_ref.shape
    bd = ch // heads
    x = x_ref[...].astype(F32)

    def run(prev, h0, n_seq, seqlen, first_row):
        xc = _causal_conv(x, prev, cw_ref[...], cb_ref[...], n_seq, seqlen)
        ra, rx = [], []
        for h in range(heads):
            xh = xc[:, h * bd:(h + 1) * bd].astype(BF16)
            ra.append(jnp.dot(xh, wa_ref[h], preferred_element_type=F32))
            rx.append(jnp.dot(xh, wx_ref[h], preferred_element_type=F32))
        r = _sigmoid(jnp.concatenate(ra, axis=-1) + ba_ref[...])
        ig = _sigmoid(jnp.concatenate(rx, axis=-1) + bx_ref[...])
        log_a = RG_C * r * _log_sigmoid(lam_ref[...])
        a = jnp.exp(log_a)
        m2 = -jnp.tanh(log_a) * (a * a + 1.0)
        mult = jnp.where(m2 > 0.0, m2 * lax.rsqrt(m2), 0.0)
        row = lax.broadcasted_iota(jnp.int32, (n_seq, seqlen, ch), 1)
        if first_row is not None:
            mult = jnp.where(row.reshape(tm, ch) + first_row == 0, 1.0, mult)
        u = xc * ig * mult
        ng = tm // SUBLANES
        ag = a.reshape(ng, SUBLANES, ch)
        ug = u.reshape(ng, SUBLANES, ch)
        grow = lax.broadcasted_iota(jnp.int32, (ng, SUBLANES, ch), 1)
        d = 1
        while d < SUBLANES:
            live = grow >= d
            a_sh = jnp.where(live, pltpu.roll(ag, d, axis=1), 1.0)
            u_sh = jnp.where(live, pltpu.roll(ug, d, axis=1), 0.0)
            ug = ug + ag * u_sh
            ag = ag * a_sh
            d *= 2
        groups_per_seq = seqlen // SUBLANES
        hs, last = [], []
        for gi in range(ng):
            if gi % groups_per_seq == 0:
                carry = h0[gi // groups_per_seq]
            hg = ug[gi] + ag[gi] * carry
            carry = hg[SUBLANES - 1:]
            hs.append(hg)
            if (gi + 1) % groups_per_seq == 0:
                last.append(carry)
        h_all = jnp.concatenate(hs, axis=0)
        o_ref[...] = (h_all * _gelu_tanh(gate_ref[...].astype(F32))).astype(o_ref.dtype)
        x3 = x.reshape(n_seq, seqlen, ch)
        return x3[:, seqlen - prev.shape[1]:, :], jnp.stack(last, axis=0)

    @pl.when(i == 0)
    def _():
        cvp_ref[...] = jnp.zeros(cvp_ref.shape, F32)
        hp_ref[...] = jnp.zeros(hp_ref.shape, F32)

    @pl.when(i < n_prompt_tiles)
    def _():
        cv, hl = run(cvp_ref[...], hp_ref[...], 1, tm, i * tm)
        cvp_ref[...] = cv
        hp_ref[...] = hl

    @pl.when(i >= n_prompt_tiles)
    def _():
        cv, hl = run(cv0_ref[...], h0_ref[...], nseq, CHUNK, None)
        s0 = (i - n_prompt_tiles) * nseq
        cvs_ref[pl.ds(s0, nseq)] = cv
        hs_ref[pl.ds(s0, nseq)] = hl


def rglru(proj, x_col, gate_col, conv_w, conv_b, w_a, b_a, w_x, b_x, lam, cv0, h0, *, n_prompt_rows, tm):
    m = proj.shape[0]
    heads, bd, _ = w_a.shape
    ch = heads * bd
    ns, wm1 = cv0.shape[0], cv0.shape[1]
    npt = n_prompt_rows // tm
    nseq = tm // CHUNK
    samp = lambda i: (jnp.maximum(i - npt, 0), 0, 0)
    const2 = lambda i: (0, 0)
    const3 = lambda i: (0, 0, 0)
    return pl.pallas_call(
        functools.partial(_rglru_body, n_prompt_tiles=npt, nseq=nseq, heads=heads),
        grid=(m // tm,),
        in_specs=[pl.BlockSpec((tm, ch), lambda i: (i, x_col)),
                  pl.BlockSpec((tm, ch), lambda i: (i, gate_col)),
                  pl.BlockSpec((wm1 + 1, ch), const2),
                  pl.BlockSpec((1, ch), const2),
                  pl.BlockSpec((heads, bd, bd), const3),
                  pl.BlockSpec((1, ch), const2),
                  pl.BlockSpec((heads, bd, bd), const3),
                  pl.BlockSpec((1, ch), const2),
                  pl.BlockSpec((1, ch), const2),
                  pl.BlockSpec((nseq, wm1, ch), samp),
                  pl.BlockSpec((nseq, 1, ch), samp)],
        out_specs=[pl.BlockSpec((tm, ch), lambda i: (i, 0)),
                   pl.BlockSpec((1, wm1, ch), const3),
                   pl.BlockSpec((1, 1, ch), const3),
                   pl.BlockSpec((ns, wm1, ch), const3),
                   pl.BlockSpec((ns, 1, ch), const3)],
        out_shape=[jax.ShapeDtypeStruct((m, ch), BF16),
                   jax.ShapeDtypeStruct((1, wm1, ch), F32),
                   jax.ShapeDtypeStruct((1, 1, ch), F32),
                   jax.ShapeDtypeStruct((ns, wm1, ch), F32),
                   jax.ShapeDtypeStruct((ns, 1, ch), F32)],
        compiler_params=_cparams(1),
        name="rglru",
    )(proj, proj, conv_w, conv_b.reshape(1, ch), w_a.astype(BF16), b_a.reshape(1, ch),
      w_x.astype(BF16), b_x.reshape(1, ch), lam.reshape(1, ch), cv0, h0)


def _gated_chunk(q, k, g, v, st_ref, slot, kb_ref, kb0, heads):
    c = q.shape[0]
    dk = q.shape[1] // heads
    dv = v.shape[1] // heads
    nsub = c // SUB
    b = _dot01_left(_tril01(c), g) * LOG2_E
    b_end = b[c - 1:c, :]
    qb = (q * jnp.exp2(b)).astype(BF16)
    k_end = (k * jnp.exp2(b_end - b)).astype(BF16)
    dec_end = jnp.exp2(b_end)
    vb = v.astype(BF16)
    kb_ref[kb0] = k
    kb_ref[kb0 + 1] = b
    lane = lax.broadcasted_iota(jnp.int32, (SUB, c), 1)
    srow = lax.broadcasted_iota(jnp.int32, (SUB, c), 0)
    lane_h = lax.broadcasted_iota(jnp.int32, (SUB // 2, c), 1)
    outs = []
    for h in range(heads):
        sk = slice(h * dk, (h + 1) * dk)
        sv = slice(h * dv, (h + 1) * dv)
        st = st_ref[slot, h]
        o = lax.dot_general(qb[:, sk], st.astype(BF16), NT_DIMS, preferred_element_type=F32)
        qh, kh, bh, vh = q[:, sk], k[:, sk], b[:, sk], vb[:, sv]
        blocks = []
        for i in range(nsub):
            r0 = i * SUB
            qi, ki, bi = qh[r0:r0 + SUB], kh[r0:r0 + SUB], bh[r0:r0 + SUB]
            half = SUB // 2
            tops, bots = jnp.zeros((half, c), F32), jnp.zeros((half, c), F32)
            for s in range(SUB):
                ks = kb_ref[kb0, r0 + s:r0 + s + 1, sk]
                bs = kb_ref[kb0 + 1, r0 + s:r0 + s + 1, sk]
                if s < half:
                    col = jnp.sum(qi[:half] * (ks * jnp.exp2(bi[:half] - bs)), axis=-1, keepdims=True)
                    tops = jnp.where(lane_h == r0 + s, col, tops)
                col = jnp.sum(qi[half:] * (ks * jnp.exp2(bi[half:] - bs)), axis=-1, keepdims=True)
                bots = jnp.where(lane_h == r0 + s, col, bots)
            a = jnp.where(srow >= lane - r0, jnp.concatenate([tops, bots], axis=0), 0.0)
            if i > 0:
                ri = bh[r0:r0 + 1]
                qt = (qi * jnp.exp2(bi - ri)).astype(BF16)
                kt = jnp.concatenate([kh[:r0] * jnp.exp2(ri - bh[:r0]),
                                      jnp.zeros((c - r0, dk), F32)], axis=0).astype(BF16)
                a = a + lax.dot_general(qt, kt, NT_DIMS, preferred_element_type=F32)
            blocks.append(a)
        a_full = jnp.concatenate(blocks, axis=0).astype(BF16)
        o = o + jnp.dot(a_full, vh, preferred_element_type=F32)
        st_ref[slot, h] = st * dec_end[:, sk] + lax.dot_general(
            vh, k_end[:, sk], TN_DIMS, preferred_element_type=F32)
        outs.append(o)
    return outs


def _head_norm_gate(outs, ng_ref, gate):
    res = []
    for h, o in enumerate(outs):
        dv = o.shape[-1]
        sv = slice(h * dv, (h + 1) * dv)
        res.append(_rmsnorm_rows(o, ng_ref[:, sv]) * _silu(gate[:, sv].astype(F32)))
    return jnp.concatenate(res, axis=-1)


def _chunk_steps(states, n_prompt_steps, one_chunk):
    c = pl.program_id(0)

    @pl.when((c == 0) | (c >= n_prompt_steps))
    def _():
        for st_ref, s0_ref in states:
            st_ref[...] = s0_ref[...]

    sample = jnp.where(c >= n_prompt_steps, 1, 0)
    for k in range(CHUNKS_PER_STEP):
        one_chunk(slice(k * CHUNK, (k + 1) * CHUNK), sample * k, k)


def _hgrn_body(q_ref, f_ref, v_ref, gate_ref, lb_ref, ng_ref, s0_ref, o_ref, st_ref, kb_ref,
               *, n_prompt_steps, heads):
    def one_chunk(rows, slot, k_in_step):
        kk = (1.0 - lb_ref[...]) / (1.0 + jnp.exp(f_ref[rows, :].astype(F32)))
        g = jnp.log1p(-kk)
        outs = _gated_chunk(_silu(q_ref[rows, :].astype(F32)), kk, g, v_ref[rows, :], st_ref, slot,
                            kb_ref, 2 * k_in_step, heads)
        o_ref[rows, :] = _head_norm_gate(outs, ng_ref, gate_ref[rows, :]).astype(o_ref.dtype)

    _chunk_steps(((st_ref, s0_ref),), n_prompt_steps, one_chunk)


def _gla_body(q_ref, k_ref, v_ref, gate_ref, gl_ref, gkw_ref, gkb_ref, ng_ref, s0_ref, o_ref, st_ref, kb_ref,
              *, n_prompt_steps, heads):
    dk = q_ref.shape[1] // heads

    def one_chunk(rows, slot, k_in_step):
        z = jnp.dot(gl_ref[rows, :].astype(BF16), gkw_ref[...], preferred_element_type=F32) + gkb_ref[...]
        g = _log_sigmoid(z) * (1.0 / D_GATE_NORM)
        outs = _gated_chunk(q_ref[rows, :].astype(F32) * dk ** -0.5, k_ref[rows, :].astype(F32), g, v_ref[rows, :],
                            st_ref, slot, kb_ref, 2 * k_in_step, heads)
        o_ref[rows, :] = _head_norm_gate(outs, ng_ref, gate_ref[rows, :]).astype(o_ref.dtype)

    _chunk_steps(((st_ref, s0_ref),), n_prompt_steps, one_chunk)


def _chunk_state_map(n_prompt_chunks, ndim):
    return lambda c: (jnp.maximum(c - (n_prompt_chunks - 1), 0),) + (0,) * (ndim - 1)


def hgrn2(proj, lb, norm_g, s0, *, n_prompt_chunks):
    m = proj.shape[0]
    _, heads, dv, dk = s0.shape
    kw, vw = heads * dk, heads * dv
    n_prompt_steps = n_prompt_chunks // CHUNKS_PER_STEP
    smap = _chunk_state_map(n_prompt_steps, 4)
    row = lambda c: (0, 0)
    return pl.pallas_call(
        functools.partial(_hgrn_body, n_prompt_steps=n_prompt_steps, heads=heads),
        grid=(m // STEP_ROWS,),
        in_specs=[pl.BlockSpec((STEP_ROWS, kw), lambda c: (c, 0)),
                  pl.BlockSpec((STEP_ROWS, kw), lambda c: (c, 1)),
                  pl.BlockSpec((STEP_ROWS, vw), lambda c: (c, 2 * kw // vw)),
                  pl.BlockSpec((STEP_ROWS, vw), lambda c: (c, 2 * kw // vw + 1)),
                  pl.BlockSpec((1, kw), row),
                  pl.BlockSpec((1, vw), row),
                  pl.BlockSpec((CHUNKS_PER_STEP, heads, dv, dk), smap)],
        out_specs=[pl.BlockSpec((STEP_ROWS, vw), lambda c: (c, 0)),
                   pl.BlockSpec((CHUNKS_PER_STEP, heads, dv, dk), smap)],
        out_shape=[jax.ShapeDtypeStruct((m, vw), BF16),
                   jax.ShapeDtypeStruct(s0.shape, F32)],
        scratch_shapes=[pltpu.VMEM((2 * CHUNKS_PER_STEP, CHUNK, kw), F32)],
        compiler_params=_cparams(1),
        name="hgrn2",
    )(proj, proj, proj, proj, lb.reshape(1, kw), norm_g.reshape(1, vw), s0)


def gla(proj, cols, gk_w, gk_b, norm_g, s0, *, n_prompt_chunks):
    m = proj.shape[0]
    _, heads, dv, dk = s0.shape
    kw, vw = heads * dk, heads * dv
    cq, ck, cv, cg, cgl = cols
    n_prompt_steps = n_prompt_chunks // CHUNKS_PER_STEP
    smap = _chunk_state_map(n_prompt_steps, 4)
    row = lambda c: (0, 0)
    return pl.pallas_call(
        functools.partial(_gla_body, n_prompt_steps=n_prompt_steps, heads=heads),
        grid=(m // STEP_ROWS,),
        in_specs=[pl.BlockSpec((STEP_ROWS, kw), lambda c: (c, cq)),
                  pl.BlockSpec((STEP_ROWS, kw), lambda c: (c, ck)),
                  pl.BlockSpec((STEP_ROWS, vw), lambda c: (c, cv)),
                  pl.BlockSpec((STEP_ROWS, vw), lambda c: (c, cg)),
                  pl.BlockSpec((STEP_ROWS, LANES), lambda c: (c, cgl)),
                  pl.BlockSpec((LANES, kw), row),
                  pl.BlockSpec((1, kw), row),
                  pl.BlockSpec((1, vw), row),
                  pl.BlockSpec((CHUNKS_PER_STEP, heads, dv, dk), smap)],
        out_specs=[pl.BlockSpec((STEP_ROWS, vw), lambda c: (c, 0)),
                   pl.BlockSpec((CHUNKS_PER_STEP, heads, dv, dk), smap)],
        out_shape=[jax.ShapeDtypeStruct((m, vw), BF16),
                   jax.ShapeDtypeStruct(s0.shape, F32)],
        scratch_shapes=[pltpu.VMEM((2 * CHUNKS_PER_STEP, CHUNK, kw), F32)],
        compiler_params=_cparams(1),
        name="gla",
    )(proj, proj, proj, proj, proj, gk_w, gk_b.reshape(1, kw), norm_g.reshape(1, vw), s0)


def _ssd_body(z_ref, x_ref, bc_ref, dt_ref, cw_ref, cb_ref, dtb_ref, alog_ref, dsk_ref, ng_ref,
              cv0_ref, s0_ref, o_ref, cv_ref, st_ref, *, n_prompt_steps, heads, groups):
    def one_chunk(rws, slot, _):
        _ssd_chunk(rws, slot, z_ref, x_ref, bc_ref, dt_ref, cw_ref, cb_ref, dtb_ref, alog_ref, dsk_ref, ng_ref,
                   o_ref, cv_ref, st_ref, heads=heads, groups=groups)

    _chunk_steps(((st_ref, s0_ref), (cv_ref, cv0_ref)), n_prompt_steps, one_chunk)


def _ssd_chunk(rws, slot, z_ref, x_ref, bc_ref, dt_ref, cw_ref, cb_ref, dtb_ref, alog_ref, dsk_ref, ng_ref,
               o_ref, cv_ref, st_ref, *, heads, groups):
    rows, xw_ = CHUNK, x_ref.shape[1]
    hp = xw_ // heads
    n = bc_ref.shape[1] // (2 * groups)
    gw = xw_ // groups
    x_raw, bc_raw = x_ref[rws, :].astype(F32), bc_ref[rws, :].astype(F32)
    prev = cv_ref[pl.ds(slot, 1)]
    cw, cb = cw_ref[...], cb_ref[...]
    xs = _silu(_causal_conv(x_raw, prev[:, :, :xw_], cw[:, :xw_], cb[:, :xw_], 1, rows))
    bcs = _silu(_causal_conv(bc_raw, prev[:, :, xw_:], cw[:, xw_:], cb[:, xw_:], 1, rows))
    wm1 = prev.shape[1]
    cv_ref[slot, :, :xw_] = x_raw[rows - wm1:, :]
    cv_ref[slot, :, xw_:] = bc_raw[rows - wm1:, :]

    erow = lax.broadcasted_iota(jnp.int32, (LANES, xw_), 0)
    ecol = lax.broadcasted_iota(jnp.int32, (LANES, xw_), 1)
    expand = jnp.where(ecol // hp == erow, 1.0, 0.0).astype(BF16)
    dt = _softplus(_dot01_right(dt_ref[rws, :].astype(F32), expand) + dtb_ref[...])
    dta = dt * (-jnp.exp(alog_ref[...]))
    cum = _dot01_left(_tril01(rows), dta)
    cum_end = cum[rows - 1:rows, :]
    xdt = xs * dt
    xdt_b = xdt.astype(BF16)
    x_end = (xdt * jnp.exp(cum_end - cum)).astype(BF16)
    e_cum = jnp.exp(cum)
    dec_end = jnp.exp(cum_end)

    prow = lax.broadcasted_iota(jnp.int32, (rows, LANES), 0)
    plane = lax.broadcasted_iota(jnp.int32, (rows, LANES), 1)
    assert hp == rows and LANES % hp == 0
    pair = LANES // hp
    psrc = plane % hp
    y_parts = []
    for g in range(groups):
        bg = bcs[:, g * n:(g + 1) * n].astype(BF16)
        cg = bcs[:, (groups + g) * n:(groups + g + 1) * n].astype(BF16)
        gl = slice(g * gw, (g + 1) * gw)
        st = st_ref[slot, :, gl]
        y_inter = jnp.dot(cg, st.astype(BF16), preferred_element_type=F32) * e_cum[:, gl]
        cb_rep = lax.dot_general(cg, jnp.concatenate([bg] * pair, axis=0), NT_DIMS, preferred_element_type=F32)
        y_intra = []
        for sl in range(gw // LANES):
            lo = g * gw + sl * LANES
            cs = cum[:, lo:lo + LANES]
            cdiag = jnp.sum(jnp.where(prow == psrc, cs, 0.0), axis=0, keepdims=True)
            decay = jnp.exp(jnp.minimum(cs - cdiag, 0.0))
            mt = jnp.where(prow >= psrc, cb_rep * decay, 0.0).astype(BF16)
            xp = xdt_b[:, lo:lo + LANES]
            rhs = jnp.concatenate(
                [jnp.where(plane // hp == p, xp, jnp.zeros_like(xp)) for p in range(pair)], axis=0)
            y_intra.append(jnp.dot(mt, rhs, preferred_element_type=F32))
        y_parts.append(jnp.concatenate(y_intra, axis=-1) + y_inter)
        st_ref[slot, :, gl] = st * dec_end[:, gl] + lax.dot_general(
            bg, x_end[:, gl], TN_DIMS, preferred_element_type=F32)
    y = jnp.concatenate(y_parts, axis=-1) + dsk_ref[...] * xs
    yz = y * _silu(z_ref[rws, :].astype(F32))
    res = []
    for g in range(groups):
        gl = slice(g * gw, (g + 1) * gw)
        res.append(_rmsnorm_rows(yz[:, gl], ng_ref[:, gl]))
    o_ref[rws, :] = jnp.concatenate(res, axis=-1).astype(o_ref.dtype)


def ssd(proj, cols, conv_w, conv_b, dt_bias_x, a_log_x, d_skip_x, norm_g, cv0, s0, *, n_prompt_chunks):
    m = proj.shape[0]
    xw_ = s0.shape[2]
    n = s0.shape[1]
    bcw = 2 * C_GROUPS * n
    cz, cx, cbc, cdt = cols
    wm1 = cv0.shape[1]
    n_prompt_steps = n_prompt_chunks // CHUNKS_PER_STEP
    smap3 = _chunk_state_map(n_prompt_steps, 3)
    row = lambda c: (0, 0)
    return pl.pallas_call(
        functools.partial(_ssd_body, n_prompt_steps=n_prompt_steps, heads=C_HEADS, groups=C_GROUPS),
        grid=(m // STEP_ROWS,),
        in_specs=[pl.BlockSpec((STEP_ROWS, xw_), lambda c: (c, cz)),
                  pl.BlockSpec((STEP_ROWS, xw_), lambda c: (c, cx)),
                  pl.BlockSpec((STEP_ROWS, bcw), lambda c: (c, cbc)),
                  pl.BlockSpec((STEP_ROWS, LANES), lambda c: (c, cdt)),
                  pl.BlockSpec((wm1 + 1, xw_ + bcw), row),
                  pl.BlockSpec((1, xw_ + bcw), row),
                  pl.BlockSpec((1, xw_), row),
                  pl.BlockSpec((1, xw_), row),
                  pl.BlockSpec((1, xw_), row),
                  pl.BlockSpec((1, xw_), row),
                  pl.BlockSpec((CHUNKS_PER_STEP, wm1, xw_ + bcw), smap3),
                  pl.BlockSpec((CHUNKS_PER_STEP, n, xw_), smap3)],
        out_specs=[pl.BlockSpec((STEP_ROWS, xw_), lambda c: (c, 0)),
                   pl.BlockSpec((CHUNKS_PER_STEP, wm1, xw_ + bcw), smap3),
                   pl.BlockSpec((CHUNKS_PER_STEP, n, xw_), smap3)],
        out_shape=[jax.ShapeDtypeStruct((m, xw_), BF16),
                   jax.ShapeDtypeStruct(cv0.shape, F32),
                   jax.ShapeDtypeStruct(s0.shape, F32)],
        compiler_params=_cparams(1),
        name="ssd",
    )(proj, proj, proj, proj, conv_w, conv_b.reshape(1, xw_ + bcw), dt_bias_x.reshape(1, xw_),
      a_log_x.reshape(1, xw_), d_skip_x.reshape(1, xw_), norm_g.reshape(1, xw_), cv0, s0)


def _with_zero_first(s, n=1):
    return jnp.concatenate([jnp.zeros((n,) + s.shape[1:], s.dtype), s], axis=0)


def _drop_unused_prompt_slots(st):
    return jnp.concatenate([st[:1], st[CHUNKS_PER_STEP:]], axis=0)


def _col_tile_starts(width, tn):
    return [min(j * tn, width - tn) for j in range(-(-width // tn))]


def _block_cols(a, width, tn):
    return jnp.stack([a[..., o:o + tn] for o in _col_tile_starts(width, tn)], axis=0)


def _unblock_cols(b, width, tn):
    return jnp.concatenate([b[j][..., j * tn - o:] for j, o in enumerate(_col_tile_starts(width, tn))], axis=-1)


def kernel(x_prompt, x_sample, mem_prompt, cache_mem_k, cache_mem_v, state_hgrn, state_rglru, state_rg_conv, state_ssd, state_ssd_conv, state_gla, state_ffn_conv, norm_mix, norm_xattn, norm_mem, norm_ffn, ev_w_in, hgrn_lb_logits, hgrn_norm, rg_conv_w, rg_conv_b, rg_w_a, rg_b_a, rg_w_x, rg_b_x, rg_lambda, ev_w_out, od_w_in, ssd_conv_w, ssd_conv_b, ssd_dt_bias, ssd_A_log, ssd_D, ssd_norm, gla_gk_w, gla_gk_b, gla_norm, od_w_out, xa_wq, xa_wk, xa_wv, xa_wo, ffn_up, ffn_conv_w, ffn_conv_b, ffn_down, final_norm):
    bp, p_rows, d = x_prompt.shape
    ns, s_len, _ = x_sample.shape
    assert bp == 1 and s_len == CHUNK and p_rows % STEP_ROWS == 0 and ns % CHUNKS_PER_STEP == 0
    depth = norm_mix.shape[0]
    s_rows = ns * s_len
    m = p_rows + s_rows
    npc = p_rows // CHUNK
    nm = mem_prompt.shape[1]
    tm = math.gcd(math.gcd(p_rows, s_rows), 1024)
    tm_seq = math.gcd(tm, 256)

    x = (x_prompt.reshape(p_rows, d), x_sample.reshape(s_rows, d))
    lb_all = jnp.cumsum(jax.nn.softmax(hgrn_lb_logits.astype(F32), axis=0), axis=0)
    ev_w_in_b, ev_w_out_b, od_w_out_b = ev_w_in.astype(BF16), ev_w_out.astype(BF16), od_w_out.astype(BF16)
    wq_b, wk_b, wv_b, wo_b = xa_wq.astype(BF16), xa_wk.astype(BF16), xa_wv.astype(BF16), xa_wo.astype(BF16)
    w_up_b, w_down_b = ffn_up.astype(BF16), ffn_down.astype(BF16)

    a_kw = hgrn_lb_logits.shape[1]
    a_vw = hgrn_norm.shape[1]
    b_w = rg_lambda.shape[1]
    c_w = ssd_norm.shape[1]
    c_bc = 2 * C_GROUPS * C_STATE
    c_hd = c_w // C_HEADS
    d_val = gla_norm.shape[1]
    d_key = gla_gk_b.shape[1]
    d_rank = gla_gk_w.shape[1]
    ff = ffn_conv_b.shape[1]
    ffn_tn = 512
    in_tn = 1536
    tm_full = tm // 2

    mem_k, mem_v = [], []
    hg, rl, rc, ss, sc, gs, fc_p, fc_s = [], [], [], [], [], [], [], []
    for l in range(depth):
        j = l // 2
        if l % 2 == 0:
            proj = norm_matmul(x, norm_mix[l], ev_w_in_b, j, tm=tm, tn=in_tn, out_dtype=PROJ_DTYPE)
            s0 = _with_zero_first(jnp.swapaxes(state_hgrn[j], -1, -2), CHUNKS_PER_STEP)
            o_a, st = hgrn2(proj, lb_all[l], hgrn_norm[j], s0, n_prompt_chunks=npc)
            hg.append(jnp.swapaxes(_drop_unused_prompt_slots(st), -1, -2))
            xcol = (2 * a_kw + 2 * a_vw) // b_w
            o_b, cvp, hp_, cvs, hs_ = rglru(
                proj, xcol, xcol + 1, rg_conv_w[j], rg_conv_b[j], rg_w_a[j], rg_b_a[j], rg_w_x[j], rg_b_x[j],
                rg_lambda[j], state_rg_conv[j], state_rglru[j].reshape(ns, 1, b_w), n_prompt_rows=p_rows, tm=tm_seq)
            rc.append((cvp, cvs))
            rl.append((hp_.reshape(1, b_w), hs_.reshape(ns, b_w)))
            x = matmul_res([(o_a, 0, a_vw, 0), (o_b, 0, b_w, a_vw)], ev_w_out_b, j, x, tm=tm_full, tn=d)
        else:
            w = od_w_in[j]
            offs = np_cumsum_offsets((c_w, c_w + c_bc, C_HEADS, d_key, d_key, d_val, d_val, d_rank))
            o_z, o_xbc, o_dt, o_q, o_k, o_v, o_g, o_gl = offs
            pad = lambda a: jnp.pad(a, ((0, 0), (0, LANES - a.shape[1])))
            w_re = jnp.concatenate([
                w[:, o_z:o_z + c_w], w[:, o_xbc:o_xbc + c_w], w[:, o_xbc + c_w:o_xbc + c_w + c_bc],
                w[:, o_q:o_q + d_key], w[:, o_k:o_k + d_key],
                pad(w[:, o_dt:o_dt + C_HEADS]), pad(w[:, o_gl:o_gl + d_rank]),
                jnp.zeros((d, 2 * LANES), w.dtype),
                w[:, o_v:o_v + d_val], w[:, o_g:o_g + d_val]], axis=1).astype(BF16)
            proj = norm_matmul(x, norm_mix[l], w_re, tm=tm, tn=in_tn, out_dtype=PROJ_DTYPE)
            assert c_w == d_val
            base = 2 * c_w + c_bc
            col_dt = (base + 2 * d_key) // LANES
            cv0 = _with_zero_first(state_ssd_conv[j], CHUNKS_PER_STEP)
            s0c = _with_zero_first(jnp.transpose(state_ssd[j], (0, 3, 1, 2)).reshape(ns, C_STATE, c_w),
                                   CHUNKS_PER_STEP)
            rep = lambda a: jnp.repeat(a, c_hd)
            o_c, cv, stc = ssd(proj, (0, 1, 2 * c_w // c_bc, col_dt), ssd_conv_w[j], ssd_conv_b[j],
                               rep(ssd_dt_bias[j]), rep(ssd_A_log[j]), rep(ssd_D[j]), ssd_norm[j], cv0, s0c,
                               n_prompt_chunks=npc)
            sc.append(_drop_unused_prompt_slots(cv))
            stc = _drop_unused_prompt_slots(stc)
            ss.append(jnp.transpose(stc.reshape(1 + ns, C_STATE, C_HEADS, c_hd), (0, 2, 3, 1)))
            s0d = _with_zero_first(jnp.swapaxes(state_gla[j], -1, -2), CHUNKS_PER_STEP)
            gkw = jnp.pad(gla_gk_w[j], ((0, LANES - d_rank), (0, 0))).astype(BF16)
            v_start = base + 2 * d_key + 4 * LANES
            o_d, std = gla(proj, (base // d_key, base // d_key + 1, v_start // d_val, v_start // d_val + 1, col_dt + 1),
                           gkw, gla_gk_b[j], gla_norm[j], s0d, n_prompt_chunks=npc)
            gs.append(jnp.swapaxes(_drop_unused_prompt_slots(std), -1, -2))
            x = matmul_res([(o_c, 0, c_w, 0), (o_d, 0, d_val, c_w)], od_w_out_b, j, x, tm=tm_full, tn=d)

        mem = mem_prompt.reshape(nm, d)
        k_p = norm_matmul(mem, norm_mem[l], wk_b, l, tm=nm, tn=512, out_dtype=F32)
        v_p = norm_matmul(mem, norm_mem[l], wv_b, l, tm=nm, tn=512, out_dtype=F32)
        mem_k.append(k_p.reshape(1, nm, MEM_HEADS, d // MEM_HEADS))
        mem_v.append(v_p.reshape(1, nm, MEM_HEADS, d // MEM_HEADS))
        q = norm_matmul(x, norm_xattn[l], wq_b, l, tm=tm, tn=d, out_dtype=BF16)
        att = mem_attention(q, k_p, v_p, cache_mem_k, cache_mem_v, l, n_prompt_rows=p_rows, tm=tm)
        x = matmul_res([(att, 0, d, 0)], wo_b, l, x, tm=tm_full, tn=d)

        prev_s = _block_cols(state_ffn_conv[l], ff, ffn_tn)
        act, stp, sts = conv_ffn_up(x, norm_ffn[l], w_up_b, l, ffn_conv_w[l], ffn_conv_b[l],
                                    prev_s, n_prompt_rows=p_rows, tm=tm, tn=ffn_tn)
        fc_p.append(_unblock_cols(stp, ff, ffn_tn))
        fc_s.append(_unblock_cols(sts, ff, ffn_tn))
        starts = _col_tile_starts(ff, ffn_tn)
        body = (len(starts) - 1) * ffn_tn
        down_terms = [(act, 0, body, 0), (act, body + (body - starts[-1]), ff - body, body)]
        x = matmul_res(down_terms, w_down_b, l, x, tm=tm, tn=512)

    y_prompt, y_sample = rmsnorm_rows(x, final_norm, n_prompt_rows=p_rows, tm=tm)
    y_prompt = y_prompt.reshape(1, p_rows, d)
    y_sample = y_sample.reshape(ns, s_len, d)
    stack_p = lambda lst: jnp.stack([a[:1] for a in lst])
    stack_s = lambda lst: jnp.stack([a[1:] for a in lst])
    return (y_prompt, y_sample, jnp.stack(mem_k), jnp.stack(mem_v),
            stack_p(hg), jnp.stack([a[0] for a in rl]), jnp.stack([a[0] for a in rc]),
            stack_p(ss), stack_p(sc), stack_p(gs), jnp.stack(fc_p),
            stack_s(hg), jnp.stack([a[1] for a in rl]), jnp.stack([a[1] for a in rc]),
            stack_s(ss), stack_s(sc), stack_s(gs), jnp.stack(fc_s))


def np_cumsum_offsets(sizes):
    offs, acc = [], 0
    for s in sizes:
        offs.append(acc)
        acc += s
    return offs
```

```python
import functools
import math

import jax
import jax.numpy as jnp
from jax import lax
from jax.experimental import pallas as pl
from jax.experimental.pallas import tpu as pltpu

F32 = jnp.float32
BF16 = jnp.bfloat16
PROJ_DTYPE = F32

CHUNK = 64
CHUNKS_PER_STEP = 8
STEP_ROWS = CHUNK * CHUNKS_PER_STEP
SUB = 16
EPS = 1e-6
LOG2_E = 1.4426950408889634
LANES = 128
SUBLANES = 8
VMEM_LIMIT = 56 * 1024 * 1024

MEM_HEADS = 4
A_HEADS = 8
B_HEADS = 8
C_HEADS = 16
C_GROUPS = 2
C_STATE = 128
D_HEADS = 4
RG_C = 8.0
D_GATE_NORM = 16.0

NT_DIMS = (((1,), (1,)), ((), ()))
TN_DIMS = (((0,), (0,)), ((), ()))


def _cparams(n_axes):
    return pltpu.CompilerParams(dimension_semantics=("arbitrary",) * n_axes,
                                vmem_limit_bytes=VMEM_LIMIT)


def _sigmoid(x):
    return 1.0 / (1.0 + jnp.exp(-x))


def _silu(x):
    return x * _sigmoid(x)


def _log_sigmoid(x):
    return jnp.minimum(x, 0.0) - jnp.log1p(jnp.exp(-jnp.abs(x)))


def _softplus(x):
    return jnp.maximum(x, 0.0) + jnp.log1p(jnp.exp(-jnp.abs(x)))


def _gelu_tanh(x):
    return 0.5 * x * (1.0 + jnp.tanh(math.sqrt(2.0 / math.pi) * (x + 0.044715 * (x * x * x))))


def _split3(x):
    hi = x.astype(BF16)
    r = x - hi.astype(F32)
    mid = r.astype(BF16)
    lo = (r - mid.astype(F32)).astype(BF16)
    return hi, mid, lo


def _dot01_left(t01, x):
    return sum(jnp.dot(t01, p, preferred_element_type=F32) for p in _split3(x))


def _dot01_right(x, e01):
    return sum(jnp.dot(p, e01, preferred_element_type=F32) for p in _split3(x))


def _tril01(n):
    r = lax.broadcasted_iota(jnp.int32, (n, n), 0)
    c = lax.broadcasted_iota(jnp.int32, (n, n), 1)
    return jnp.where(r >= c, 1.0, 0.0).astype(BF16)


def _causal_conv(x, prev, w, b, nseq, seqlen, fresh=None):
    width = w.shape[0]
    ch = x.shape[-1]
    head = 8
    tap = lambda d: w[width - 1 - d:width - d].reshape(1, 1, ch)
    x3 = x.reshape(nseq, seqlen, ch)
    rolled = [pltpu.roll(x, d, axis=0).reshape(nseq, seqlen, ch) for d in range(1, width)]
    y = b + tap(0) * x3
    for d in range(1, width):
        y = y + tap(d) * rolled[d - 1]
    row = lax.broadcasted_iota(jnp.int32, (nseq, head, ch), 1)
    yh = b + tap(0) * x3[:, :head]
    for d in range(1, width):
        xs = rolled[d - 1][:, :head]
        for r in range(d):
            p = width - 1 - d + r
            xs = jnp.where(row == r, prev[:, p:p + 1, :], xs)
        yh = yh + tap(d) * xs
    if fresh is not None:
        yh = jnp.where(fresh, yh, y[:, :head])
    y = jnp.concatenate([yh, y[:, head:]], axis=1)
    return y.reshape(nseq * seqlen, ch)


def _rmsnorm_rows(x, g):
    ms = jnp.mean(x * x, axis=-1, keepdims=True)
    return x * lax.rsqrt(ms + EPS) * g


def _row_specs(rows, tm, width, col=lambda j: 0, single_buffer_pair=False):
    blk = (pl.Element(tm), pl.Element(width))
    row = lambda t: pl.multiple_of(t * tm, tm)
    if not isinstance(rows, tuple):
        return [pl.BlockSpec(blk, lambda i, j: (row(i), col(j)))], rows.shape[0], None
    npt = rows[0].shape[0] // tm
    mode = dict(pipeline_mode=pl.Buffered(1)) if single_buffer_pair else {}
    specs = [pl.BlockSpec(blk, lambda i, j: (row(jnp.minimum(i, npt - 1)), col(j)), **mode),
             pl.BlockSpec(blk, lambda i, j: (row(jnp.maximum(i - npt, 0)), col(j)), **mode)]
    return specs, rows[0].shape[0] + rows[1].shape[0], npt


def _for_row_source(npt, fn, also=True):
    if npt is None and also is True:
        fn(0)
    elif npt is None:
        pl.when(also)(lambda: fn(0))
    else:
        i = pl.program_id(0)
        pl.when(also & (i < npt))(lambda: fn(0))
        pl.when(also & (i >= npt))(lambda: fn(1))


def _norm_matmul_body(*refs, npt):
    n_x = 1 if npt is None else 2
    x_refs = refs[:n_x]
    g_ref, w_ref, o_ref, xn_ref = refs[n_x:]

    def normalize(k):
        xn_ref[...] = _rmsnorm_rows(x_refs[k][...], g_ref[...]).astype(BF16)

    _for_row_source(npt, normalize, also=pl.program_id(1) == 0)
    o_ref[...] = jnp.dot(xn_ref[...], w_ref[...], preferred_element_type=F32).astype(o_ref.dtype)


def _weight_spec(w, layer, k_rows, k_block, tn):
    if layer is None:
        return pl.BlockSpec((k_rows, tn), lambda i, j: (k_block, j))
    return pl.BlockSpec((None, k_rows, tn), lambda i, j: (layer, k_block, j))


def norm_matmul(x, g, w, layer=None, *, tm, tn, out_dtype):
    d, n = w.shape[-2:]
    x_specs, m, npt = _row_specs(x, tm, d, single_buffer_pair=True)
    xs = x if isinstance(x, tuple) else (x,)
    return pl.pallas_call(
        functools.partial(_norm_matmul_body, npt=npt),
        grid=(m // tm, n // tn),
        in_specs=x_specs + [pl.BlockSpec((1, d), lambda i, j: (0, 0)), _weight_spec(w, layer, d, 0, tn)],
        out_specs=pl.BlockSpec((tm, tn), lambda i, j: (i, j)),
        out_shape=jax.ShapeDtypeStruct((m, n), out_dtype),
        scratch_shapes=[pltpu.VMEM((tm, d), BF16)],
        compiler_params=_cparams(2),
        name="norm_matmul",
    )(*xs, g.reshape(1, d), w)


def _matmul_res_body(*refs, arity, npt):
    pos = 0
    terms = []
    for n_a in arity[:-1]:
        terms.append((refs[pos:pos + n_a], refs[pos + n_a]))
        pos += n_a + 1
    r_refs, o_ref = refs[pos:pos + arity[-1]], refs[-1]

    def compute(k):
        acc = r_refs[min(k, len(r_refs) - 1)][...]
        for a_refs, w_ref in terms:
            acc = acc + jnp.dot(a_refs[min(k, len(a_refs) - 1)][...], w_ref[0], preferred_element_type=F32)
        o_ref[...] = acc

    _for_row_source(npt, compute)


def matmul_res(terms, w, layer, res, *, tm, tn):
    n = w.shape[-1]
    el = pl.Element
    in_specs, operands, arity, npts = [], [], [], set()
    for a, a_col, k, w_row in terms:
        specs, m, npt = _row_specs(a, tm, k, col=lambda j, c=a_col: c)
        in_specs += specs + [pl.BlockSpec((el(1), el(k), el(tn)),
                                          lambda i, j, r=w_row: (layer, r, pl.multiple_of(j * tn, tn)))]
        operands += list(a if isinstance(a, tuple) else (a,)) + [w]
        arity.append(len(specs))
        npts.add(npt)
    r_specs, m, npt = _row_specs(res, tm, tn, col=lambda j: pl.multiple_of(j * tn, tn))
    arity.append(len(r_specs))
    npts = (npts | {npt}) - {None}
    assert len(npts) <= 1
    return pl.pallas_call(
        functools.partial(_matmul_res_body, arity=tuple(arity), npt=npts.pop() if npts else None),
        grid=(m // tm, n // tn),
        in_specs=in_specs + r_specs,
        out_specs=pl.BlockSpec((tm, tn), lambda i, j: (i, j)),
        out_shape=jax.ShapeDtypeStruct((m, n), F32),
        compiler_params=_cparams(2),
        name="matmul_res",
    )(*operands, *(res if isinstance(res, tuple) else (res,)))


def _rmsnorm_body(x_ref, g_ref, op_ref, os_ref, *, npt):
    y = _rmsnorm_rows(x_ref[...], g_ref[...])

    @pl.when(pl.program_id(0) < npt)
    def _():
        op_ref[...] = y

    @pl.when(pl.program_id(0) >= npt)
    def _():
        os_ref[...] = y


def rmsnorm_rows(x, g, *, n_prompt_rows, tm):
    m, d = x.shape
    npt = n_prompt_rows // tm
    return pl.pallas_call(
        functools.partial(_rmsnorm_body, npt=npt),
        grid=(m // tm,),
        in_specs=[pl.BlockSpec((tm, d), lambda i: (i, 0)), pl.BlockSpec((1, d), lambda i: (0, 0))],
        out_specs=[pl.BlockSpec((tm, d), lambda i: (jnp.minimum(i, npt - 1), 0)),
                   pl.BlockSpec((tm, d), lambda i: (jnp.maximum(i - npt, 0), 0))],
        out_shape=[jax.ShapeDtypeStruct((n_prompt_rows, d), F32),
                   jax.ShapeDtypeStruct((m - n_prompt_rows, d), F32)],
        compiler_params=_cparams(1),
        name="final_norm",
    )(x, g.reshape(1, d))


def _attn_head(qh, kh, vh):
    s = lax.dot_general(qh, kh, NT_DIMS, preferred_element_type=F32) * qh.shape[-1] ** -0.5
    p = jnp.exp(s - jnp.max(s, axis=-1, keepdims=True))
    p = p / jnp.sum(p, axis=-1, keepdims=True)
    return jnp.dot(p.astype(BF16), vh, preferred_element_type=F32)


def _attn_prompt_body(q_ref, k_ref, v_ref, o_ref, *, heads):
    hd = q_ref.shape[1] // heads
    outs = []
    for h in range(heads):
        sl = slice(h * hd, (h + 1) * hd)
        outs.append(_attn_head(q_ref[:, sl], k_ref[:, sl].astype(BF16), v_ref[:, sl].astype(BF16)))
    o_ref[...] = jnp.concatenate(outs, axis=-1).astype(o_ref.dtype)


def _attn_sample_body(q_ref, k_ref, v_ref, o_ref, *, heads):
    rows = q_ref.shape[0]
    nm, _, hd = k_ref.shape
    k2 = k_ref[...].reshape(nm * heads, hd).astype(BF16)
    v2 = v_ref[...].reshape(nm * heads, hd).astype(BF16)
    qs = jnp.concatenate([q_ref[:, h * hd:(h + 1) * hd] for h in range(heads)], axis=0)
    s = lax.dot_general(qs, k2, NT_DIMS, preferred_element_type=F32) * hd ** -0.5
    q_head = lax.broadcasted_iota(jnp.int32, s.shape, 0) // rows
    k_head = lax.broadcasted_iota(jnp.int32, s.shape, 1) % heads
    own = q_head == k_head
    m = jnp.max(jnp.where(own, s, -jnp.inf), axis=-1, keepdims=True)
    p = jnp.where(own, jnp.exp(s - m), 0.0)
    p = p / jnp.sum(p, axis=-1, keepdims=True)
    o = jnp.dot(p.astype(BF16), v2, preferred_element_type=F32)
    o_ref[...] = jnp.concatenate([o[h * rows:(h + 1) * rows] for h in range(heads)], axis=-1).astype(o_ref.dtype)


def mem_attention(q, k_prompt, v_prompt, cache_k, cache_v, layer, *, n_prompt_rows, tm):
    m, d = q.shape
    nm = k_prompt.shape[0]
    ns, heads, hd = cache_k.shape[1], cache_k.shape[3], cache_k.shape[4]
    npc = n_prompt_rows // CHUNK
    att = pl.pallas_call(
        functools.partial(_attn_prompt_body, heads=heads),
        grid=(n_prompt_rows // tm,),
        in_specs=[pl.BlockSpec((tm, d), lambda i: (i, 0)),
                  pl.BlockSpec((nm, d), lambda i: (0, 0)),
                  pl.BlockSpec((nm, d), lambda i: (0, 0))],
        out_specs=pl.BlockSpec((tm, d), lambda i: (i, 0)),
        out_shape=jax.ShapeDtypeStruct((n_prompt_rows, d), BF16),
        compiler_params=_cparams(1),
        name="mem_attention_prompt",
    )(q, k_prompt, v_prompt)
    cache_spec = pl.BlockSpec((None, None, nm, heads, hd), lambda s: (layer, s, 0, 0, 0))
    att_s = pl.pallas_call(
        functools.partial(_attn_sample_body, heads=heads),
        grid=(ns,),
        in_specs=[pl.BlockSpec((CHUNK, d), lambda s: (npc + s, 0)), cache_spec, cache_spec],
        out_specs=pl.BlockSpec((CHUNK, d), lambda s: (s, 0)),
        out_shape=jax.ShapeDtypeStruct((m - n_prompt_rows, d), BF16),
        compiler_params=_cparams(1),
        name="mem_attention_sample",
    )(q, cache_k, cache_v)
    return att, att_s


def _ffn_up_body(x_ref, g_ref, wg_ref, wv_ref, cw_ref, cb_ref, prev_ref, act_ref, stp_ref, sts_ref,
                 xn_ref, *, n_prompt_tiles, nseq, col_splits):
    i = pl.program_id(0)
    j = pl.program_id(1)
    tn = act_ref.shape[1]
    wm1 = stp_ref.shape[2]
    sw = tn // col_splits

    @pl.when((i == 0) & (j == 0))
    def _():
        stp_ref[...] = jnp.zeros(stp_ref.shape, F32)

    @pl.when(j == 0)
    def _():
        xn_ref[...] = _rmsnorm_rows(x_ref[...], g_ref[...]).astype(BF16)

    xn = xn_ref[...]
    sample = jnp.where(i >= n_prompt_tiles, 1, 0)
    chunk = lax.broadcasted_iota(jnp.int32, (nseq, 1, sw), 0)
    from_state = chunk * 0 + sample == 1
    fresh = chunk * (1 - sample) == 0
    hist = []
    for k in range(col_splits):
        cs = slice(k * sw, (k + 1) * sw)
        gate = jnp.dot(xn, wg_ref[0, :, cs], preferred_element_type=F32)
        val = jnp.dot(xn, wv_ref[0, :, cs], preferred_element_type=F32)
        prev = jnp.where(from_state, prev_ref[0, :, :, cs], stp_ref[j, :, :, cs])
        gc = _causal_conv(gate, prev, cw_ref[:, cs], cb_ref[:, cs], nseq, CHUNK, fresh=fresh)
        act_ref[:, cs] = (_silu(gc) * val).astype(act_ref.dtype)
        hist.append(gate.reshape(nseq, CHUNK, sw)[:, CHUNK - wm1:, :])
    hist = jnp.concatenate(hist, axis=-1)

    @pl.when(i < n_prompt_tiles)
    def _():
        stp_ref[j] = hist[nseq - 1:]

    @pl.when(i >= n_prompt_tiles)
    def _():
        sts_ref[j, pl.ds((i - n_prompt_tiles) * nseq, nseq)] = hist


def conv_ffn_up(x, g, w_up, layer, conv_w, conv_b, prev_s, *, n_prompt_rows, tm, tn):
    m, d = x.shape
    ff = w_up.shape[2] // 2
    nj = -(-ff // tn)
    ns, wm1 = prev_s.shape[1], prev_s.shape[2]
    npt = n_prompt_rows // tm
    nseq = tm // CHUNK
    col = lambda j, base=0: pl.multiple_of(base + jnp.minimum(j * tn, ff - tn), LANES)
    assert ff % LANES == 0 and tn % LANES == 0
    el = pl.Element
    return pl.pallas_call(
        functools.partial(_ffn_up_body, n_prompt_tiles=npt, nseq=nseq, col_splits=2),
        grid=(m // tm, nj),
        in_specs=[pl.BlockSpec((tm, d), lambda i, j: (i, 0)),
                  pl.BlockSpec((1, d), lambda i, j: (0, 0)),
                  pl.BlockSpec((el(1), el(d), el(tn)), lambda i, j: (layer, 0, col(j))),
                  pl.BlockSpec((el(1), el(d), el(tn)), lambda i, j: (layer, 0, col(j, ff))),
                  pl.BlockSpec((el(wm1 + 1), el(tn)), lambda i, j: (0, col(j))),
                  pl.BlockSpec((el(1), el(tn)), lambda i, j: (0, col(j))),
                  pl.BlockSpec((1, nseq, wm1, tn), lambda i, j: (j, jnp.maximum(i - npt, 0), 0, 0))],
        out_specs=[pl.BlockSpec((tm, tn), lambda i, j: (i, j)),
                   pl.BlockSpec((nj, 1, wm1, tn), lambda i, j: (0, 0, 0, 0)),
                   pl.BlockSpec((nj, ns, wm1, tn), lambda i, j: (0, 0, 0, 0))],
        out_shape=[jax.ShapeDtypeStruct((m, nj * tn), BF16),
                   jax.ShapeDtypeStruct((nj, 1, wm1, tn), F32),
                   jax.ShapeDtypeStruct((nj, ns, wm1, tn), F32)],
        scratch_shapes=[pltpu.VMEM((tm, d), BF16)],
        compiler_params=_cparams(2),
        name="ffn_up",
    )(x, g.reshape(1, d), w_up, w_up, conv_w, conv_b.reshape(1, ff), prev_s)


def _rglru_body(x_ref, gate_ref, cw_ref, cb_ref, wa_ref, ba_ref, wx_ref, bx_ref, lam_ref,
                cv0_ref, h0_ref, o_ref, cvp_ref, hp_ref, cvs_ref, hs_ref, *, n_prompt_tiles, nseq, heads):
    i = pl.program_id(0)
    tm, ch = x_ref.shape
    bd = ch // heads
    x = x_ref[...].astype(F32)

    def run(prev, h0, n_seq, seqlen, first_row):
        xc = _causal_conv(x, prev, cw_ref[...], cb_ref[...], n_seq, seqlen)
        ra, rx = [], []
        for h in range(heads):
            xh = xc[:, h * bd:(h + 1) * bd].astype(BF16)
            ra.append(jnp.dot(xh, wa_ref[h], preferred_element_type=F32))
            rx.append(jnp.dot(xh, wx_ref[h], preferred_element_type=F32))
        r = _sigmoid(jnp.concatenate(ra, axis=-1) + ba_ref[...])
        ig = _sigmoid(jnp.concatenate(rx, axis=-1) + bx_ref[...])
        log_a = RG_C * r * _log_sigmoid(lam_ref[...])
        a = jnp.exp(log_a)
        m2 = -jnp.tanh(log_a) * (a * a + 1.0)
        mult = jnp.where(m2 > 0.0, m2 * lax.rsqrt(m2), 0.0)
        row = lax.broadcasted_iota(jnp.int32, (n_seq, seqlen, ch), 1)
        if first_row is not None:
            mult = jnp.where(row.reshape(tm, ch) + first_row == 0, 1.0, mult)
        u = xc * ig * mult
        ng = tm // SUBLANES
        ag = a.reshape(ng, SUBLANES, ch)
        ug = u.reshape(ng, SUBLANES, ch)
        grow = lax.broadcasted_iota(jnp.int32, (ng, SUBLANES, ch), 1)
        d = 1
        while d < SUBLANES:
            live = grow >= d
            a_sh = jnp.where(live, pltpu.roll(ag, d, axis=1), 1.0)
            u_sh = jnp.where(live, pltpu.roll(ug, d, axis=1), 0.0)
            ug = ug + ag * u_sh
            ag = ag * a_sh
            d *= 2
        groups_per_seq = seqlen // SUBLANES
        hs, last = [], []
        for gi in range(ng):
            if gi % groups_per_seq == 0:
                carry = h0[gi // groups_per_seq]
            hg = ug[gi] + ag[gi] * carry
            carry = hg[SUBLANES - 1:]
            hs.append(hg)
            if (gi + 1) % groups_per_seq == 0:
                last.append(carry)
        h_all = jnp.concatenate(hs, axis=0)
        o_ref[...] = (h_all * _gelu_tanh(gate_ref[...].astype(F32))).astype(o_ref.dtype)
        x3 = x.reshape(n_seq, seqlen, ch)
        return x3[:, seqlen - prev.shape[1]:, :], jnp.stack(last, axis=0)

    @pl.when(i == 0)
    def _():
        cvp_ref[...] = jnp.zeros(cvp_ref.shape, F32)
        hp_ref[...] = jnp.zeros(hp_ref.shape, F32)

    @pl.when(i < n_prompt_tiles)
    def _():
        cv, hl = run(cvp_ref[...], hp_ref[...], 1, tm, i * tm)
        cvp_ref[...] = cv
        hp_ref[...] = hl

    @pl.when(i >= n_prompt_tiles)
    def _():
        cv, hl = run(cv0_ref[...], h0_ref[...], nseq, CHUNK, None)
        s0 = (i - n_prompt_tiles) * nseq
        cvs_ref[pl.ds(s0, nseq)] = cv
        hs_ref[pl.ds(s0, nseq)] = hl


def rglru(proj, x_col, gate_col, conv_w, conv_b, w_a, b_a, w_x, b_x, lam, cv0, h0, *, n_prompt_rows, tm):
    m = proj.shape[0]
    heads, bd, _ = w_a.shape
    ch = heads * bd
    ns, wm1 = cv0.shape[0], cv0.shape[1]
    npt = n_prompt_rows // tm
    nseq = tm // CHUNK
    samp = lambda i: (jnp.maximum(i - npt, 0), 0, 0)
    const2 = lambda i: (0, 0)
    const3 = lambda i: (0, 0, 0)
    return pl.pallas_call(
        functools.partial(_rglru_body, n_prompt_tiles=npt, nseq=nseq, heads=heads),
        grid=(m // tm,),
        in_specs=[pl.BlockSpec((tm, ch), lambda i: (i, x_col)),
                  pl.BlockSpec((tm, ch), lambda i: (i, gate_col)),
                  pl.BlockSpec((wm1 + 1, ch), const2),
                  pl.BlockSpec((1, ch), const2),
                  pl.BlockSpec((heads, bd, bd), const3),
                  pl.BlockSpec((1, ch), const2),
                  pl.BlockSpec((heads, bd, bd), const3),
                  pl.BlockSpec((1, ch), const2),
                  pl.BlockSpec((1, ch), const2),
                  pl.BlockSpec((nseq, wm1, ch), samp),
                  pl.BlockSpec((nseq, 1, ch), samp)],
        out_specs=[pl.BlockSpec((tm, ch), lambda i: (i, 0)),
                   pl.BlockSpec((1, wm1, ch), const3),
                   pl.BlockSpec((1, 1, ch), const3),
                   pl.BlockSpec((ns, wm1, ch), const3),
                   pl.BlockSpec((ns, 1, ch), const3)],
        out_shape=[jax.ShapeDtypeStruct((m, ch), BF16),
                   jax.ShapeDtypeStruct((1, wm1, ch), F32),
                   jax.ShapeDtypeStruct((1, 1, ch), F32),
                   jax.ShapeDtypeStruct((ns, wm1, ch), F32),
                   jax.ShapeDtypeStruct((ns, 1, ch), F32)],
        compiler_params=_cparams(1),
        name="rglru",
    )(proj, proj, conv_w, conv_b.reshape(1, ch), w_a.astype(BF16), b_a.reshape(1, ch),
      w_x.astype(BF16), b_x.reshape(1, ch), lam.reshape(1, ch), cv0, h0)


def _gated_chunk(q, k, g, v, st_ref, slot, kb_ref, kb0, heads):
    c = q.shape[0]
    dk = q.shape[1] // heads
    dv = v.shape[1] // heads
    nsub = c // SUB
    b = _dot01_left(_tril01(c), g) * LOG2_E
    b_end = b[c - 1:c, :]
    qb = (q * jnp.exp2(b)).astype(BF16)
    k_end = (k * jnp.exp2(b_end - b)).astype(BF16)
    dec_end = jnp.exp2(b_end)
    vb = v.astype(BF16)
    kb_ref[kb0] = k
    kb_ref[kb0 + 1] = b
    lane = lax.broadcasted_iota(jnp.int32, (SUB, c), 1)
    srow = lax.broadcasted_iota(jnp.int32, (SUB, c), 0)
    lane_h = lax.broadcasted_iota(jnp.int32, (SUB // 2, c), 1)
    outs = []
    for h in range(heads):
        sk = slice(h * dk, (h + 1) * dk)
        sv = slice(h * dv, (h + 1) * dv)
        st = st_ref[slot, h]
        o = lax.dot_general(qb[:, sk], st.astype(BF16), NT_DIMS, preferred_element_type=F32)
        qh, kh, bh, vh = q[:, sk], k[:, sk], b[:, sk], vb[:, sv]
        blocks = []
        for i in range(nsub):
            r0 = i * SUB
            qi, ki, bi = qh[r0:r0 + SUB], kh[r0:r0 + SUB], bh[r0:r0 + SUB]
            half = SUB // 2
            tops, bots = jnp.zeros((half, c), F32), jnp.zeros((half, c), F32)
            for s in range(SUB):
                ks = kb_ref[kb0, r0 + s:r0 + s + 1, sk]
                bs = kb_ref[kb0 + 1, r0 + s:r0 + s + 1, sk]
                if s < half:
                    col = jnp.sum(qi[:half] * (ks * jnp.exp2(bi[:half] - bs)), axis=-1, keepdims=True)
                    tops = jnp.where(lane_h == r0 + s, col, tops)
                col = jnp.sum(qi[half:] * (ks * jnp.exp2(bi[half:] - bs)), axis=-1, keepdims=True)
                bots = jnp.where(lane_h == r0 + s, col, bots)
            a = jnp.where(srow >= lane - r0, jnp.concatenate([tops, bots], axis=0), 0.0)
            if i > 0:
                ri = bh[r0:r0 + 1]
                qt = (qi * jnp.exp2(bi - ri)).astype(BF16)
                kt = jnp.concatenate([kh[:r0] * jnp.exp2(ri - bh[:r0]),
                                      jnp.zeros((c - r0, dk), F32)], axis=0).astype(BF16)
                a = a + lax.dot_general(qt, kt, NT_DIMS, preferred_element_type=F32)
            blocks.append(a)
        a_full = jnp.concatenate(blocks, axis=0).astype(BF16)
        o = o + jnp.dot(a_full, vh, preferred_element_type=F32)
        st_ref[slot, h] = st * dec_end[:, sk] + lax.dot_general(
            vh, k_end[:, sk], TN_DIMS, preferred_element_type=F32)
        outs.append(o)
    return outs


def _head_norm_gate(outs, ng_ref, gate):
    res = []
    for h, o in enumerate(outs):
        dv = o.shape[-1]
        sv = slice(h * dv, (h + 1) * dv)
        res.append(_rmsnorm_rows(o, ng_ref[:, sv]) * _silu(gate[:, sv].astype(F32)))
    return jnp.concatenate(res, axis=-1)


def _chunk_steps(states, n_prompt_steps, one_chunk):
    c = pl.program_id(0)

    @pl.when((c == 0) | (c >= n_prompt_steps))
    def _():
        for st_ref, s0_ref in states:
            st_ref[...] = s0_ref[...]

    sample = jnp.where(c >= n_prompt_steps, 1, 0)
    for k in range(CHUNKS_PER_STEP):
        one_chunk(slice(k * CHUNK, (k + 1) * CHUNK), sample * k, k)


def _hgrn_body(q_ref, f_ref, v_ref, gate_ref, lb_ref, ng_ref, s0_ref, o_ref, st_ref, kb_ref,
               *, n_prompt_steps, heads):
    def one_chunk(rows, slot, k_in_step):
        kk = (1.0 - lb_ref[...]) / (1.0 + jnp.exp(f_ref[rows, :].astype(F32)))
        g = jnp.log1p(-kk)
        outs = _gated_chunk(_silu(q_ref[rows, :].astype(F32)), kk, g, v_ref[rows, :], st_ref, slot,
                            kb_ref, 2 * k_in_step, heads)
        o_ref[rows, :] = _head_norm_gate(outs, ng_ref, gate_ref[rows, :]).astype(o_ref.dtype)

    _chunk_steps(((st_ref, s0_ref),), n_prompt_steps, one_chunk)


def _gla_body(q_ref, k_ref, v_ref, gate_ref, gl_ref, gkw_ref, gkb_ref, ng_ref, s0_ref, o_ref, st_ref, kb_ref,
              *, n_prompt_steps, heads):
    dk = q_ref.shape[1] // heads

    def one_chunk(rows, slot, k_in_step):
        z = jnp.dot(gl_ref[rows, :].astype(BF16), gkw_ref[...], preferred_element_type=F32) + gkb_ref[...]
        g = _log_sigmoid(z) * (1.0 / D_GATE_NORM)
        outs = _gated_chunk(q_ref[rows, :].astype(F32) * dk ** -0.5, k_ref[rows, :].astype(F32), g, v_ref[rows, :],
                            st_ref, slot, kb_ref, 2 * k_in_step, heads)
        o_ref[rows, :] = _head_norm_gate(outs, ng_ref, gate_ref[rows, :]).astype(o_ref.dtype)

    _chunk_steps(((st_ref, s0_ref),), n_prompt_steps, one_chunk)


def _chunk_state_map(n_prompt_chunks, ndim):
    return lambda c: (jnp.maximum(c - (n_prompt_chunks - 1), 0),) + (0,) * (ndim - 1)


def hgrn2(proj, lb, norm_g, s0, *, n_prompt_chunks):
    m = proj.shape[0]
    _, heads, dv, dk = s0.shape
    kw, vw = heads * dk, heads * dv
    n_prompt_steps = n_prompt_chunks // CHUNKS_PER_STEP
    smap = _chunk_state_map(n_prompt_steps, 4)
    row = lambda c: (0, 0)
    return pl.pallas_call(
        functools.partial(_hgrn_body, n_prompt_steps=n_prompt_steps, heads=heads),
        grid=(m // STEP_ROWS,),
        in_specs=[pl.BlockSpec((STEP_ROWS, kw), lambda c: (c, 0)),
                  pl.BlockSpec((STEP_ROWS, kw), lambda c: (c, 1)),
                  pl.BlockSpec((STEP_ROWS, vw), lambda c: (c, 2 * kw // vw)),
                  pl.BlockSpec((STEP_ROWS, vw), lambda c: (c, 2 * kw // vw + 1)),
                  pl.BlockSpec((1, kw), row),
                  pl.BlockSpec((1, vw), row),
                  pl.BlockSpec((CHUNKS_PER_STEP, heads, dv, dk), smap)],
        out_specs=[pl.BlockSpec((STEP_ROWS, vw), lambda c: (c, 0)),
                   pl.BlockSpec((CHUNKS_PER_STEP, heads, dv, dk), smap)],
        out_shape=[jax.ShapeDtypeStruct((m, vw), BF16),
                   jax.ShapeDtypeStruct(s0.shape, F32)],
        scratch_shapes=[pltpu.VMEM((2 * CHUNKS_PER_STEP, CHUNK, kw), F32)],
        compiler_params=_cparams(1),
        name="hgrn2",
    )(proj, proj, proj, proj, lb.reshape(1, kw), norm_g.reshape(1, vw), s0)


def gla(proj, cols, gk_w, gk_b, norm_g, s0, *, n_prompt_chunks):
    m = proj.shape[0]
    _, heads, dv, dk = s0.shape
    kw, vw = heads * dk, heads * dv
    cq, ck, cv, cg, cgl = cols
    n_prompt_steps = n_prompt_chunks // CHUNKS_PER_STEP
    smap = _chunk_state_map(n_prompt_steps, 4)
    row = lambda c: (0, 0)
    return pl.pallas_call(
        functools.partial(_gla_body, n_prompt_steps=n_prompt_steps, heads=heads),
        grid=(m // STEP_ROWS,),
        in_specs=[pl.BlockSpec((STEP_ROWS, kw), lambda c: (c, cq)),
                  pl.BlockSpec((STEP_ROWS, kw), lambda c: (c, ck)),
                  pl.BlockSpec((STEP_ROWS, vw), lambda c: (c, cv)),
                  pl.BlockSpec((STEP_ROWS, vw), lambda c: (c, cg)),
                  pl.BlockSpec((STEP_ROWS, LANES), lambda c: (c, cgl)),
                  pl.BlockSpec((LANES, kw), row),
                  pl.BlockSpec((1, kw), row),
                  pl.BlockSpec((1, vw), row),
                  pl.BlockSpec((CHUNKS_PER_STEP, heads, dv, dk), smap)],
        out_specs=[pl.BlockSpec((STEP_ROWS, vw), lambda c: (c, 0)),
                   pl.BlockSpec((CHUNKS_PER_STEP, heads, dv, dk), smap)],
        out_shape=[jax.ShapeDtypeStruct((m, vw), BF16),
                   jax.ShapeDtypeStruct(s0.shape, F32)],
        scratch_shapes=[pltpu.VMEM((2 * CHUNKS_PER_STEP, CHUNK, kw), F32)],
        compiler_params=_cparams(1),
        name="gla",
    )(proj, proj, proj, proj, proj, gk_w, gk_b.reshape(1, kw), norm_g.reshape(1, vw), s0)


def _ssd_body(z_ref, x_ref, bc_ref, dt_ref, cw_ref, cb_ref, dtb_ref, alog_ref, dsk_ref, ng_ref,
              cv0_ref, s0_ref, o_ref, cv_ref, st_ref, *, n_prompt_steps, heads, groups):
    def one_chunk(rws, slot, _):
        _ssd_chunk(rws, slot, z_ref, x_ref, bc_ref, dt_ref, cw_ref, cb_ref, dtb_ref, alog_ref, dsk_ref, ng_ref,
                   o_ref, cv_ref, st_ref, heads=heads, groups=groups)

    _chunk_steps(((st_ref, s0_ref), (cv_ref, cv0_ref)), n_prompt_steps, one_chunk)


def _ssd_chunk(rws, slot, z_ref, x_ref, bc_ref, dt_ref, cw_ref, cb_ref, dtb_ref, alog_ref, dsk_ref, ng_ref,
               o_ref, cv_ref, st_ref, *, heads, groups):
    rows, xw_ = CHUNK, x_ref.shape[1]
    hp = xw_ // heads
    n = bc_ref.shape[1] // (2 * groups)
    gw = xw_ // groups
    x_raw, bc_raw = x_ref[rws, :].astype(F32), bc_ref[rws, :].astype(F32)
    prev = cv_ref[pl.ds(slot, 1)]
    cw, cb = cw_ref[...], cb_ref[...]
    xs = _silu(_causal_conv(x_raw, prev[:, :, :xw_], cw[:, :xw_], cb[:, :xw_], 1, rows))
    bcs = _silu(_causal_conv(bc_raw, prev[:, :, xw_:], cw[:, xw_:], cb[:, xw_:], 1, rows))
    wm1 = prev.shape[1]
    cv_ref[slot, :, :xw_] = x_raw[rows - wm1:, :]
    cv_ref[slot, :, xw_:] = bc_raw[rows - wm1:, :]

    erow = lax.broadcasted_iota(jnp.int32, (LANES, xw_), 0)
    ecol = lax.broadcasted_iota(jnp.int32, (LANES, xw_), 1)
    expand = jnp.where(ecol // hp == erow, 1.0, 0.0).astype(BF16)
    dt = _softplus(_dot01_right(dt_ref[rws, :].astype(F32), expand) + dtb_ref[...])
    dta = dt * (-jnp.exp(alog_ref[...]))
    cum = _dot01_left(_tril01(rows), dta)
    cum_end = cum[rows - 1:rows, :]
    xdt = xs * dt
    xdt_b = xdt.astype(BF16)
    x_end = (xdt * jnp.exp(cum_end - cum)).astype(BF16)
    e_cum = jnp.exp(cum)
    dec_end = jnp.exp(cum_end)

    prow = lax.broadcasted_iota(jnp.int32, (rows, LANES), 0)
    plane = lax.broadcasted_iota(jnp.int32, (rows, LANES), 1)
    assert hp == rows and LANES % hp == 0
    pair = LANES // hp
    psrc = plane % hp
    y_parts = []
    for g in range(groups):
        bg = bcs[:, g * n:(g + 1) * n].astype(BF16)
        cg = bcs[:, (groups + g) * n:(groups + g + 1) * n].astype(BF16)
        gl = slice(g * gw, (g + 1) * gw)
        st = st_ref[slot, :, gl]
        y_inter = jnp.dot(cg, st.astype(BF16), preferred_element_type=F32) * e_cum[:, gl]
        cb_rep = lax.dot_general(cg, jnp.concatenate([bg] * pair, axis=0), NT_DIMS, preferred_element_type=F32)
        y_intra = []
        for sl in range(gw // LANES):
            lo = g * gw + sl * LANES
            cs = cum[:, lo:lo + LANES]
            cdiag = jnp.sum(jnp.where(prow == psrc, cs, 0.0), axis=0, keepdims=True)
            decay = jnp.exp(jnp.minimum(cs - cdiag, 0.0))
            mt = jnp.where(prow >= psrc, cb_rep * decay, 0.0).astype(BF16)
            xp = xdt_b[:, lo:lo + LANES]
            rhs = jnp.concatenate(
                [jnp.where(plane // hp == p, xp, jnp.zeros_like(xp)) for p in range(pair)], axis=0)
            y_intra.append(jnp.dot(mt, rhs, preferred_element_type=F32))
        y_parts.append(jnp.concatenate(y_intra, axis=-1) + y_inter)
        st_ref[slot, :, gl] = st * dec_end[:, gl] + lax.dot_general(
            bg, x_end[:, gl], TN_DIMS, preferred_element_type=F32)
    y = jnp.concatenate(y_parts, axis=-1) + dsk_ref[...] * xs
    yz = y * _silu(z_ref[rws, :].astype(F32))
    res = []
    for g in range(groups):
        gl = slice(g * gw, (g + 1) * gw)
        res.append(_rmsnorm_rows(yz[:, gl], ng_ref[:, gl]))
    o_ref[rws, :] = jnp.concatenate(res, axis=-1).astype(o_ref.dtype)


def ssd(proj, cols, conv_w, conv_b, dt_bias_x, a_log_x, d_skip_x, norm_g, cv0, s0, *, n_prompt_chunks):
    m = proj.shape[0]
    xw_ = s0.shape[2]
    n = s0.shape[1]
    bcw = 2 * C_GROUPS * n
    cz, cx, cbc, cdt = cols
    wm1 = cv0.shape[1]
    n_prompt_steps = n_prompt_chunks // CHUNKS_PER_STEP
    smap3 = _chunk_state_map(n_prompt_steps, 3)
    row = lambda c: (0, 0)
    return pl.pallas_call(
        functools.partial(_ssd_body, n_prompt_steps=n_prompt_steps, heads=C_HEADS, groups=C_GROUPS),
        grid=(m // STEP_ROWS,),
        in_specs=[pl.BlockSpec((STEP_ROWS, xw_), lambda c: (c, cz)),
                  pl.BlockSpec((STEP_ROWS, xw_), lambda c: (c, cx)),
                  pl.BlockSpec((STEP_ROWS, bcw), lambda c: (c, cbc)),
                  pl.BlockSpec((STEP_ROWS, LANES), lambda c: (c, cdt)),
                  pl.BlockSpec((wm1 + 1, xw_ + bcw), row),
                  pl.BlockSpec((1, xw_ + bcw), row),
                  pl.BlockSpec((1, xw_), row),
                  pl.BlockSpec((1, xw_), row),
                  pl.BlockSpec((1, xw_), row),
                  pl.BlockSpec((1, xw_), row),
                  pl.BlockSpec((CHUNKS_PER_STEP, wm1, xw_ + bcw), smap3),
                  pl.BlockSpec((CHUNKS_PER_STEP, n, xw_), smap3)],
        out_specs=[pl.BlockSpec((STEP_ROWS, xw_), lambda c: (c, 0)),
                   pl.BlockSpec((CHUNKS_PER_STEP, wm1, xw_ + bcw), smap3),
                   pl.BlockSpec((CHUNKS_PER_STEP, n, xw_), smap3)],
        out_shape=[jax.ShapeDtypeStruct((m, xw_), BF16),
                   jax.ShapeDtypeStruct(cv0.shape, F32),
                   jax.ShapeDtypeStruct(s0.shape, F32)],
        compiler_params=_cparams(1),
        name="ssd",
    )(proj, proj, proj, proj, conv_w, conv_b.reshape(1, xw_ + bcw), dt_bias_x.reshape(1, xw_),
      a_log_x.reshape(1, xw_), d_skip_x.reshape(1, xw_), norm_g.reshape(1, xw_), cv0, s0)


def _with_zero_first(s, n=1):
    return jnp.concatenate([jnp.zeros((n,) + s.shape[1:], s.dtype), s], axis=0)


def _drop_unused_prompt_slots(st):
    return jnp.concatenate([st[:1], st[CHUNKS_PER_STEP:]], axis=0)


def _col_tile_starts(width, tn):
    return [min(j * tn, width - tn) for j in range(-(-width // tn))]


def _block_cols(a, width, tn):
    return jnp.stack([a[..., o:o + tn] for o in _col_tile_starts(width, tn)], axis=0)


def _unblock_cols(b, width, tn):
    return jnp.concatenate([b[j][..., j * tn - o:] for j, o in enumerate(_col_tile_starts(width, tn))], axis=-1)


def kernel(x_prompt, x_sample, mem_prompt, cache_mem_k, cache_mem_v, state_hgrn, state_rglru, state_rg_conv, state_ssd, state_ssd_conv, state_gla, state_ffn_conv, norm_mix, norm_xattn, norm_mem, norm_ffn, ev_w_in, hgrn_lb_logits, hgrn_norm, rg_conv_w, rg_conv_b, rg_w_a, rg_b_a, rg_w_x, rg_b_x, rg_lambda, ev_w_out, od_w_in, ssd_conv_w, ssd_conv_b, ssd_dt_bias, ssd_A_log, ssd_D, ssd_norm, gla_gk_w, gla_gk_b, gla_norm, od_w_out, xa_wq, xa_wk, xa_wv, xa_wo, ffn_up, ffn_conv_w, ffn_conv_b, ffn_down, final_norm):
    bp, p_rows, d = x_prompt.shape
    ns, s_len, _ = x_sample.shape
    assert bp == 1 and s_len == CHUNK and p_rows % STEP_ROWS == 0 and ns % CHUNKS_PER_STEP == 0
    depth = norm_mix.shape[0]
    s_rows = ns * s_len
    m = p_rows + s_rows
    npc = p_rows // CHUNK
    nm = mem_prompt.shape[1]
    tm = math.gcd(math.gcd(p_rows, s_rows), 1024)
    tm_seq = math.gcd(tm, 256)

    x = (x_prompt.reshape(p_rows, d), x_sample.reshape(s_rows, d))
    lb_all = jnp.cumsum(jax.nn.softmax(hgrn_lb_logits.astype(F32), axis=0), axis=0)
    ev_w_in_b, ev_w_out_b, od_w_out_b = ev_w_in.astype(BF16), ev_w_out.astype(BF16), od_w_out.astype(BF16)
    wq_b, wk_b, wv_b, wo_b = xa_wq.astype(BF16), xa_wk.astype(BF16), xa_wv.astype(BF16), xa_wo.astype(BF16)
    w_up_b, w_down_b = ffn_up.astype(BF16), ffn_down.astype(BF16)

    a_kw = hgrn_lb_logits.shape[1]
    a_vw = hgrn_norm.shape[1]
    b_w = rg_lambda.shape[1]
    c_w = ssd_norm.shape[1]
    c_bc = 2 * C_GROUPS * C_STATE
    c_hd = c_w // C_HEADS
    d_val = gla_norm.shape[1]
    d_key = gla_gk_b.shape[1]
    d_rank = gla_gk_w.shape[1]
    ff = ffn_conv_b.shape[1]
    ffn_tn = 512
    in_tn = 1536
    tm_full = tm // 2

    mem_k, mem_v = [], []
    hg, rl, rc, ss, sc, gs, fc_p, fc_s = [], [], [], [], [], [], [], []
    for l in range(depth):
        j = l // 2
        if l % 2 == 0:
            proj = norm_matmul(x, norm_mix[l], ev_w_in_b, j, tm=tm, tn=in_tn, out_dtype=PROJ_DTYPE)
            s0 = _with_zero_first(jnp.swapaxes(state_hgrn[j], -1, -2), CHUNKS_PER_STEP)
            o_a, st = hgrn2(proj, lb_all[l], hgrn_norm[j], s0, n_prompt_chunks=npc)
            hg.append(jnp.swapaxes(_drop_unused_prompt_slots(st), -1, -2))
            xcol = (2 * a_kw + 2 * a_vw) // b_w
            o_b, cvp, hp_, cvs, hs_ = rglru(
                proj, xcol, xcol + 1, rg_conv_w[j], rg_conv_b[j], rg_w_a[j], rg_b_a[j], rg_w_x[j], rg_b_x[j],
                rg_lambda[j], state_rg_conv[j], state_rglru[j].reshape(ns, 1, b_w), n_prompt_rows=p_rows, tm=tm_seq)
            rc.append((cvp, cvs))
            rl.append((hp_.reshape(1, b_w), hs_.reshape(ns, b_w)))
            x = matmul_res([(o_a, 0, a_vw, 0), (o_b, 0, b_w, a_vw)], ev_w_out_b, j, x, tm=tm_full, tn=d)
        else:
            w = od_w_in[j]
            offs = np_cumsum_offsets((c_w, c_w + c_bc, C_HEADS, d_key, d_key, d_val, d_val, d_rank))
            o_z, o_xbc, o_dt, o_q, o_k, o_v, o_g, o_gl = offs
            pad = lambda a: jnp.pad(a, ((0, 0), (0, LANES - a.shape[1])))
            w_re = jnp.concatenate([
                w[:, o_z:o_z + c_w], w[:, o_xbc:o_xbc + c_w], w[:, o_xbc + c_w:o_xbc + c_w + c_bc],
                w[:, o_q:o_q + d_key], w[:, o_k:o_k + d_key],
                pad(w[:, o_dt:o_dt + C_HEADS]), pad(w[:, o_gl:o_gl + d_rank]),
                jnp.zeros((d, 2 * LANES), w.dtype),
                w[:, o_v:o_v + d_val], w[:, o_g:o_g + d_val]], axis=1).astype(BF16)
            proj = norm_matmul(x, norm_mix[l], w_re, tm=tm, tn=in_tn, out_dtype=PROJ_DTYPE)
            assert c_w == d_val
            base = 2 * c_w + c_bc
            col_dt = (base + 2 * d_key) // LANES
            cv0 = _with_zero_first(state_ssd_conv[j], CHUNKS_PER_STEP)
            s0c = _with_zero_first(jnp.transpose(state_ssd[j], (0, 3, 1, 2)).reshape(ns, C_STATE, c_w),
                                   CHUNKS_PER_STEP)
            rep = lambda a: jnp.repeat(a, c_hd)
            o_c, cv, stc = ssd(proj, (0, 1, 2 * c_w // c_bc, col_dt), ssd_conv_w[j], ssd_conv_b[j],
                               rep(ssd_dt_bias[j]), rep(ssd_A_log[j]), rep(ssd_D[j]), ssd_norm[j], cv0, s0c,
                               n_prompt_chunks=npc)
            sc.append(_drop_unused_prompt_slots(cv))
            stc = _drop_unused_prompt_slots(stc)
            ss.append(jnp.transpose(stc.reshape(1 + ns, C_STATE, C_HEADS, c_hd), (0, 2, 3, 1)))
            s0d = _with_zero_first(jnp.swapaxes(state_gla[j], -1, -2), CHUNKS_PER_STEP)
            gkw = jnp.pad(gla_gk_w[j], ((0, LANES - d_rank), (0, 0))).astype(BF16)
            v_start = base + 2 * d_key + 4 * LANES
            o_d, std = gla(proj, (base // d_key, base // d_key + 1, v_start // d_val, v_start // d_val + 1, col_dt + 1),
                           gkw, gla_gk_b[j], gla_norm[j], s0d, n_prompt_chunks=npc)
            gs.append(jnp.swapaxes(_drop_unused_prompt_slots(std), -1, -2))
            x = matmul_res([(o_c, 0, c_w, 0), (o_d, 0, d_val, c_w)], od_w_out_b, j, x, tm=tm_full, tn=d)

        mem = mem_prompt.reshape(nm, d)
        k_p = norm_matmul(mem, norm_mem[l], wk_b, l, tm=nm, tn=512, out_dtype=F32)
        v_p = norm_matmul(mem, norm_mem[l], wv_b, l, tm=nm, tn=512, out_dtype=F32)
        mem_k.append(k_p.reshape(1, nm, MEM_HEADS, d // MEM_HEADS))
        mem_v.append(v_p.reshape(1, nm, MEM_HEADS, d // MEM_HEADS))
        q = norm_matmul(x, norm_xattn[l], wq_b, l, tm=tm, tn=d, out_dtype=BF16)
        att = mem_attention(q, k_p, v_p, cache_mem_k, cache_mem_v, l, n_prompt_rows=p_rows, tm=tm)
        x = matmul_res([(att, 0, d, 0)], wo_b, l, x, tm=tm_full, tn=d)

        prev_s = _block_cols(state_ffn_conv[l], ff, ffn_tn)
        act, stp, sts = conv_ffn_up(x, norm_ffn[l], w_up_b, l, ffn_conv_w[l], ffn_conv_b[l],
                                    prev_s, n_prompt_rows=p_rows, tm=tm, tn=ffn_tn)
        fc_p.append(_unblock_cols(stp, ff, ffn_tn))
        fc_s.append(_unblock_cols(sts, ff, ffn_tn))
        starts = _col_tile_starts(ff, ffn_tn)
        body = (len(starts) - 1) * ffn_tn
        down_terms = [(act, 0, body, 0), (act, body + (body - starts[-1]), ff - body, body)]
        x = matmul_res(down_terms, w_down_b, l, x, tm=tm, tn=512)

    y_prompt, y_sample = rmsnorm_rows(x, final_norm, n_prompt_rows=p_rows, tm=tm)
    y_prompt = y_prompt.reshape(1, p_rows, d)
    y_sample = y_sample.reshape(ns, s_len, d)
    stack_p = lambda lst: jnp.stack([a[:1] for a in lst])
    stack_s = lambda lst: jnp.stack([a[1:] for a in lst])
    return (y_prompt, y_sample, jnp.stack(mem_k), jnp.stack(mem_v),
            stack_p(hg), jnp.stack([a[0] for a in rl]), jnp.stack([a[0] for a in rc]),
            stack_p(ss), stack_p(sc), stack_p(gs), jnp.stack(fc_p),
            stack_s(hg), jnp.stack([a[1] for a in rl]), jnp.stack([a[1] for a in rc]),
            stack_s(ss), stack_s(sc), stack_s(gs), jnp.stack(fc_s))


def np_cumsum_offsets(sizes):
    offs, acc = [], 0
    for s in sizes:
        offs.append(acc)
        acc += s
    return offs
```
